```python
import math
import jax, jax.numpy as jnp
from jax import lax
import numpy as np

D_MODEL = 1024
BATCH = 4
SEQ = 8192
DEPTH = 2

CHUNK = 64
D_PLE = 256
BRANCH_W = 512
D_MIX = 3 * BRANCH_W
RWKV_HEADS = 8
RWKV_HEAD_DIM = BRANCH_W // RWKV_HEADS
DECAY_LORA = 64
ICLR_LORA = 64
RWKV_GN_EPS = 64e-5
S5_GROUP = 16
S5_GROUPS = BRANCH_W // S5_GROUP
S5_STATE = 64
LRU_BLOCKS = 8
LRU_BLOCK_DIM = BRANCH_W // LRU_BLOCKS
CONV_WIDTH = 4
LRU_C = 8.0
NORM_EPS = 1e-6
RWKV_SHIFT_W = 3 * BRANCH_W + DECAY_LORA + ICLR_LORA
SPLITS = [RWKV_SHIFT_W,
          RWKV_SHIFT_W + BRANCH_W,
          RWKV_SHIFT_W + 2 * BRANCH_W,
          RWKV_SHIFT_W + 3 * BRANCH_W,
          RWKV_SHIFT_W + 4 * BRANCH_W]
D_IN = RWKV_SHIFT_W + 5 * BRANCH_W

kernel_name = "hymba_rwkv7_s5_rglru_ple"


def rmsnorm(x, g):
    xf = x.astype(jnp.float32)
    return xf * lax.rsqrt(jnp.mean(xf * xf, axis=-1, keepdims=True) + NORM_EPS) * g.astype(jnp.float32)


def linear_binop(e1, e2):
    a1, b1 = e1
    a2, b2 = e2
    return a1 * a2, a2 * b1 + b2


def rwkv7_group(z, gate, mu, w0, w2, a0, a2, k_k, k_a, r_k, ln_w, ln_b):
    bsz, seq, _ = z.shape
    z = z.astype(jnp.float32)
    z_prev = jnp.pad(z, ((0, 0), (1, 0), (0, 0)))[:, :-1]
    z = z + mu * (z_prev - z)
    r, k, v, wd, ad = jnp.split(z, [BRANCH_W, 2 * BRANCH_W, 3 * BRANCH_W, 3 * BRANCH_W + DECAY_LORA], axis=-1)
    w = -jax.nn.softplus(-(w0 + jnp.tanh(wd) @ w2)) - 0.5
    log_decay = -jnp.exp(w)
    a = jax.nn.sigmoid(a0 + ad @ a2)
    heads = lambda t: t.reshape(bsz, seq, RWKV_HEADS, RWKV_HEAD_DIM)
    kk = heads(k * k_k)
    kk = kk / jnp.maximum(jnp.sqrt(jnp.sum(kk * kk, axis=-1, keepdims=True)), 1e-12)
    k = k * (1.0 + (a - 1.0) * k_a)
    r_h, k_h, v_h, a_h, dec_h = heads(r), heads(k), heads(v), heads(a), heads(jnp.exp(log_decay))
    b_h = kk * a_h

    def step(S, inp):
        r_t, k_t, v_t, kk_t, b_t, d_t = inp
        S = (S * d_t[:, :, None, :]
             - jnp.einsum('bhvk,bhk->bhv', S, kk_t)[..., None] * b_t[:, :, None, :]
             + v_t[..., None] * k_t[:, :, None, :])
        return S, jnp.einsum('bhvk,bhk->bhv', S, r_t)

    tm = lambda t: jnp.moveaxis(t, 1, 0)
    S0 = jnp.zeros((bsz, RWKV_HEADS, RWKV_HEAD_DIM, RWKV_HEAD_DIM), jnp.float32)
    _, y = lax.scan(step, S0, (tm(r_h), tm(k_h), tm(v_h), tm(kk), tm(b_h), tm(dec_h)))
    y = jnp.moveaxis(y, 0, 1)
    mean = jnp.mean(y, axis=-1, keepdims=True)
    var = jnp.mean(jnp.square(y - mean), axis=-1, keepdims=True)
    y = ((y - mean) * lax.rsqrt(var + RWKV_GN_EPS)).reshape(bsz, seq, BRANCH_W) * ln_w + ln_b
    bonus = jnp.sum(r_h * k_h * r_k, axis=-1, keepdims=True) * v_h
    y = y + bonus.reshape(bsz, seq, BRANCH_W)
    return y * jax.nn.silu(gate.astype(jnp.float32))


def s5_group(u, gate, a_re, a_im, log_dt, b_re, b_im, c_re, c_im, d, glu_w, glu_b):
    bsz, seq, _ = u.shape
    f32 = jnp.float32
    u = u.astype(f32)
    lam = lax.complex(a_re.astype(f32), a_im.astype(f32))
    dt = jnp.exp(log_dt.astype(f32))[:, None]
    lam_bar = jnp.exp(lam * dt)
    b_bar = ((lam_bar - 1.0) / lam)[:, :, None] * lax.complex(b_re.astype(f32), b_im.astype(f32))
    c = lax.complex(c_re.astype(f32), c_im.astype(f32))
    steps = jnp.arange(1, CHUNK + 1, dtype=f32)[:, None, None]
    pows = jnp.exp(lam * dt * steps)
    lam_b = jnp.broadcast_to(lam_bar, (bsz, CHUNK, S5_GROUPS, S5_STATE))
    ug = u.reshape(bsz, seq // CHUNK, CHUNK, S5_GROUPS, S5_GROUP)
    ug = jnp.moveaxis(ug, 1, 0)

    def chunk_step(state, u_c):
        bu = jnp.einsum('bcgh,gph->bcgp', u_c, b_bar)
        _, xs = lax.associative_scan(linear_binop, (lam_b, bu), axis=1)
        xs = xs + pows[None] * state[:, None]
        y = jnp.real(jnp.einsum('bcgp,ghp->bcgh', xs, c))
        return xs[:, -1], y

    state0 = jnp.zeros((bsz, S5_GROUPS, S5_STATE), jnp.complex64)
    _, y = lax.scan(chunk_step, state0, ug)
    y = jnp.moveaxis(y, 0, 1).reshape(bsz, seq, BRANCH_W) + d * u
    zg = jax.nn.gelu(y)
    zg = zg * jax.nn.sigmoid(zg @ glu_w + glu_b)
    return zg * jax.nn.silu(gate.astype(f32))


def rglru_group(xb, gate, conv_w, conv_b, wa, ba, wx, bx, lam):
    bsz, seq, _ = xb.shape
    xb = xb.astype(jnp.float32)
    xp = jnp.pad(xb, ((0, 0), (CONV_WIDTH - 1, 0), (0, 0)))
    xc = conv_b + sum(xp[:, j:j + seq] * conv_w[j] for j in range(CONV_WIDTH))
    xh = xc.reshape(bsz, seq, LRU_BLOCKS, LRU_BLOCK_DIM)
    r = jax.nn.sigmoid(jnp.einsum('blhi,hij->blhj', xh, wa).reshape(bsz, seq, BRANCH_W) + ba)
    i = jax.nn.sigmoid(jnp.einsum('blhi,hij->blhj', xh, wx).reshape(bsz, seq, BRANCH_W) + bx)
    log_a = -LRU_C * r * jax.nn.softplus(-lam)
    a = jnp.exp(log_a)
    mult = jnp.sqrt(-jnp.expm1(2.0 * log_a))
    _, h = lax.associative_scan(linear_binop, (a, mult * (i * xc)), axis=1)
    return h * jax.nn.silu(gate.astype(jnp.float32))


def hybrid_layer(h, p_i, norm_g, w_in,
                 rwkv_mu, rwkv_w0, rwkv_w2, rwkv_a0, rwkv_a2, rwkv_k_k, rwkv_k_a, rwkv_r_k, rwkv_ln_w, rwkv_ln_b,
                 s5_a_re, s5_a_im, s5_log_dt, s5_b_re, s5_b_im, s5_c_re, s5_c_im, s5_d, s5_glu_w, s5_glu_b,
                 lru_conv_w, lru_conv_b, lru_wa, lru_ba, lru_wx, lru_bx, lru_lambda,
                 w_out, ple_w, ple_norm_g, ple_gate_w):
    xn = rmsnorm(h, norm_g)
    zin = xn @ w_in.astype(jnp.float32)
    z_rw, g_rw, u_s5, g_s5, x_lru, g_lru = jnp.split(zin, SPLITS, axis=-1)
    y_rw = rwkv7_group(z_rw, g_rw, rwkv_mu, rwkv_w0, rwkv_w2, rwkv_a0, rwkv_a2,
                       rwkv_k_k, rwkv_k_a, rwkv_r_k, rwkv_ln_w, rwkv_ln_b)
    y_s5 = s5_group(u_s5, g_s5, s5_a_re, s5_a_im, s5_log_dt, s5_b_re, s5_b_im,
                    s5_c_re, s5_c_im, s5_d, s5_glu_w, s5_glu_b)
    y_lru = rglru_group(x_lru, g_lru, lru_conv_w, lru_conv_b, lru_wa, lru_ba, lru_wx, lru_bx, lru_lambda)
    y = jnp.concatenate([y_rw, y_s5, y_lru], axis=-1) @ w_out.astype(jnp.float32)
    h = h + y
    e = rmsnorm(p_i @ ple_w, ple_norm_g)
    return h + e * jax.nn.sigmoid(h @ ple_gate_w.astype(jnp.float32))


def setup_inputs(seed: int = 0) -> dict:
    key = jax.random.key(seed)
    ks = iter(jax.random.split(key, 48))
    f32 = jnp.float32
    L, W = DEPTH, BRANCH_W
    G, P, Hg = S5_GROUPS, S5_STATE, S5_GROUP
    NB, BD = LRU_BLOCKS, LRU_BLOCK_DIM

    def nrm(shape, scale):
        return scale * jax.random.normal(next(ks), shape, f32)

    def uni(shape, lo, hi):
        return jax.random.uniform(next(ks), shape, f32, lo, hi)

    x = nrm((BATCH, SEQ, D_MODEL), 1.0)
    p = nrm((DEPTH, BATCH, SEQ, D_PLE), 1.0)
    norm_g = 1.0 + nrm((L, D_MODEL), 0.02)
    w_in = nrm((L, D_MODEL, D_IN), D_MODEL ** -0.5)
    rwkv_mu = uni((L, RWKV_SHIFT_W), 0.0, 1.0)
    rwkv_w0 = uni((L, W), -4.0, 1.0)
    rwkv_w2 = nrm((L, DECAY_LORA, W), 0.1 * DECAY_LORA ** -0.5)
    rwkv_a0 = nrm((L, W), 0.1)
    rwkv_a2 = nrm((L, ICLR_LORA, W), 0.1 * ICLR_LORA ** -0.5)
    rwkv_k_k = 0.85 + nrm((L, W), 0.02)
    rwkv_k_a = 1.0 + nrm((L, W), 0.02)
    rwkv_r_k = nrm((L, RWKV_HEADS, RWKV_HEAD_DIM), 0.1)
    rwkv_ln_w = 1.0 + nrm((L, W), 0.02)
    rwkv_ln_b = nrm((L, W), 0.02)
    s5_a_re = -0.5 + nrm((L, G, P), 0.01)
    s5_a_im = math.pi * jnp.arange(P, dtype=f32) + nrm((L, G, P), 0.01)
    s5_log_dt = uni((L, G), math.log(1e-3), math.log(1e-1))
    s5_b_re = nrm((L, G, P, Hg), (2 * Hg) ** -0.5)
    s5_b_im = nrm((L, G, P, Hg), (2 * Hg) ** -0.5)
    s5_c_re = nrm((L, G, Hg, P), P ** -0.5)
    s5_c_im = nrm((L, G, Hg, P), P ** -0.5)
    s5_d = nrm((L, W), 1.0)
    s5_glu_w = nrm((L, W, W), W ** -0.5)
    s5_glu_b = nrm((L, W), 0.02)
    lru_conv_w = nrm((L, CONV_WIDTH, W), CONV_WIDTH ** -0.5)
    lru_conv_b = nrm((L, W), 0.02)
    lru_wa = nrm((L, NB, BD, BD), BD ** -0.5)
    lru_ba = nrm((L, W), 0.02)
    lru_wx = nrm((L, NB, BD, BD), BD ** -0.5)
    lru_bx = nrm((L, W), 0.02)
    a_c = uni((L, W), 0.9, 0.999) ** (1.0 / LRU_C)
    lru_lambda = jnp.log(a_c) - jnp.log1p(-a_c)
    w_out = nrm((L, D_MIX, D_MODEL), D_MIX ** -0.5)
    ple_w = nrm((L, D_PLE, D_MODEL), D_PLE ** -0.5)
    ple_norm_g = 1.0 + nrm((L, D_MODEL), 0.02)
    ple_gate_w = nrm((L, D_MODEL, D_MODEL), D_MODEL ** -0.5)
    final_norm_g = 1.0 + nrm((D_MODEL,), 0.02)
    return {"x": x, "p": p, "norm_g": norm_g, "w_in": w_in,
            "rwkv_mu": rwkv_mu, "rwkv_w0": rwkv_w0, "rwkv_w2": rwkv_w2, "rwkv_a0": rwkv_a0,
            "rwkv_a2": rwkv_a2, "rwkv_k_k": rwkv_k_k, "rwkv_k_a": rwkv_k_a, "rwkv_r_k": rwkv_r_k,
            "rwkv_ln_w": rwkv_ln_w, "rwkv_ln_b": rwkv_ln_b,
            "s5_a_re": s5_a_re, "s5_a_im": s5_a_im, "s5_log_dt": s5_log_dt,
            "s5_b_re": s5_b_re, "s5_b_im": s5_b_im, "s5_c_re": s5_c_re, "s5_c_im": s5_c_im,
            "s5_d": s5_d, "s5_glu_w": s5_glu_w, "s5_glu_b": s5_glu_b,
            "lru_conv_w": lru_conv_w, "lru_conv_b": lru_conv_b, "lru_wa": lru_wa, "lru_ba": lru_ba,
            "lru_wx": lru_wx, "lru_bx": lru_bx, "lru_lambda": lru_lambda,
            "w_out": w_out, "ple_w": ple_w, "ple_norm_g": ple_norm_g, "ple_gate_w": ple_gate_w,
            "final_norm_g": final_norm_g}


def reference(x, p, norm_g, w_in,
              rwkv_mu, rwkv_w0, rwkv_w2, rwkv_a0, rwkv_a2, rwkv_k_k, rwkv_k_a, rwkv_r_k, rwkv_ln_w, rwkv_ln_b,
              s5_a_re, s5_a_im, s5_log_dt, s5_b_re, s5_b_im, s5_c_re, s5_c_im, s5_d, s5_glu_w, s5_glu_b,
              lru_conv_w, lru_conv_b, lru_wa, lru_ba, lru_wx, lru_bx, lru_lambda,
              w_out, ple_w, ple_norm_g, ple_gate_w, final_norm_g):
    h = x.astype(jnp.float32)
    for i in range(DEPTH):
        h = hybrid_layer(h, p[i].astype(jnp.float32), norm_g[i], w_in[i],
                         rwkv_mu[i], rwkv_w0[i], rwkv_w2[i], rwkv_a0[i], rwkv_a2[i], rwkv_k_k[i],
                         rwkv_k_a[i], rwkv_r_k[i], rwkv_ln_w[i], rwkv_ln_b[i],
                         s5_a_re[i], s5_a_im[i], s5_log_dt[i], s5_b_re[i], s5_b_im[i], s5_c_re[i],
                         s5_c_im[i], s5_d[i], s5_glu_w[i], s5_glu_b[i],
                         lru_conv_w[i], lru_conv_b[i], lru_wa[i], lru_ba[i], lru_wx[i], lru_bx[i],
                         lru_lambda[i], w_out[i], ple_w[i], ple_norm_g[i], ple_gate_w[i])
    return rmsnorm(h, final_norm_g).astype(x.dtype)
```

```python
import functools
import math

import jax
import jax.numpy as jnp
from jax import lax
from jax.experimental import pallas as pl
from jax.experimental.pallas import tpu as pltpu

F32 = jnp.float32
BF16 = jnp.bfloat16

D_MODEL = 1024
D_PLE = 256
BRANCH_W = 512
RWKV_HEADS = 8
HEAD_DIM = 64
LORA = 64
RWKV_GN_EPS = 64e-5
S5_GROUPS = 32
S5_GROUP = 16
S5_STATE = 64
S5_N = S5_GROUPS * S5_STATE
S5_BLOCKS = 4
S5_BLK_IN = BRANCH_W // S5_BLOCKS
S5_BLK_ST = S5_N // S5_BLOCKS
LRU_BLOCKS = 8
LRU_BLOCK_DIM = 64
CONV_WIDTH = 4
LRU_C = 8.0
NORM_EPS = 1e-6
RWKV_SHIFT_W = 3 * BRANCH_W + 2 * LORA
C_GRW = RWKV_SHIFT_W
C_US5 = C_GRW + BRANCH_W
C_GS5 = C_US5 + BRANCH_W
C_XLRU = C_GS5 + BRANCH_W
C_GLRU = C_XLRU + BRANCH_W
D_IN = C_GLRU + BRANCH_W

TL = 256
CHUNK = 64
NCH = TL // CHUNK
NSEG = 8
SEG = TL // NSEG
CARRY = 8
IN_COL_BLK = 1408
SCAN_LANES = 512
VMEM_LIMIT = 56 * 1024 * 1024


def _dot(a, b):
    return jnp.dot(a.astype(BF16), b.astype(BF16), preferred_element_type=F32)


def _bdot(a, b, lhs_c, rhs_c):
    dims = (((lhs_c,), (rhs_c,)), ((0,), (0,)))
    return lax.dot_general(a.astype(BF16), b.astype(BF16), dims, preferred_element_type=F32)


def _hi_lo(x):
    hi = x.astype(BF16)
    lo = (x - hi.astype(F32)).astype(BF16)
    return hi, lo


def _dot_hilo(m, x):
    hi, lo = _hi_lo(x)
    return (jnp.dot(m, hi, preferred_element_type=F32)
            + jnp.dot(m, lo, preferred_element_type=F32))


def _sigmoid(x):
    return 1.0 / (1.0 + jnp.exp(-x))


def _silu(x):
    return x * _sigmoid(x)


def _softplus(x):
    return jnp.maximum(x, 0.0) + jnp.log(1.0 + jnp.exp(-jnp.abs(x)))


def _rmsnorm(x, g):
    return x * lax.rsqrt(jnp.mean(x * x, axis=-1, keepdims=True) + NORM_EPS) * g


def _split_heads(x):
    x3 = x.reshape(NCH, CHUNK, BRANCH_W)
    parts = [x3[:, :, h * HEAD_DIM:(h + 1) * HEAD_DIM] for h in range(RWKV_HEADS)]
    return jnp.concatenate(parts, axis=0)


def _merge_heads(y):
    parts = [y[h * NCH:(h + 1) * NCH].reshape(TL, HEAD_DIM) for h in range(RWKV_HEADS)]
    return jnp.concatenate(parts, axis=-1)


def _unit_lower_inverse_minus_eye(low):
    t_idx = lax.broadcasted_iota(jnp.int32, (CHUNK, CHUNK), 0)
    s_idx = lax.broadcasted_iota(jnp.int32, (CHUNK, CHUNK), 1)
    e = None
    size = 1
    while size < CHUNK:
        sibling = ((t_idx // (2 * size)) == (s_idx // (2 * size))) & ((t_idx // size) != (s_idx // size))
        off = jnp.where(sibling[None], low, 0.0)
        if e is None:
            e = -off
        else:
            a = off + _bdot(e, off, 2, 1)
            e = e - a - _bdot(a, e, 2, 1)
        size *= 2
    return e


def _rwkv_group(zin_ref, w, s_ref):
    rows = pl.ds(CARRY, TL)
    prev = pl.ds(CARRY - 1, TL)
    mu = w["rwkv_mu"][...]

    def shifted(lo, hi):
        z = zin_ref[rows, lo:hi]
        zp = zin_ref[prev, lo:hi]
        return z + mu[:, lo:hi] * (zp - z)

    r = shifted(0, BRANCH_W)
    k = shifted(BRANCH_W, 2 * BRANCH_W)
    v = shifted(2 * BRANCH_W, 3 * BRANCH_W)
    lora_in = shifted(3 * BRANCH_W, RWKV_SHIFT_W)
    lane = lax.broadcasted_iota(jnp.int32, lora_in.shape, 1)
    lora_in = jnp.where(lane < LORA, jnp.tanh(lora_in), lora_in)
    lora = _dot(lora_in, w["rwkv_lora"][...])
    wdec = -_softplus(-(w["rwkv_w0"][...] + lora[:, :BRANCH_W])) - 0.5
    logd = -jnp.exp(wdec)
    a = _sigmoid(w["rwkv_a0"][...] + lora[:, BRANCH_W:])

    head_ones = w["head_ones"][...]
    kk = k * w["rwkv_k_k"][...]
    kk_norm = jnp.sqrt(_dot(kk * kk, head_ones))
    kk = kk / jnp.maximum(kk_norm, 1e-12)
    k = k * (1.0 + (a - 1.0) * w["rwkv_k_a"][...])
    b = kk * a

    cum = _dot_hilo(w["chunk_tri"][...], logd)
    g_inc = cum[:TL]
    g_rest = cum[TL:]
    e_inc = jnp.exp(g_inc)
    e_neg = jnp.exp(-g_inc)
    e_rest = jnp.exp(g_rest)
    kkd = _split_heads(kk * jnp.exp(g_inc - logd))
    rd = _split_heads(r * e_inc)
    bi = _split_heads(b * e_neg)
    ki = _split_heads(k * e_neg)
    bd = _split_heads(b * e_rest)
    kd = _split_heads(k * e_rest)
    vh = _split_heads(v)
    gamma_c = _split_heads(e_inc)[:, CHUNK - 1:CHUNK, :]

    t_idx = lax.broadcasted_iota(jnp.int32, (CHUNK, CHUNK), 0)
    s_idx = lax.broadcasted_iota(jnp.int32, (CHUNK, CHUNK), 1)
    strict = (s_idx < t_idx)[None]
    incl = (s_idx <= t_idx)[None]
    eye = (s_idx == t_idx)[None]
    lb = jnp.where(strict, _bdot(kkd, bi, 2, 2), 0.0)
    lk = jnp.where(strict, _bdot(kkd, ki, 2, 2), 0.0)
    mb = jnp.where(incl, _bdot(rd, bi, 2, 2), 0.0)
    mk = jnp.where(incl, _bdot(rd, ki, 2, 2), 0.0)

    e = _unit_lower_inverse_minus_eye(lb)
    lkv = _bdot(lk, vh, 2, 1)
    wmat = kkd + _bdot(e, kkd, 2, 1)
    uloc = -(lkv + _bdot(e, lkv, 2, 1))
    pmat = jnp.where(eye, gamma_c, 0.0) - _bdot(wmat, bd, 1, 1)
    qmat = _bdot(uloc, bd, 1, 1) + _bdot(vh, kd, 1, 1)
    yloc = _bdot(mb, uloc, 2, 1) + _bdot(mk, vh, 2, 1)
    reff = rd - _bdot(mb, wmat, 2, 1)

    def per_chunk(x, c):
        return x.reshape(RWKV_HEADS, NCH, CHUNK, HEAD_DIM)[:, c]

    state = s_ref[...]
    ys = []
    for c in range(NCH):
        ys.append(per_chunk(yloc, c) + _bdot(per_chunk(reff, c), state, 2, 2))
        state = _bdot(state, per_chunk(pmat, c), 2, 1) + per_chunk(qmat, c)
    s_ref[...] = state
    y = jnp.stack(ys, axis=1).reshape(RWKV_HEADS * NCH, CHUNK, HEAD_DIM)
    y = _merge_heads(y)

    inv_n = 1.0 / HEAD_DIM
    mean = _dot(y, head_ones) * inv_n
    yc = y - mean
    var = _dot(yc * yc, head_ones) * inv_n
    yn = yc * lax.rsqrt(var + RWKV_GN_EPS) * w["rwkv_ln_w"][...] + w["rwkv_ln_b"][...]
    bonus = _dot(r * k * w["rwkv_r_k"][...], head_ones) * v
    gate = zin_ref[rows, C_GRW:C_GRW + BRANCH_W]
    return (yn + bonus) * _silu(gate)


def _s5_group(zin_ref, w, xre_ref, xim_ref, st_re_ref, st_im_ref, start_re_ref, start_im_ref):
    rows = pl.ds(CARRY, TL)
    u = zin_ref[rows, C_US5:C_US5 + BRANCH_W]
    u_il = jnp.dot(w["perm"][...], u.astype(BF16), preferred_element_type=F32).astype(BF16)
    for blk in range(S5_BLOCKS):
        bu = jnp.dot(u_il[:, blk * S5_BLK_IN:(blk + 1) * S5_BLK_IN], w["s5_b"][blk],
                     preferred_element_type=F32)
        xre_ref[:, blk * S5_BLK_ST:(blk + 1) * S5_BLK_ST] = bu[:, :S5_BLK_ST]
        xim_ref[:, blk * S5_BLK_ST:(blk + 1) * S5_BLK_ST] = bu[:, S5_BLK_ST:]

    for lb in range(S5_N // SCAN_LANES):
        cols = slice(lb * SCAN_LANES, (lb + 1) * SCAN_LANES)
        lam_re = jnp.broadcast_to(w["s5_pow_re"][0:1, cols], (NSEG, SCAN_LANES))
        lam_im = jnp.broadcast_to(w["s5_pow_im"][0:1, cols], (NSEG, SCAN_LANES))

        def step(t, carry, cols=cols, lam_re=lam_re, lam_im=lam_im):
            sr, si = carry
            rws = pl.ds(pl.multiple_of(t * NSEG, NSEG), NSEG)
            nr = lam_re * sr - lam_im * si + xre_ref[rws, cols]
            ni = lam_re * si + lam_im * sr + xim_ref[rws, cols]
            xre_ref[rws, cols] = nr
            xim_ref[rws, cols] = ni
            return nr, ni

        zero = jnp.zeros((NSEG, SCAN_LANES), F32)
        lax.fori_loop(0, SEG, step, (zero, zero))

    last = pl.ds(TL - NSEG, NSEG)
    end_re = xre_ref[last, :]
    end_im = xim_ref[last, :]
    seg_re = w["s5_pow_re"][SEG - 1:SEG, :]
    seg_im = w["s5_pow_im"][SEG - 1:SEG, :]
    cur_re = st_re_ref[...]
    cur_im = st_im_ref[...]
    for i in range(NSEG):
        start_re_ref[i:i + 1, :] = cur_re
        start_im_ref[i:i + 1, :] = cur_im
        nxt_re = end_re[i:i + 1, :] + seg_re * cur_re - seg_im * cur_im
        nxt_im = end_im[i:i + 1, :] + seg_re * cur_im + seg_im * cur_re
        cur_re, cur_im = nxt_re, nxt_im
    st_re_ref[...] = cur_re
    st_im_ref[...] = cur_im

    for lb in range(S5_N // SCAN_LANES):
        cols = slice(lb * SCAN_LANES, (lb + 1) * SCAN_LANES)
        s_re = start_re_ref[:, cols]
        s_im = start_im_ref[:, cols]

        def fix(t, carry, cols=cols, s_re=s_re, s_im=s_im):
            rws = pl.ds(pl.multiple_of(t * NSEG, NSEG), NSEG)
            p_re = w["s5_pow_re"][pl.ds(t, 1), cols]
            p_im = w["s5_pow_im"][pl.ds(t, 1), cols]
            xre_ref[rws, cols] = xre_ref[rws, cols] + (p_re * s_re - p_im * s_im)
            xim_ref[rws, cols] = xim_ref[rws, cols] + (p_re * s_im + p_im * s_re)
            return carry

        lax.fori_loop(0, SEG, fix, 0)

    ys = []
    for blk in range(S5_BLOCKS):
        cols = slice(blk * S5_BLK_ST, (blk + 1) * S5_BLK_ST)
        ys.append(_dot(xre_ref[:, cols], w["s5_c_re"][blk]) - _dot(xim_ref[:, cols], w["s5_c_im"][blk]))
    y_il = jnp.concatenate(ys, axis=-1)
    y = _dot_hilo(w["perm_t"][...], y_il) + w["s5_d"][...] * u
    zg = jax.nn.gelu(y)
    zg = zg * _sigmoid(_dot(zg, w["s5_glu_w"][...]) + w["s5_glu_b"][...])
    gate = zin_ref[rows, C_GS5:C_GS5 + BRANCH_W]
    return zg * _silu(gate)


def _lru_group(zin_ref, w, h_ref, cp_ref, st_ref, start_ref):
    conv_w = w["lru_conv_w"][...]
    xc = w["lru_conv_b"][...] + zin_ref[pl.ds(CARRY, TL), C_XLRU:C_XLRU + BRANCH_W] * conv_w[CONV_WIDTH - 1:CONV_WIDTH, :]
    for back in range(1, CONV_WIDTH):
        tap = CONV_WIDTH - 1 - back
        xc = xc + zin_ref[pl.ds(CARRY - back, TL), C_XLRU:C_XLRU + BRANCH_W] * conv_w[tap:tap + 1, :]
    xc = _dot_hilo(w["perm"][...], xc)
    gates = _dot(xc, w["lru_gates"][...])
    r = _sigmoid(gates[:, :BRANCH_W] + w["lru_ba"][...])
    i = _sigmoid(gates[:, BRANCH_W:] + w["lru_bx"][...])
    log_a = (-LRU_C * _softplus(-w["lru_lambda"][...])) * r
    a = jnp.exp(log_a)
    mult = jnp.sqrt(1.0 - jnp.exp(2.0 * log_a))
    h_ref[...] = mult * (i * xc)
    cp_ref[...] = a

    def step(t, carry):
        hs, cs = carry
        rws = pl.ds(pl.multiple_of(t * NSEG, NSEG), NSEG)
        a_t = cp_ref[rws, :]
        hs = a_t * hs + h_ref[rws, :]
        cs = a_t * cs
        h_ref[rws, :] = hs
        cp_ref[rws, :] = cs
        return hs, cs

    lax.fori_loop(0, SEG, step, (jnp.zeros((NSEG, BRANCH_W), F32), jnp.ones((NSEG, BRANCH_W), F32)))

    last = pl.ds(TL - NSEG, NSEG)
    end_h = h_ref[last, :]
    end_cp = cp_ref[last, :]
    cur = st_ref[...]
    for s in range(NSEG):
        start_ref[s:s + 1, :] = cur
        cur = end_h[s:s + 1, :] + end_cp[s:s + 1, :] * cur
    st_ref[...] = cur
    start = start_ref[...]
    h3 = h_ref[...].reshape(SEG, NSEG, BRANCH_W) + cp_ref[...].reshape(SEG, NSEG, BRANCH_W) * start[None]
    h = _dot_hilo(w["perm_t"][...], h3.reshape(TL, BRANCH_W))
    gate = zin_ref[pl.ds(CARRY, TL), C_GLRU:C_GLRU + BRANCH_W]
    return h * _silu(gate)


def _layer_body(names, is_last, *refs):
    n_in = len(names)
    w = dict(zip(names, refs[:n_in]))
    out_ref = refs[n_in]
    (zin_ref, s_ref, xre_ref, xim_ref, s5_re_ref, s5_im_ref, s5_start_re_ref, s5_start_im_ref,
     lru_h_ref, lru_cp_ref, lru_st_ref, lru_start_ref) = refs[n_in + 1:]

    @pl.when(pl.program_id(1) == 0)
    def _():
        zin_ref[0:CARRY, :] = jnp.zeros((CARRY, D_IN), F32)
        s_ref[...] = jnp.zeros_like(s_ref)
        s5_re_ref[...] = jnp.zeros_like(s5_re_ref)
        s5_im_ref[...] = jnp.zeros_like(s5_im_ref)
        lru_st_ref[...] = jnp.zeros_like(lru_st_ref)

    h = w["h"][0]
    xn = _rmsnorm(h, w["norm_g"][...]).astype(BF16)
    for c0 in range(0, D_IN, IN_COL_BLK):
        zin_ref[pl.ds(CARRY, TL), c0:c0 + IN_COL_BLK] = jnp.dot(
            xn, w["w_in"][:, c0:c0 + IN_COL_BLK], preferred_element_type=F32)

    y_rw = _rwkv_group(zin_ref, w, s_ref)
    y_s5 = _s5_group(zin_ref, w, xre_ref, xim_ref, s5_re_ref, s5_im_ref, s5_start_re_ref, s5_start_im_ref)
    y_lru = _lru_group(zin_ref, w, lru_h_ref, lru_cp_ref, lru_st_ref, lru_start_ref)
    zin_ref[0:CARRY, :] = zin_ref[TL:TL + CARRY, :]

    h = h + (_dot(y_rw, w["w_out"][0:BRANCH_W, :])
             + _dot(y_s5, w["w_out"][BRANCH_W:2 * BRANCH_W, :])
             + _dot(y_lru, w["w_out"][2 * BRANCH_W:3 * BRANCH_W, :]))
    e = _rmsnorm(_dot(w["p"][0], w["ple_w"][...]), w["ple_norm_g"][...])
    h = h + e * _sigmoid(_dot(h, w["ple_gate_w"][...]))
    if is_last:
        h = _rmsnorm(h, w["final_norm_g"][...])
    out_ref[0] = h


def _block_diag(blocks):
    n, a, b = blocks.shape
    eye = jnp.eye(n, dtype=blocks.dtype)
    return jnp.einsum("nab,nm->namb", blocks, eye).reshape(n * a, n * b)


def _constants():
    nat = jnp.arange(TL)
    interleaved_row = (nat % SEG) * NSEG + nat // SEG
    perm = jnp.zeros((TL, TL), F32).at[interleaved_row, nat].set(1.0)
    t = jnp.arange(TL)[:, None]
    s = jnp.arange(TL)[None, :]
    same = (t // CHUNK) == (s // CHUNK)
    tri = jnp.concatenate([(same & (s <= t)), (same & (s > t))], axis=0).astype(BF16)
    head_ones = _block_diag(jnp.ones((RWKV_HEADS, HEAD_DIM, HEAD_DIM), F32)).astype(BF16)
    return perm.astype(BF16), perm.T.astype(BF16), tri, head_ones


def _s5_params(a_re, a_im, log_dt, b_re, b_im, c_re, c_im):
    lam = lax.complex(a_re.astype(F32), a_im.astype(F32))
    dt = jnp.exp(log_dt.astype(F32))[:, None]
    lam_bar = jnp.exp(lam * dt)
    b_bar = ((lam_bar - 1.0) / lam)[:, :, None] * lax.complex(b_re.astype(F32), b_im.astype(F32))
    steps = jnp.arange(1, SEG + 1, dtype=F32)[:, None, None]
    pows = jnp.exp(lam * dt * steps).reshape(SEG, S5_N)
    gpb = S5_GROUPS // S5_BLOCKS
    b_in = jnp.swapaxes(b_bar, 1, 2).reshape(S5_BLOCKS, gpb, S5_GROUP, S5_STATE)

    def blk_in(x):
        return jnp.stack([_block_diag(x[j]) for j in range(S5_BLOCKS)])

    s5_b = jnp.concatenate([blk_in(jnp.real(b_in)), blk_in(jnp.imag(b_in))], axis=-1).astype(BF16)
    c_out = jnp.swapaxes(lax.complex(c_re.astype(F32), c_im.astype(F32)), 1, 2)
    c_out = c_out.reshape(S5_BLOCKS, gpb, S5_STATE, S5_GROUP)
    s5_c_re = blk_in(jnp.real(c_out)).astype(BF16)
    s5_c_im = blk_in(jnp.imag(c_out)).astype(BF16)
    return s5_b, s5_c_re, s5_c_im, jnp.real(pows), jnp.imag(pows)


def _layer_call(h, p_i, weights, is_last):
    bsz, seq, _ = h.shape
    assert seq % TL == 0
    names = ["h", "p"] + list(weights.keys())
    arrays = [h, p_i] + list(weights.values())
    in_specs = [
        pl.BlockSpec((1, TL, D_MODEL), lambda b, j: (b, j, 0)),
        pl.BlockSpec((1, TL, D_PLE), lambda b, j: (b, j, 0)),
    ]
    for arr in weights.values():
        zeros = (0,) * arr.ndim
        in_specs.append(pl.BlockSpec(arr.shape, lambda b, j, zeros=zeros: zeros,
                                     pipeline_mode=pl.Buffered(1)))
    scratch = [
        pltpu.VMEM((CARRY + TL, D_IN), F32),
        pltpu.VMEM((RWKV_HEADS, HEAD_DIM, HEAD_DIM), F32),
        pltpu.VMEM((TL, S5_N), F32),
        pltpu.VMEM((TL, S5_N), F32),
        pltpu.VMEM((1, S5_N), F32),
        pltpu.VMEM((1, S5_N), F32),
        pltpu.VMEM((NSEG, S5_N), F32),
        pltpu.VMEM((NSEG, S5_N), F32),
        pltpu.VMEM((TL, BRANCH_W), F32),
        pltpu.VMEM((TL, BRANCH_W), F32),
        pltpu.VMEM((1, BRANCH_W), F32),
        pltpu.VMEM((NSEG, BRANCH_W), F32),
    ]
    return pl.pallas_call(
        functools.partial(_layer_body, tuple(names), is_last),
        grid=(bsz, seq // TL),
        in_specs=in_specs,
        out_specs=pl.BlockSpec((1, TL, D_MODEL), lambda b, j: (b, j, 0)),
        out_shape=jax.ShapeDtypeStruct(h.shape, F32),
        scratch_shapes=scratch,
        compiler_params=pltpu.CompilerParams(
            dimension_semantics=("arbitrary", "arbitrary"),
            vmem_limit_bytes=VMEM_LIMIT),
        name="hybrid_layer",
    )(*arrays)


def kernel(x, p, norm_g, w_in, rwkv_mu, rwkv_w0, rwkv_w2, rwkv_a0, rwkv_a2, rwkv_k_k, rwkv_k_a, rwkv_r_k, rwkv_ln_w, rwkv_ln_b, s5_a_re, s5_a_im, s5_log_dt, s5_b_re, s5_b_im, s5_c_re, s5_c_im, s5_d, s5_glu_w, s5_glu_b, lru_conv_w, lru_conv_b, lru_wa, lru_ba, lru_wx, lru_bx, lru_lambda, w_out, ple_w, ple_norm_g, ple_gate_w, final_norm_g):
    depth = w_in.shape[0]
    perm, perm_t, tri, head_ones = _constants()
    row = lambda v: v.reshape(1, -1).astype(F32)
    h = x.astype(F32)
    for i in range(depth):
        s5_b, s5_cr, s5_ci, pow_re, pow_im = _s5_params(
            s5_a_re[i], s5_a_im[i], s5_log_dt[i], s5_b_re[i], s5_b_im[i], s5_c_re[i], s5_c_im[i])
        zero = jnp.zeros((LORA, BRANCH_W), F32)
        lora = jnp.concatenate([jnp.concatenate([rwkv_w2[i], zero], axis=1),
                                jnp.concatenate([zero, rwkv_a2[i]], axis=1)], axis=0)
        weights = {
            "norm_g": row(norm_g[i]),
            "w_in": w_in[i].astype(BF16),
            "rwkv_mu": row(rwkv_mu[i]),
            "rwkv_lora": lora.astype(BF16),
            "rwkv_w0": row(rwkv_w0[i]),
            "rwkv_a0": row(rwkv_a0[i]),
            "rwkv_k_k": row(rwkv_k_k[i]),
            "rwkv_k_a": row(rwkv_k_a[i]),
            "rwkv_r_k": row(rwkv_r_k[i]),
            "rwkv_ln_w": row(rwkv_ln_w[i]),
            "rwkv_ln_b": row(rwkv_ln_b[i]),
            "head_ones": head_ones,
            "chunk_tri": tri,
            "perm": perm,
            "perm_t": perm_t,
            "s5_b": s5_b,
            "s5_c_re": s5_cr,
            "s5_c_im": s5_ci,
            "s5_pow_re": pow_re,
            "s5_pow_im": pow_im,
            "s5_d": row(s5_d[i]),
            "s5_glu_w": s5_glu_w[i].astype(BF16),
            "s5_glu_b": row(s5_glu_b[i]),
            "lru_conv_w": lru_conv_w[i].astype(F32),
            "lru_conv_b": row(lru_conv_b[i]),
            "lru_gates": jnp.concatenate([_block_diag(lru_wa[i]), _block_diag(lru_wx[i])], axis=1).astype(BF16),
            "lru_ba": row(lru_ba[i]),
            "lru_bx": row(lru_bx[i]),
            "lru_lambda": row(lru_lambda[i]),
            "w_out": w_out[i].astype(BF16),
            "ple_w": ple_w[i].astype(BF16),
            "ple_norm_g": row(ple_norm_g[i]),
            "ple_gate_w": ple_gate_w[i].astype(BF16),
            "final_norm_g": row(final_norm_g),
        }
        h = _layer_call(h, p[i].astype(F32), weights, is_last=(i == depth - 1))
    return h.astype(x.dtype)
```

```python
import functools
import math

import jax
import jax.numpy as jnp
from jax import lax
from jax.experimental import pallas as pl
from jax.experimental.pallas import tpu as pltpu

F32 = jnp.float32
BF16 = jnp.bfloat16

D_MODEL = 1024
D_PLE = 256
BRANCH_W = 512
RWKV_HEADS = 8
HEAD_DIM = 64
PAIR_W = 2 * HEAD_DIM
PAIRS = RWKV_HEADS // 2
LORA = 64
RWKV_GN_EPS = 64e-5
S5_GROUPS = 32
S5_GROUP = 16
S5_STATE = 64
S5_N = S5_GROUPS * S5_STATE
S5_BLOCKS = 4
S5_BLK_IN = BRANCH_W // S5_BLOCKS
S5_BLK_ST = S5_N // S5_BLOCKS
LRU_BLOCKS = 8
LRU_BLOCK_DIM = 64
CONV_WIDTH = 4
LRU_C = 8.0
NORM_EPS = 1e-6
RWKV_SHIFT_W = 3 * BRANCH_W + 2 * LORA
C_GRW = RWKV_SHIFT_W
C_US5 = C_GRW + BRANCH_W
C_GS5 = C_US5 + BRANCH_W
C_XLRU = C_GS5 + BRANCH_W
C_GLRU = C_XLRU + BRANCH_W
D_IN = C_GLRU + BRANCH_W

TL = 256
CHUNK = 64
NCH = TL // CHUNK
NSEG = 8
SEG = TL // NSEG
CARRY = 8
IN_COL_BLKS = ((0, 1280), (1280, 2560), (2560, D_IN))
SCAN_LANES = 512
VMEM_LIMIT = 56 * 1024 * 1024


def _dot(a, b):
    return jnp.dot(a.astype(BF16), b.astype(BF16), preferred_element_type=F32)


def _bdot(a, b, lhs_c, rhs_c):
    dims = (((lhs_c,), (rhs_c,)), ((0,), (0,)))
    return lax.dot_general(a.astype(BF16), b.astype(BF16), dims, preferred_element_type=F32)


def _hi_lo(x):
    hi = x.astype(BF16)
    lo = (x - hi.astype(F32)).astype(BF16)
    return hi, lo


def _dot_hilo(m, x):
    hi, lo = _hi_lo(x)
    return (jnp.dot(m, hi, preferred_element_type=F32)
            + jnp.dot(m, lo, preferred_element_type=F32))


def _sigmoid(x):
    return 1.0 / (1.0 + jnp.exp(-x))


def _silu(x):
    return x * _sigmoid(x)


def _softplus(x):
    return jnp.maximum(x, 0.0) + jnp.log(1.0 + jnp.exp(-jnp.abs(x)))


def _rmsnorm(x, g):
    return x * lax.rsqrt(jnp.mean(x * x, axis=-1, keepdims=True) + NORM_EPS) * g


def _split_pairs(x):
    x3 = x.reshape(NCH, CHUNK, BRANCH_W)
    return jnp.concatenate([x3[:, :, j * PAIR_W:(j + 1) * PAIR_W] for j in range(PAIRS)], axis=0)


def _merge_pairs(y):
    return jnp.concatenate([y[j * NCH:(j + 1) * NCH].reshape(TL, PAIR_W) for j in range(PAIRS)], axis=-1)


def _per_chunk(x, c):
    return x.reshape((PAIRS, NCH) + x.shape[1:])[:, c]


def _bd(x):
    x = x.astype(BF16)
    lane = lax.broadcasted_iota(jnp.int32, x.shape, 2)
    zero = jnp.zeros_like(x)
    return jnp.concatenate([jnp.where(lane < HEAD_DIM, x, zero), jnp.where(lane >= HEAD_DIM, x, zero)], axis=1)


def _pair_nn(x, y):
    return _bdot(x, _bd(y), 2, 1)


def _head_sum(x, ones_pair):
    return jnp.concatenate([_dot(x[:, j * PAIR_W:(j + 1) * PAIR_W], ones_pair) for j in range(PAIRS)], axis=-1)


def _unit_lower_inverse_minus_eye(low, t_idx, s_idx):
    e = None
    size = 1
    while size < CHUNK:
        sibling = ((t_idx // (2 * size)) == (s_idx // (2 * size))) & ((t_idx // size) != (s_idx // size))
        off = jnp.where(sibling[None], low, 0.0)
        if e is None:
            e = -off
        else:
            a = off + _pair_nn(e, off)
            e = e - a - _pair_nn(a, e)
        size *= 2
    return e


def _rwkv_group(zin_ref, w, s_ref):
    rows = pl.ds(CARRY, TL)
    prev = pl.ds(CARRY - 1, TL)
    mu = w["rwkv_mu"][...]

    def shifted(lo, hi):
        z = zin_ref[rows, lo:hi]
        zp = zin_ref[prev, lo:hi]
        return z + mu[:, lo:hi] * (zp - z)

    r = shifted(0, BRANCH_W)
    k = shifted(BRANCH_W, 2 * BRANCH_W)
    v = shifted(2 * BRANCH_W, 3 * BRANCH_W)
    lora_in = shifted(3 * BRANCH_W, RWKV_SHIFT_W)
    lane = lax.broadcasted_iota(jnp.int32, lora_in.shape, 1)
    lora_in = jnp.where(lane < LORA, jnp.tanh(lora_in), lora_in)
    lora = _dot(lora_in, w["rwkv_lora"][...])
    wdec = -_softplus(-(w["rwkv_w0"][...] + lora[:, :BRANCH_W])) - 0.5
    logd = -jnp.exp(wdec)
    a = _sigmoid(w["rwkv_a0"][...] + lora[:, BRANCH_W:])

    ones_pair = w["ones_pair"][...]
    kk = k * w["rwkv_k_k"][...]
    kk_norm = jnp.sqrt(_head_sum(kk * kk, ones_pair))
    kk = kk / jnp.maximum(kk_norm, 1e-12)
    k = k * (1.0 + (a - 1.0) * w["rwkv_k_a"][...])
    b = kk * a

    g_inc = _dot_hilo(w["chunk_tri"][...], logd)
    g_inc3 = g_inc.reshape(NCH, CHUNK, BRANCH_W)
    g_rest = (g_inc3[:, CHUNK - 1:CHUNK, :] - g_inc3).reshape(TL, BRANCH_W)
    e_inc = jnp.exp(g_inc)
    e_neg = jnp.exp(-g_inc)
    e_rest = jnp.exp(g_rest)
    kkd = _split_pairs(kk * jnp.exp(g_inc - logd))
    rd = _split_pairs(r * e_inc)
    bi = _split_pairs(b * e_neg)
    ki = _split_pairs(k * e_neg)
    bd = _split_pairs(b * e_rest)
    kd = _split_pairs(k * e_rest)
    vp = _split_pairs(v)
    gamma_c = _split_pairs(e_inc)[:, CHUNK - 1:CHUNK, :]

    t_idx = lax.broadcasted_iota(jnp.int32, (CHUNK, PAIR_W), 0)
    s_idx = lax.broadcasted_iota(jnp.int32, (CHUNK, PAIR_W), 1) % HEAD_DIM
    strict = (s_idx < t_idx)[None]
    incl = (s_idx <= t_idx)[None]
    row = lax.broadcasted_iota(jnp.int32, (PAIR_W, PAIR_W), 0)
    col = lax.broadcasted_iota(jnp.int32, (PAIR_W, PAIR_W), 1)
    same_head = ((row // HEAD_DIM) == (col // HEAD_DIM))[None]
    eye = (row == col)[None]

    lhs = jnp.concatenate([kkd, rd], axis=1)
    a_b = _bdot(lhs, _bd(bi), 2, 2)
    a_k = _bdot(lhs, _bd(ki), 2, 2)
    lb = jnp.where(strict, a_b[:, :CHUNK], 0.0)
    mb = jnp.where(incl, a_b[:, CHUNK:], 0.0)
    lk = jnp.where(strict, a_k[:, :CHUNK], 0.0)
    mk = jnp.where(incl, a_k[:, CHUNK:], 0.0)

    e = _unit_lower_inverse_minus_eye(lb, t_idx, s_idx)
    lkv = _pair_nn(lk, vp)
    ex = _bdot(e, jnp.concatenate([_bd(kkd), _bd(lkv)], axis=2), 2, 1)
    wmat = kkd + ex[:, :, :PAIR_W]
    uloc = -(lkv + ex[:, :, PAIR_W:])
    pmat = jnp.where(eye, gamma_c, 0.0) - jnp.where(same_head, _bdot(wmat, bd, 1, 1), 0.0)
    qmat = jnp.where(same_head, _bdot(jnp.concatenate([uloc, vp], axis=1),
                                      jnp.concatenate([bd, kd], axis=1), 1, 1), 0.0)
    yloc = _bdot(jnp.concatenate([mb, mk], axis=2), jnp.concatenate([_bd(uloc), _bd(vp)], axis=1), 2, 1)
    reff = rd - _pair_nn(mb, wmat)

    state = s_ref[...]
    ys = []
    for c in range(NCH):
        ys.append(_per_chunk(yloc, c) + _bdot(_per_chunk(reff, c), state, 2, 2))
        state = _bdot(state, _per_chunk(pmat, c), 2, 1) + _per_chunk(qmat, c)
    s_ref[...] = state
    y = _merge_pairs(jnp.stack(ys, axis=1).reshape(PAIRS * NCH, CHUNK, PAIR_W))

    inv_n = 1.0 / HEAD_DIM
    mean = _head_sum(y, ones_pair) * inv_n
    yc = y - mean
    var = _head_sum(yc * yc, ones_pair) * inv_n
    yn = yc * lax.rsqrt(var + RWKV_GN_EPS) * w["rwkv_ln_w"][...] + w["rwkv_ln_b"][...]
    bonus = _head_sum(r * k * w["rwkv_r_k"][...], ones_pair) * v
    gate = zin_ref[rows, C_GRW:C_GRW + BRANCH_W]
    return (yn + bonus) * _silu(gate)


def _s5_group(zin_ref, w, xre_ref, xim_ref, st_re_ref, st_im_ref, start_re_ref, start_im_ref):
    rows = pl.ds(CARRY, TL)
    u = zin_ref[rows, C_US5:C_US5 + BRANCH_W]
    u_il = jnp.dot(w["perm"][...], u.astype(BF16), preferred_element_type=F32).astype(BF16)
    for blk in range(S5_BLOCKS):
        bu = jnp.dot(u_il[:, blk * S5_BLK_IN:(blk + 1) * S5_BLK_IN], w["s5_b"][blk],
                     preferred_element_type=F32)
        xre_ref[:, blk * S5_BLK_ST:(blk + 1) * S5_BLK_ST] = bu[:, :S5_BLK_ST]
        xim_ref[:, blk * S5_BLK_ST:(blk + 1) * S5_BLK_ST] = bu[:, S5_BLK_ST:]

    for lb in range(S5_N // SCAN_LANES):
        cols = slice(lb * SCAN_LANES, (lb + 1) * SCAN_LANES)
        lam_re = jnp.broadcast_to(w["s5_pow_re"][0:1, cols], (NSEG, SCAN_LANES))
        lam_im = jnp.broadcast_to(w["s5_pow_im"][0:1, cols], (NSEG, SCAN_LANES))

        def step(t, carry, cols=cols, lam_re=lam_re, lam_im=lam_im):
            sr, si = carry
            rws = pl.ds(pl.multiple_of(t * NSEG, NSEG), NSEG)
            nr = lam_re * sr - lam_im * si + xre_ref[rws, cols]
            ni = lam_re * si + lam_im * sr + xim_ref[rws, cols]
            xre_ref[rws, cols] = nr
            xim_ref[rws, cols] = ni
            return nr, ni

        zero = jnp.zeros((NSEG, SCAN_LANES), F32)
        lax.fori_loop(0, SEG, step, (zero, zero))

    last = pl.ds(TL - NSEG, NSEG)
    end_re = xre_ref[last, :]
    end_im = xim_ref[last, :]
    seg_re = w["s5_pow_re"][SEG - 1:SEG, :]
    seg_im = w["s5_pow_im"][SEG - 1:SEG, :]
    cur_re = st_re_ref[...]
    cur_im = st_im_ref[...]
    for i in range(NSEG):
        start_re_ref[i:i + 1, :] = cur_re
        start_im_ref[i:i + 1, :] = cur_im
        nxt_re = end_re[i:i + 1, :] + seg_re * cur_re - seg_im * cur_im
        nxt_im = end_im[i:i + 1, :] + seg_re * cur_im + seg_im * cur_re
        cur_re, cur_im = nxt_re, nxt_im
    st_re_ref[...] = cur_re
    st_im_ref[...] = cur_im

    for lb in range(S5_N // SCAN_LANES):
        cols = slice(lb * SCAN_LANES, (lb + 1) * SCAN_LANES)
        s_re = start_re_ref[:, cols]
        s_im = start_im_ref[:, cols]

        def fix(t, carry, cols=cols, s_re=s_re, s_im=s_im):
            rws = pl.ds(pl.multiple_of(t * NSEG, NSEG), NSEG)
            p_re = w["s5_pow_re"][pl.ds(t, 1), cols]
            p_im = w["s5_pow_im"][pl.ds(t, 1), cols]
            xre_ref[rws, cols] = xre_ref[rws, cols] + (p_re * s_re - p_im * s_im)
            xim_ref[rws, cols] = xim_ref[rws, cols] + (p_re * s_im + p_im * s_re)
            return carry

        lax.fori_loop(0, SEG, fix, 0)

    ys = []
    for blk in range(S5_BLOCKS):
        cols = slice(blk * S5_BLK_ST, (blk + 1) * S5_BLK_ST)
        ys.append(_dot(xre_ref[:, cols], w["s5_c_re"][blk]) - _dot(xim_ref[:, cols], w["s5_c_im"][blk]))
    y_il = jnp.concatenate(ys, axis=-1)
    y = _dot_hilo(w["perm_t"][...], y_il) + w["s5_d"][...] * u
    zg = jax.nn.gelu(y)
    zg = zg * _sigmoid(_dot(zg, w["s5_glu_w"][...]) + w["s5_glu_b"][...])
    gate = zin_ref[rows, C_GS5:C_GS5 + BRANCH_W]
    return zg * _silu(gate)


def _lru_group(zin_ref, w, h_ref, cp_ref, st_ref, start_ref):
    conv_w = w["lru_conv_w"][...]
    xc = w["lru_conv_b"][...] + zin_ref[pl.ds(CARRY, TL), C_XLRU:C_XLRU + BRANCH_W] * conv_w[CONV_WIDTH - 1:CONV_WIDTH, :]
    for back in range(1, CONV_WIDTH):
        tap = CONV_WIDTH - 1 - back
        xc = xc + zin_ref[pl.ds(CARRY - back, TL), C_XLRU:C_XLRU + BRANCH_W] * conv_w[tap:tap + 1, :]
    xc = _dot_hilo(w["perm"][...], xc)
    gates = [_dot(xc[:, j * PAIR_W:(j + 1) * PAIR_W], w["lru_gates"][j]) for j in range(PAIRS)]
    r = _sigmoid(jnp.concatenate([g[:, :PAIR_W] for g in gates], axis=-1) + w["lru_ba"][...])
    i = _sigmoid(jnp.concatenate([g[:, PAIR_W:] for g in gates], axis=-1) + w["lru_bx"][...])
    log_a = (-LRU_C * _softplus(-w["lru_lambda"][...])) * r
    a = jnp.exp(log_a)
    mult = jnp.sqrt(1.0 - jnp.exp(2.0 * log_a))
    h_ref[...] = mult * (i * xc)
    cp_ref[...] = a

    def step(t, carry):
        hs, cs = carry
        rws = pl.ds(pl.multiple_of(t * NSEG, NSEG), NSEG)
        a_t = cp_ref[rws, :]
        hs = a_t * hs + h_ref[rws, :]
        cs = a_t * cs
        h_ref[rws, :] = hs
        cp_ref[rws, :] = cs
        return hs, cs

    lax.fori_loop(0, SEG, step, (jnp.zeros((NSEG, BRANCH_W), F32), jnp.ones((NSEG, BRANCH_W), F32)))

    last = pl.ds(TL - NSEG, NSEG)
    end_h = h_ref[last, :]
    end_cp = cp_ref[last, :]
    cur = st_ref[...]
    for s in range(NSEG):
        start_ref[s:s + 1, :] = cur
        cur = end_h[s:s + 1, :] + end_cp[s:s + 1, :] * cur
    st_ref[...] = cur
    start = start_ref[...]
    h3 = h_ref[...].reshape(SEG, NSEG, BRANCH_W) + cp_ref[...].reshape(SEG, NSEG, BRANCH_W) * start[None]
    h = _dot_hilo(w["perm_t"][...], h3.reshape(TL, BRANCH_W))
    gate = zin_ref[pl.ds(CARRY, TL), C_GLRU:C_GLRU + BRANCH_W]
    return h * _silu(gate)


def _layer_body(names, is_last, *refs):
    n_in = len(names)
    w = dict(zip(names, refs[:n_in]))
    out_ref = refs[n_in]
    (zin_ref, s_ref, xre_ref, xim_ref, s5_re_ref, s5_im_ref, s5_start_re_ref, s5_start_im_ref,
     lru_h_ref, lru_cp_ref, lru_st_ref, lru_start_ref) = refs[n_in + 1:]

    @pl.when(pl.program_id(1) == 0)
    def _():
        zin_ref[0:CARRY, :] = jnp.zeros((CARRY, D_IN), F32)
        s_ref[...] = jnp.zeros_like(s_ref)
        s5_re_ref[...] = jnp.zeros_like(s5_re_ref)
        s5_im_ref[...] = jnp.zeros_like(s5_im_ref)
        lru_st_ref[...] = jnp.zeros_like(lru_st_ref)

    h = w["h"][0]
    xn = _rmsnorm(h, w["norm_g"][...]).astype(BF16)
    for c0, c1 in IN_COL_BLKS:
        zin_ref[pl.ds(CARRY, TL), c0:c1] = jnp.dot(xn, w["w_in"][:, c0:c1], preferred_element_type=F32)

    y_rw = _rwkv_group(zin_ref, w, s_ref)
    y_s5 = _s5_group(zin_ref, w, xre_ref, xim_ref, s5_re_ref, s5_im_ref, s5_start_re_ref, s5_start_im_ref)
    y_lru = _lru_group(zin_ref, w, lru_h_ref, lru_cp_ref, lru_st_ref, lru_start_ref)
    zin_ref[0:CARRY, :] = zin_ref[TL:TL + CARRY, :]

    h = h + (_dot(y_rw, w["w_out"][0:BRANCH_W, :])
             + _dot(y_s5, w["w_out"][BRANCH_W:2 * BRANCH_W, :])
             + _dot(y_lru, w["w_out"][2 * BRANCH_W:3 * BRANCH_W, :]))
    e = _rmsnorm(_dot(w["p"][0], w["ple_w"][...]), w["ple_norm_g"][...])
    h = h + e * _sigmoid(_dot(h, w["ple_gate_w"][...]))
    if is_last:
        h = _rmsnorm(h, w["final_norm_g"][...])
    out_ref[0] = h


def _block_diag(blocks):
    n, a, b = blocks.shape
    eye = jnp.eye(n, dtype=blocks.dtype)
    return jnp.einsum("nab,nm->namb", blocks, eye).reshape(n * a, n * b)


def _constants():
    nat = jnp.arange(TL)
    interleaved_row = (nat % SEG) * NSEG + nat // SEG
    perm = jnp.zeros((TL, TL), F32).at[interleaved_row, nat].set(1.0)
    t = jnp.arange(TL)[:, None]
    s = jnp.arange(TL)[None, :]
    same = (t // CHUNK) == (s // CHUNK)
    tri = (same & (s <= t)).astype(BF16)
    ones_pair = _block_diag(jnp.ones((2, HEAD_DIM, HEAD_DIM), F32)).astype(BF16)
    return perm.astype(BF16), perm.T.astype(BF16), tri, ones_pair


def _s5_params(a_re, a_im, log_dt, b_re, b_im, c_re, c_im):
    a_re, a_im = a_re.astype(F32), a_im.astype(F32)
    dt = jnp.exp(log_dt.astype(F32))[:, None]
    mag = jnp.exp(a_re * dt)
    lr, li = mag * jnp.cos(a_im * dt), mag * jnp.sin(a_im * dt)
    den = a_re * a_re + a_im * a_im
    f_re = ((lr - 1.0) * a_re + li * a_im) / den
    f_im = (li * a_re - (lr - 1.0) * a_im) / den
    b_re, b_im = b_re.astype(F32), b_im.astype(F32)
    bb_re = f_re[:, :, None] * b_re - f_im[:, :, None] * b_im
    bb_im = f_re[:, :, None] * b_im + f_im[:, :, None] * b_re
    steps = jnp.arange(1, SEG + 1, dtype=F32)[:, None, None]
    pmag = jnp.exp(a_re * dt * steps)
    pow_re = (pmag * jnp.cos(a_im * dt * steps)).reshape(SEG, S5_N)
    pow_im = (pmag * jnp.sin(a_im * dt * steps)).reshape(SEG, S5_N)
    gpb = S5_GROUPS // S5_BLOCKS

    def blocks(x, rows, cols):
        x = x.reshape(S5_BLOCKS, gpb, rows, cols)
        return jnp.stack([_block_diag(x[j]) for j in range(S5_BLOCKS)])

    s5_b = jnp.concatenate([blocks(jnp.swapaxes(bb_re, 1, 2), S5_GROUP, S5_STATE),
                            blocks(jnp.swapaxes(bb_im, 1, 2), S5_GROUP, S5_STATE)], axis=-1).astype(BF16)
    s5_c_re = blocks(jnp.swapaxes(c_re.astype(F32), 1, 2), S5_STATE, S5_GROUP).astype(BF16)
    s5_c_im = blocks(jnp.swapaxes(c_im.astype(F32), 1, 2), S5_STATE, S5_GROUP).astype(BF16)
    return s5_b, s5_c_re, s5_c_im, pow_re, pow_im


def _layer_call(h, p_i, weights, is_last):
    bsz, seq, _ = h.shape
    assert seq % TL == 0
    names = ["h", "p"] + list(weights.keys())
    arrays = [h, p_i] + list(weights.values())
    in_specs = [
        pl.BlockSpec((1, TL, D_MODEL), lambda b, j: (b, j, 0)),
        pl.BlockSpec((1, TL, D_PLE), lambda b, j: (b, j, 0)),
    ]
    for arr in weights.values():
        zeros = (0,) * arr.ndim
        in_specs.append(pl.BlockSpec(arr.shape, lambda b, j, zeros=zeros: zeros,
                                     pipeline_mode=pl.Buffered(1)))
    scratch = [
        pltpu.VMEM((CARRY + TL, D_IN), F32),
        pltpu.VMEM((PAIRS, PAIR_W, PAIR_W), F32),
        pltpu.VMEM((TL, S5_N), F32),
        pltpu.VMEM((TL, S5_N), F32),
        pltpu.VMEM((1, S5_N), F32),
        pltpu.VMEM((1, S5_N), F32),
        pltpu.VMEM((NSEG, S5_N), F32),
        pltpu.VMEM((NSEG, S5_N), F32),
        pltpu.VMEM((TL, BRANCH_W), F32),
        pltpu.VMEM((TL, BRANCH_W), F32),
        pltpu.VMEM((1, BRANCH_W), F32),
        pltpu.VMEM((NSEG, BRANCH_W), F32),
    ]
    return pl.pallas_call(
        functools.partial(_layer_body, tuple(names), is_last),
        grid=(bsz, seq // TL),
        in_specs=in_specs,
        out_specs=pl.BlockSpec((1, TL, D_MODEL), lambda b, j: (b, j, 0)),
        out_shape=jax.ShapeDtypeStruct(h.shape, F32),
        scratch_shapes=scratch,
        compiler_params=pltpu.CompilerParams(
            dimension_semantics=("arbitrary", "arbitrary"),
            vmem_limit_bytes=VMEM_LIMIT),
        name="hybrid_layer",
    )(*arrays)


def kernel(x, p, norm_g, w_in, rwkv_mu, rwkv_w0, rwkv_w2, rwkv_a0, rwkv_a2, rwkv_k_k, rwkv_k_a, rwkv_r_k, rwkv_ln_w, rwkv_ln_b, s5_a_re, s5_a_im, s5_log_dt, s5_b_re, s5_b_im, s5_c_re, s5_c_im, s5_d, s5_glu_w, s5_glu_b, lru_conv_w, lru_conv_b, lru_wa, lru_ba, lru_wx, lru_bx, lru_lambda, w_out, ple_w, ple_norm_g, ple_gate_w, final_norm_g):
    depth = w_in.shape[0]
    perm, perm_t, tri, ones_pair = _constants()
    row = lambda v: v.reshape(1, -1).astype(F32)
    h = x.astype(F32)
    for i in range(depth):
        s5_b, s5_cr, s5_ci, pow_re, pow_im = _s5_params(
            s5_a_re[i], s5_a_im[i], s5_log_dt[i], s5_b_re[i], s5_b_im[i], s5_c_re[i], s5_c_im[i])
        zero = jnp.zeros((LORA, BRANCH_W), F32)
        lora = jnp.concatenate([jnp.concatenate([rwkv_w2[i], zero], axis=1),
                                jnp.concatenate([zero, rwkv_a2[i]], axis=1)], axis=0)
        weights = {
            "norm_g": row(norm_g[i]),
            "w_in": w_in[i].astype(BF16),
            "rwkv_mu": row(rwkv_mu[i]),
            "rwkv_lora": lora.astype(BF16),
            "rwkv_w0": row(rwkv_w0[i]),
            "rwkv_a0": row(rwkv_a0[i]),
            "rwkv_k_k": row(rwkv_k_k[i]),
            "rwkv_k_a": row(rwkv_k_a[i]),
            "rwkv_r_k": row(rwkv_r_k[i]),
            "rwkv_ln_w": row(rwkv_ln_w[i]),
            "rwkv_ln_b": row(rwkv_ln_b[i]),
            "ones_pair": ones_pair,
            "chunk_tri": tri,
            "perm": perm,
            "perm_t": perm_t,
            "s5_b": s5_b,
            "s5_c_re": s5_cr,
            "s5_c_im": s5_ci,
            "s5_pow_re": pow_re,
            "s5_pow_im": pow_im,
            "s5_d": row(s5_d[i]),
            "s5_glu_w": s5_glu_w[i].astype(BF16),
            "s5_glu_b": row(s5_glu_b[i]),
            "lru_conv_w": lru_conv_w[i].astype(F32),
            "lru_conv_b": row(lru_conv_b[i]),
            "lru_gates": jnp.stack([
                jnp.concatenate([_block_diag(lru_wa[i, 2 * j:2 * j + 2]), _block_diag(lru_wx[i, 2 * j:2 * j + 2])], axis=1)
                for j in range(PAIRS)]).astype(BF16),
            "lru_ba": row(lru_ba[i]),
            "lru_bx": row(lru_bx[i]),
            "lru_lambda": row(lru_lambda[i]),
            "w_out": w_out[i].astype(BF16),
            "ple_w": ple_w[i].astype(BF16),
            "ple_norm_g": row(ple_norm_g[i]),
            "ple_gate_w": ple_gate_w[i].astype(BF16),
            "final_norm_g": row(final_norm_g),
        }
        h = _layer_call(h, p[i].astype(F32), weights, is_last=(i == depth - 1))
    return h.astype(x.dtype)
```

```python
import functools
import math

import jax
import jax.numpy as jnp
from jax import lax
from jax.experimental import pallas as pl
from jax.experimental.pallas import tpu as pltpu

F32 = jnp.float32
BF16 = jnp.bfloat16

D_MODEL = 1024
D_PLE = 256
BRANCH_W = 512
RWKV_HEADS = 8
HEAD_DIM = 64
PAIR_W = 2 * HEAD_DIM
PAIRS = RWKV_HEADS // 2
LORA = 64
RWKV_GN_EPS = 64e-5
S5_GROUPS = 32
S5_GROUP = 16
S5_STATE = 64
S5_N = S5_GROUPS * S5_STATE
S5_BLOCKS = 4
S5_BLK_IN = BRANCH_W // S5_BLOCKS
S5_BLK_ST = S5_N // S5_BLOCKS
LRU_BLOCKS = 8
LRU_BLOCK_DIM = 64
CONV_WIDTH = 4
LRU_C = 8.0
NORM_EPS = 1e-6
RWKV_SHIFT_W = 3 * BRANCH_W + 2 * LORA
C_GRW = RWKV_SHIFT_W
C_US5 = C_GRW + BRANCH_W
C_GS5 = C_US5 + BRANCH_W
C_XLRU = C_GS5 + BRANCH_W
C_GLRU = C_XLRU + BRANCH_W
D_IN = C_GLRU + BRANCH_W

TL = 256
CHUNK = 64
NCH = TL // CHUNK
NSEG = 8
SEG = TL // NSEG
CARRY = 8
IN_COL_BLKS = ((0, 1280), (1280, 2560), (2560, D_IN))
SCAN_LANES = 512
VMEM_LIMIT = 56 * 1024 * 1024


def _dot(a, b):
    return jnp.dot(a.astype(BF16), b.astype(BF16), preferred_element_type=F32)


def _bdot(a, b, lhs_c, rhs_c):
    dims = (((lhs_c,), (rhs_c,)), ((0,), (0,)))
    return lax.dot_general(a.astype(BF16), b.astype(BF16), dims, preferred_element_type=F32)


def _hi_lo(x):
    hi = x.astype(BF16)
    lo = (x - hi.astype(F32)).astype(BF16)
    return hi, lo


def _dot_hilo(m, x):
    hi, lo = _hi_lo(x)
    return (jnp.dot(m, hi, preferred_element_type=F32)
            + jnp.dot(m, lo, preferred_element_type=F32))


def _sigmoid(x):
    return 1.0 / (1.0 + jnp.exp(-x))


def _silu(x):
    return x * _sigmoid(x)


def _softplus(x):
    return jnp.maximum(x, 0.0) + jnp.log(1.0 + jnp.exp(-jnp.abs(x)))


def _rmsnorm(x, g):
    return x * lax.rsqrt(jnp.mean(x * x, axis=-1, keepdims=True) + NORM_EPS) * g


def _split_pairs(x):
    x3 = x.reshape(NCH, CHUNK, BRANCH_W)
    return jnp.concatenate([x3[:, :, j * PAIR_W:(j + 1) * PAIR_W] for j in range(PAIRS)], axis=0)


def _merge_pairs(y):
    return jnp.concatenate([y[j * NCH:(j + 1) * NCH].reshape(TL, PAIR_W) for j in range(PAIRS)], axis=-1)


def _per_chunk(x, c):
    return x.reshape((PAIRS, NCH) + x.shape[1:])[:, c]


def _bd(x):
    x = x.astype(BF16)
    lane = lax.broadcasted_iota(jnp.int32, x.shape, 2)
    zero = jnp.zeros_like(x)
    return jnp.concatenate([jnp.where(lane < HEAD_DIM, x, zero), jnp.where(lane >= HEAD_DIM, x, zero)], axis=1)


def _pair_nn(x, y):
    return _bdot(x, _bd(y), 2, 1)


def _head_sum(x, ones_pair):
    return jnp.concatenate([_dot(x[:, j * PAIR_W:(j + 1) * PAIR_W], ones_pair) for j in range(PAIRS)], axis=-1)


def _unit_lower_inverse_minus_eye(low, t_idx, s_idx):
    e = None
    size = 1
    while size < CHUNK:
        sibling = ((t_idx // (2 * size)) == (s_idx // (2 * size))) & ((t_idx // size) != (s_idx // size))
        off = jnp.where(sibling[None], low, 0.0)
        if e is None:
            e = -off
        else:
            a = off + _pair_nn(e, off)
            e = e - a - _pair_nn(a, e)
        size *= 2
    return e


def _rwkv_group(zin_ref, w, s_ref):
    rows = pl.ds(CARRY, TL)
    prev = pl.ds(CARRY - 1, TL)
    mu = w["rwkv_mu"][...]

    def shifted(lo, hi):
        z = zin_ref[rows, lo:hi]
        zp = zin_ref[prev, lo:hi]
        return z + mu[:, lo:hi] * (zp - z)

    r = shifted(0, BRANCH_W)
    k = shifted(BRANCH_W, 2 * BRANCH_W)
    v = shifted(2 * BRANCH_W, 3 * BRANCH_W)
    lora_in = shifted(3 * BRANCH_W, RWKV_SHIFT_W)
    lane = lax.broadcasted_iota(jnp.int32, lora_in.shape, 1)
    lora_in = jnp.where(lane < LORA, jnp.tanh(lora_in), lora_in)
    lora = _dot(lora_in, w["rwkv_lora"][...])
    wdec = -_softplus(-(w["rwkv_w0"][...] + lora[:, :BRANCH_W])) - 0.5
    logd = -jnp.exp(wdec)
    a = _sigmoid(w["rwkv_a0"][...] + lora[:, BRANCH_W:])

    ones_pair = w["ones_pair"][...]
    kk = k * w["rwkv_k_k"][...]
    kk = kk * lax.rsqrt(jnp.maximum(_head_sum(kk * kk, ones_pair), 1e-24))
    k = k * (1.0 + (a - 1.0) * w["rwkv_k_a"][...])
    b = kk * a

    g_inc = _dot_hilo(w["chunk_tri"][...], logd)
    g_inc3 = g_inc.reshape(NCH, CHUNK, BRANCH_W)
    g_rest = (g_inc3[:, CHUNK - 1:CHUNK, :] - g_inc3).reshape(TL, BRANCH_W)
    e_inc = jnp.exp(g_inc)
    e_neg = jnp.exp(-g_inc)
    e_rest = jnp.exp(g_rest)
    kkd = _split_pairs(kk * jnp.exp(g_inc - logd))
    rd = _split_pairs(r * e_inc)
    bi = _split_pairs(b * e_neg)
    ki = _split_pairs(k * e_neg)
    bd = _split_pairs(b * e_rest)
    kd = _split_pairs(k * e_rest)
    vp = _split_pairs(v)
    gamma_c = _split_pairs(e_inc)[:, CHUNK - 1:CHUNK, :]

    t_idx = lax.broadcasted_iota(jnp.int32, (CHUNK, PAIR_W), 0)
    s_idx = lax.broadcasted_iota(jnp.int32, (CHUNK, PAIR_W), 1) % HEAD_DIM
    strict = (s_idx < t_idx)[None]
    incl = (s_idx <= t_idx)[None]
    row = lax.broadcasted_iota(jnp.int32, (PAIR_W, PAIR_W), 0)
    col = lax.broadcasted_iota(jnp.int32, (PAIR_W, PAIR_W), 1)
    same_head = ((row // HEAD_DIM) == (col // HEAD_DIM))[None]
    eye = (row == col)[None]

    lhs = jnp.concatenate([kkd, rd], axis=1)
    a_b = _bdot(lhs, _bd(bi), 2, 2)
    a_k = _bdot(lhs, _bd(ki), 2, 2)
    lb = jnp.where(strict, a_b[:, :CHUNK], 0.0)
    mb = jnp.where(incl, a_b[:, CHUNK:], 0.0)
    lk = jnp.where(strict, a_k[:, :CHUNK], 0.0)
    mk = jnp.where(incl, a_k[:, CHUNK:], 0.0)

    e = _unit_lower_inverse_minus_eye(lb, t_idx, s_idx)
    lkv = _pair_nn(lk, vp)
    ex = _bdot(e, jnp.concatenate([_bd(kkd), _bd(lkv)], axis=2), 2, 1)
    wmat = kkd + ex[:, :, :PAIR_W]
    uloc = -(lkv + ex[:, :, PAIR_W:])
    pmat = jnp.where(eye, gamma_c, 0.0) - jnp.where(same_head, _bdot(wmat, bd, 1, 1), 0.0)
    qmat = jnp.where(same_head, _bdot(jnp.concatenate([uloc, vp], axis=1),
                                      jnp.concatenate([bd, kd], axis=1), 1, 1), 0.0)
    yloc = _bdot(jnp.concatenate([mb, mk], axis=2), jnp.concatenate([_bd(uloc), _bd(vp)], axis=1), 2, 1)
    reff = rd - _pair_nn(mb, wmat)

    state = s_ref[...]
    ys = []
    for c in range(NCH):
        ys.append(_per_chunk(yloc, c) + _bdot(_per_chunk(reff, c), state, 2, 2))
        state = _bdot(state, _per_chunk(pmat, c), 2, 1) + _per_chunk(qmat, c)
    s_ref[...] = state
    y = _merge_pairs(jnp.stack(ys, axis=1).reshape(PAIRS * NCH, CHUNK, PAIR_W))

    inv_n = 1.0 / HEAD_DIM
    mean = _head_sum(y, ones_pair) * inv_n
    yc = y - mean
    var = _head_sum(yc * yc, ones_pair) * inv_n
    yn = yc * lax.rsqrt(var + RWKV_GN_EPS) * w["rwkv_ln_w"][...] + w["rwkv_ln_b"][...]
    bonus = _head_sum(r * k * w["rwkv_r_k"][...], ones_pair) * v
    gate = zin_ref[rows, C_GRW:C_GRW + BRANCH_W]
    return (yn + bonus) * _silu(gate)


def _s5_group(zin_ref, w, xre_ref, xim_ref, st_re_ref, st_im_ref, start_re_ref, start_im_ref):
    rows = pl.ds(CARRY, TL)
    u = zin_ref[rows, C_US5:C_US5 + BRANCH_W]
    u_il = jnp.dot(w["perm"][...], u.astype(BF16), preferred_element_type=F32).astype(BF16)
    for blk in range(S5_BLOCKS):
        bu = jnp.dot(u_il[:, blk * S5_BLK_IN:(blk + 1) * S5_BLK_IN], w["s5_b"][blk],
                     preferred_element_type=F32)
        xre_ref[:, blk * S5_BLK_ST:(blk + 1) * S5_BLK_ST] = bu[:, :S5_BLK_ST]
        xim_ref[:, blk * S5_BLK_ST:(blk + 1) * S5_BLK_ST] = bu[:, S5_BLK_ST:]

    for lb in range(S5_N // SCAN_LANES):
        cols = slice(lb * SCAN_LANES, (lb + 1) * SCAN_LANES)
        lam_re = jnp.broadcast_to(w["s5_pow_re"][0:1, cols], (NSEG, SCAN_LANES))
        lam_im = jnp.broadcast_to(w["s5_pow_im"][0:1, cols], (NSEG, SCAN_LANES))

        def step(t, carry, cols=cols, lam_re=lam_re, lam_im=lam_im):
            sr, si = carry
            rws = pl.ds(pl.multiple_of(t * NSEG, NSEG), NSEG)
            nr = lam_re * sr - lam_im * si + xre_ref[rws, cols]
            ni = lam_re * si + lam_im * sr + xim_ref[rws, cols]
            xre_ref[rws, cols] = nr
            xim_ref[rws, cols] = ni
            return nr, ni

        zero = jnp.zeros((NSEG, SCAN_LANES), F32)
        lax.fori_loop(0, SEG, step, (zero, zero))

    last = pl.ds(TL - NSEG, NSEG)
    end_re = xre_ref[last, :]
    end_im = xim_ref[last, :]
    seg_re = w["s5_pow_re"][SEG - 1:SEG, :]
    seg_im = w["s5_pow_im"][SEG - 1:SEG, :]
    cur_re = st_re_ref[...]
    cur_im = st_im_ref[...]
    for i in range(NSEG):
        start_re_ref[i:i + 1, :] = cur_re
        start_im_ref[i:i + 1, :] = cur_im
        nxt_re = end_re[i:i + 1, :] + seg_re * cur_re - seg_im * cur_im
        nxt_im = end_im[i:i + 1, :] + seg_re * cur_im + seg_im * cur_re
        cur_re, cur_im = nxt_re, nxt_im
    st_re_ref[...] = cur_re
    st_im_ref[...] = cur_im

    ys = []
    seg3 = (SEG, NSEG, S5_BLK_ST)
    for blk in range(S5_BLOCKS):
        cols = slice(blk * S5_BLK_ST, (blk + 1) * S5_BLK_ST)
        s_re = start_re_ref[:, cols][None]
        s_im = start_im_ref[:, cols][None]
        p_re = w["s5_powb_re"][:, cols].reshape(seg3)
        p_im = w["s5_powb_im"][:, cols].reshape(seg3)
        x_re = xre_ref[:, cols].reshape(seg3) + (p_re * s_re - p_im * s_im)
        x_im = xim_ref[:, cols].reshape(seg3) + (p_re * s_im + p_im * s_re)
        x_ri = jnp.concatenate([x_re.reshape(TL, S5_BLK_ST).astype(BF16),
                                x_im.reshape(TL, S5_BLK_ST).astype(BF16)], axis=-1)
        ys.append(jnp.dot(x_ri, w["s5_c"][blk], preferred_element_type=F32))
    y_il = jnp.concatenate(ys, axis=-1)
    y = _dot(w["perm_t"][...], y_il) + w["s5_d"][...] * u
    zg = jax.nn.gelu(y)
    zg = zg * _sigmoid(_dot(zg, w["s5_glu_w"][...]) + w["s5_glu_b"][...])
    gate = zin_ref[rows, C_GS5:C_GS5 + BRANCH_W]
    return zg * _silu(gate)


def _lru_group(zin_ref, w, h_ref, cp_ref, st_ref, start_ref):
    conv_w = w["lru_conv_w"][...]
    xc = w["lru_conv_b"][...] + zin_ref[pl.ds(CARRY, TL), C_XLRU:C_XLRU + BRANCH_W] * conv_w[CONV_WIDTH - 1:CONV_WIDTH, :]
    for back in range(1, CONV_WIDTH):
        tap = CONV_WIDTH - 1 - back
        xc = xc + zin_ref[pl.ds(CARRY - back, TL), C_XLRU:C_XLRU + BRANCH_W] * conv_w[tap:tap + 1, :]
    xc = _dot_hilo(w["perm"][...], xc)
    gates = [_dot(xc[:, j * PAIR_W:(j + 1) * PAIR_W], w["lru_gates"][j]) for j in range(PAIRS)]
    r = _sigmoid(jnp.concatenate([g[:, :PAIR_W] for g in gates], axis=-1) + w["lru_ba"][...])
    i = _sigmoid(jnp.concatenate([g[:, PAIR_W:] for g in gates], axis=-1) + w["lru_bx"][...])
    log_a = (-LRU_C * _softplus(-w["lru_lambda"][...])) * r
    a = jnp.exp(log_a)
    mult = jnp.sqrt(1.0 - jnp.exp(2.0 * log_a))
    h_ref[...] = mult * (i * xc)
    cp_ref[...] = a

    def step(t, carry):
        hs, cs = carry
        rws = pl.ds(pl.multiple_of(t * NSEG, NSEG), NSEG)
        a_t = cp_ref[rws, :]
        hs = a_t * hs + h_ref[rws, :]
        cs = a_t * cs
        h_ref[rws, :] = hs
        cp_ref[rws, :] = cs
        return hs, cs

    lax.fori_loop(0, SEG, step, (jnp.zeros((NSEG, BRANCH_W), F32), jnp.ones((NSEG, BRANCH_W), F32)))

    last = pl.ds(TL - NSEG, NSEG)
    end_h = h_ref[last, :]
    end_cp = cp_ref[last, :]
    cur = st_ref[...]
    for s in range(NSEG):
        start_ref[s:s + 1, :] = cur
        cur = end_h[s:s + 1, :] + end_cp[s:s + 1, :] * cur
    st_ref[...] = cur
    start = start_ref[...]
    h3 = h_ref[...].reshape(SEG, NSEG, BRANCH_W) + cp_ref[...].reshape(SEG, NSEG, BRANCH_W) * start[None]
    h = _dot(w["perm_t"][...], h3.reshape(TL, BRANCH_W))
    gate = zin_ref[pl.ds(CARRY, TL), C_GLRU:C_GLRU + BRANCH_W]
    return h * _silu(gate)


def _layer_body(names, is_last, *refs):
    n_in = len(names)
    w = dict(zip(names, refs[:n_in]))
    out_ref = refs[n_in]
    (zin_ref, s_ref, xre_ref, xim_ref, s5_re_ref, s5_im_ref, s5_start_re_ref, s5_start_im_ref,
     lru_h_ref, lru_cp_ref, lru_st_ref, lru_start_ref) = refs[n_in + 1:]

    @pl.when(pl.program_id(1) == 0)
    def _():
        zin_ref[0:CARRY, :] = jnp.zeros((CARRY, D_IN), F32)
        s_ref[...] = jnp.zeros_like(s_ref)
        s5_re_ref[...] = jnp.zeros_like(s5_re_ref)
        s5_im_ref[...] = jnp.zeros_like(s5_im_ref)
        lru_st_ref[...] = jnp.zeros_like(lru_st_ref)

    h = w["h"][0]
    xn = _rmsnorm(h, w["norm_g"][...]).astype(BF16)
    for c0, c1 in IN_COL_BLKS:
        zin_ref[pl.ds(CARRY, TL), c0:c1] = jnp.dot(xn, w["w_in"][:, c0:c1], preferred_element_type=F32)

    y_rw = _rwkv_group(zin_ref, w, s_ref)
    y_s5 = _s5_group(zin_ref, w, xre_ref, xim_ref, s5_re_ref, s5_im_ref, s5_start_re_ref, s5_start_im_ref)
    y_lru = _lru_group(zin_ref, w, lru_h_ref, lru_cp_ref, lru_st_ref, lru_start_ref)
    zin_ref[0:CARRY, :] = zin_ref[TL:TL + CARRY, :]

    h = h + (_dot(y_rw, w["w_out"][0:BRANCH_W, :])
             + _dot(y_s5, w["w_out"][BRANCH_W:2 * BRANCH_W, :])
             + _dot(y_lru, w["w_out"][2 * BRANCH_W:3 * BRANCH_W, :]))
    e = _rmsnorm(_dot(w["p"][0], w["ple_w"][...]), w["ple_norm_g"][...])
    h = h + e * _sigmoid(_dot(h, w["ple_gate_w"][...]))
    if is_last:
        h = _rmsnorm(h, w["final_norm_g"][...])
    out_ref[0] = h


def _block_diag(blocks):
    n, a, b = blocks.shape
    eye = jnp.eye(n, dtype=blocks.dtype)
    return jnp.einsum("nab,nm->namb", blocks, eye).reshape(n * a, n * b)


def _constants():
    nat = jnp.arange(TL)
    interleaved_row = (nat % SEG) * NSEG + nat // SEG
    perm = jnp.zeros((TL, TL), F32).at[interleaved_row, nat].set(1.0)
    t = jnp.arange(TL)[:, None]
    s = jnp.arange(TL)[None, :]
    same = (t // CHUNK) == (s // CHUNK)
    tri = (same & (s <= t)).astype(BF16)
    ones_pair = _block_diag(jnp.ones((2, HEAD_DIM, HEAD_DIM), F32)).astype(BF16)
    return perm.astype(BF16), perm.T.astype(BF16), tri, ones_pair


def _s5_params(a_re, a_im, log_dt, b_re, b_im, c_re, c_im):
    a_re, a_im = a_re.astype(F32), a_im.astype(F32)
    dt = jnp.exp(log_dt.astype(F32))[:, None]
    mag = jnp.exp(a_re * dt)
    lr, li = mag * jnp.cos(a_im * dt), mag * jnp.sin(a_im * dt)
    den = a_re * a_re + a_im * a_im
    f_re = ((lr - 1.0) * a_re + li * a_im) / den
    f_im = (li * a_re - (lr - 1.0) * a_im) / den
    b_re, b_im = b_re.astype(F32), b_im.astype(F32)
    bb_re = f_re[:, :, None] * b_re - f_im[:, :, None] * b_im
    bb_im = f_re[:, :, None] * b_im + f_im[:, :, None] * b_re
    steps = jnp.arange(1, SEG + 1, dtype=F32)[:, None, None]
    pmag = jnp.exp(a_re * dt * steps)
    pow_re = (pmag * jnp.cos(a_im * dt * steps)).reshape(SEG, S5_N)
    pow_im = (pmag * jnp.sin(a_im * dt * steps)).reshape(SEG, S5_N)
    gpb = S5_GROUPS // S5_BLOCKS

    def blocks(x, rows, cols):
        x = x.reshape(S5_BLOCKS, gpb, rows, cols)
        return jnp.stack([_block_diag(x[j]) for j in range(S5_BLOCKS)])

    s5_b = jnp.concatenate([blocks(jnp.swapaxes(bb_re, 1, 2), S5_GROUP, S5_STATE),
                            blocks(jnp.swapaxes(bb_im, 1, 2), S5_GROUP, S5_STATE)], axis=-1).astype(BF16)
    s5_c = jnp.concatenate([blocks(jnp.swapaxes(c_re.astype(F32), 1, 2), S5_STATE, S5_GROUP),
                            -blocks(jnp.swapaxes(c_im.astype(F32), 1, 2), S5_STATE, S5_GROUP)], axis=1).astype(BF16)
    return s5_b, s5_c, pow_re, pow_im


def _layer_call(h, p_i, weights, is_last):
    bsz, seq, _ = h.shape
    assert seq % TL == 0
    names = ["h", "p"] + list(weights.keys())
    arrays = [h, p_i] + list(weights.values())
    in_specs = [
        pl.BlockSpec((1, TL, D_MODEL), lambda b, j: (b, j, 0)),
        pl.BlockSpec((1, TL, D_PLE), lambda b, j: (b, j, 0)),
    ]
    for arr in weights.values():
        zeros = (0,) * arr.ndim
        in_specs.append(pl.BlockSpec(arr.shape, lambda b, j, zeros=zeros: zeros,
                                     pipeline_mode=pl.Buffered(1)))
    scratch = [
        pltpu.VMEM((CARRY + TL, D_IN), F32),
        pltpu.VMEM((PAIRS, PAIR_W, PAIR_W), F32),
        pltpu.VMEM((TL, S5_N), F32),
        pltpu.VMEM((TL, S5_N), F32),
        pltpu.VMEM((1, S5_N), F32),
        pltpu.VMEM((1, S5_N), F32),
        pltpu.VMEM((NSEG, S5_N), F32),
        pltpu.VMEM((NSEG, S5_N), F32),
        pltpu.VMEM((TL, BRANCH_W), F32),
        pltpu.VMEM((TL, BRANCH_W), F32),
        pltpu.VMEM((1, BRANCH_W), F32),
        pltpu.VMEM((NSEG, BRANCH_W), F32),
    ]
    return pl.pallas_call(
        functools.partial(_layer_body, tuple(names), is_last),
        grid=(bsz, seq // TL),
        in_specs=in_specs,
        out_specs=pl.BlockSpec((1, TL, D_MODEL), lambda b, j: (b, j, 0)),
        out_shape=jax.ShapeDtypeStruct(h.shape, F32),
        scratch_shapes=scratch,
        compiler_params=pltpu.CompilerParams(
            dimension_semantics=("arbitrary", "arbitrary"),
            vmem_limit_bytes=VMEM_LIMIT),
        name="hybrid_layer",
    )(*arrays)


def kernel(x, p, norm_g, w_in, rwkv_mu, rwkv_w0, rwkv_w2, rwkv_a0, rwkv_a2, rwkv_k_k, rwkv_k_a, rwkv_r_k, rwkv_ln_w, rwkv_ln_b, s5_a_re, s5_a_im, s5_log_dt, s5_b_re, s5_b_im, s5_c_re, s5_c_im, s5_d, s5_glu_w, s5_glu_b, lru_conv_w, lru_conv_b, lru_wa, lru_ba, lru_wx, lru_bx, lru_lambda, w_out, ple_w, ple_norm_g, ple_gate_w, final_norm_g):
    depth = w_in.shape[0]
    perm, perm_t, tri, ones_pair = _constants()
    row = lambda v: v.reshape(1, -1).astype(F32)
    h = x.astype(F32)
    for i in range(depth):
        s5_b, s5_c, pow_re, pow_im = _s5_params(
            s5_a_re[i], s5_a_im[i], s5_log_dt[i], s5_b_re[i], s5_b_im[i], s5_c_re[i], s5_c_im[i])
        zero = jnp.zeros((LORA, BRANCH_W), F32)
        lora = jnp.concatenate([jnp.concatenate([rwkv_w2[i], zero], axis=1),
                                jnp.concatenate([zero, rwkv_a2[i]], axis=1)], axis=0)
        weights = {
            "norm_g": row(norm_g[i]),
            "w_in": w_in[i].astype(BF16),
            "rwkv_mu": row(rwkv_mu[i]),
            "rwkv_lora": lora.astype(BF16),
            "rwkv_w0": row(rwkv_w0[i]),
            "rwkv_a0": row(rwkv_a0[i]),
            "rwkv_k_k": row(rwkv_k_k[i]),
            "rwkv_k_a": row(rwkv_k_a[i]),
            "rwkv_r_k": row(rwkv_r_k[i]),
            "rwkv_ln_w": row(rwkv_ln_w[i]),
            "rwkv_ln_b": row(rwkv_ln_b[i]),
            "ones_pair": ones_pair,
            "chunk_tri": tri,
            "perm": perm,
            "perm_t": perm_t,
            "s5_b": s5_b,
            "s5_c": s5_c,
            "s5_pow_re": pow_re,
            "s5_pow_im": pow_im,
            "s5_powb_re": jnp.repeat(pow_re, NSEG, axis=0),
            "s5_powb_im": jnp.repeat(pow_im, NSEG, axis=0),
            "s5_d": row(s5_d[i]),
            "s5_glu_w": s5_glu_w[i].astype(BF16),
            "s5_glu_b": row(s5_glu_b[i]),
            "lru_conv_w": lru_conv_w[i].astype(F32),
            "lru_conv_b": row(lru_conv_b[i]),
            "lru_gates": jnp.stack([
                jnp.concatenate([_block_diag(lru_wa[i, 2 * j:2 * j + 2]), _block_diag(lru_wx[i, 2 * j:2 * j + 2])], axis=1)
                for j in range(PAIRS)]).astype(BF16),
            "lru_ba": row(lru_ba[i]),
            "lru_bx": row(lru_bx[i]),
            "lru_lambda": row(lru_lambda[i]),
            "w_out": w_out[i].astype(BF16),
            "ple_w": ple_w[i].astype(BF16),
            "ple_norm_g": row(ple_norm_g[i]),
            "ple_gate_w": ple_gate_w[i].astype(BF16),
            "final_norm_g": row(final_norm_g),
        }
        h = _layer_call(h, p[i].astype(F32), weights, is_last=(i == depth - 1))
    return h.astype(x.dtype)
```

```python
import functools
import math

import jax
import jax.numpy as jnp
import numpy as np
from jax import lax
from jax.experimental import pallas as pl
from jax.experimental.pallas import tpu as pltpu

F32 = jnp.float32
BF16 = jnp.bfloat16

D_MODEL = 1024
D_PLE = 256
BRANCH_W = 512
RWKV_HEADS = 8
HEAD_DIM = 64
PAIR_W = 2 * HEAD_DIM
PAIRS = RWKV_HEADS // 2
GRP_HEADS = 2
GRP_W = GRP_HEADS * HEAD_DIM
GROUPS = RWKV_HEADS // GRP_HEADS
LORA = 64
RWKV_GN_EPS = 64e-5
S5_GROUPS = 32
S5_GROUP = 16
S5_STATE = 64
S5_N = S5_GROUPS * S5_STATE
S5_BLOCKS = 4
S5_BLK_IN = BRANCH_W // S5_BLOCKS
S5_BLK_ST = S5_N // S5_BLOCKS
LRU_BLOCKS = 8
LRU_BLOCK_DIM = 64
CONV_WIDTH = 4
LRU_C = 8.0
NORM_EPS = 1e-6
RWKV_SHIFT_W = 3 * BRANCH_W + 2 * LORA
C_GRW = RWKV_SHIFT_W
C_US5 = C_GRW + BRANCH_W
C_GS5 = C_US5 + BRANCH_W
C_XLRU = C_GS5 + BRANCH_W
C_GLRU = C_XLRU + BRANCH_W
D_IN = C_GLRU + BRANCH_W

TL = 256
CHUNK = 64
NCH = TL // CHUNK
NSEG = 8
SEG = TL // NSEG
CARRY = 8
IN_COL_BLKS = ((0, 1280), (1280, 2560), (2560, D_IN))
SCAN_LANES = 512
SCAN_UNROLL = 4
VMEM_LIMIT = 56 * 1024 * 1024

VEC_TABLES = {
    "vec512": ((("rwkv_w0", 1), ("rwkv_a0", 1), ("rwkv_k_k", 1), ("rwkv_k_a", 1), ("rwkv_r_k", 1),
                ("rwkv_ln_w", 1), ("rwkv_ln_b", 1), ("s5_d", 1), ("s5_glu_b", 1), ("lru_conv_b", 1),
                ("lru_ba", 1), ("lru_bx", 1), ("lru_lambda", 1), ("lru_conv_w", CONV_WIDTH)), BRANCH_W),
    "vec1024": ((("norm_g", 1), ("ple_norm_g", 1), ("final_norm_g", 1)), D_MODEL),
}


def _dot(a, b):
    return jnp.dot(a.astype(BF16), b.astype(BF16), preferred_element_type=F32)


def _bdot(a, b, lhs_c, rhs_c):
    dims = (((lhs_c,), (rhs_c,)), ((0,), (0,)))
    return lax.dot_general(a.astype(BF16), b.astype(BF16), dims, preferred_element_type=F32)


def _hi_lo(x):
    hi = x.astype(BF16)
    lo = (x - hi.astype(F32)).astype(BF16)
    return hi, lo


def _dot_hilo(m, x):
    hi, lo = _hi_lo(x)
    return (jnp.dot(m, hi, preferred_element_type=F32)
            + jnp.dot(m, lo, preferred_element_type=F32))


def _sigmoid(x):
    return 1.0 / (1.0 + jnp.exp(-x))


def _silu(x):
    return x * _sigmoid(x)


def _softplus(x):
    return jnp.maximum(x, 0.0) + jnp.log(1.0 + jnp.exp(-jnp.abs(x)))


def _rmsnorm(x, g):
    return x * lax.rsqrt(jnp.mean(x * x, axis=-1, keepdims=True) + NORM_EPS) * g


def _split_pairs(x):
    x3 = x.reshape(NCH, CHUNK, BRANCH_W)
    return jnp.concatenate([x3[:, :, j * GRP_W:(j + 1) * GRP_W] for j in range(GROUPS)], axis=0)


def _merge_pairs(y):
    return jnp.concatenate([y[j * NCH:(j + 1) * NCH].reshape(TL, GRP_W) for j in range(GROUPS)], axis=-1)


def _per_chunk(x, c):
    return x.reshape((GROUPS, NCH) + x.shape[1:])[:, c]


def _bd(x):
    x = x.astype(BF16)
    head = lax.broadcasted_iota(jnp.int32, x.shape, 2) // HEAD_DIM
    zero = jnp.zeros_like(x)
    return jnp.concatenate([jnp.where(head == g, x, zero) for g in range(GRP_HEADS)], axis=1)


def _pair_nn(x, y):
    return _bdot(x, _bd(y), 2, 1)


def _head_sum(x, ones_pair):
    return jnp.concatenate([_dot(x[:, j * PAIR_W:(j + 1) * PAIR_W], ones_pair) for j in range(PAIRS)], axis=-1)


def _unit_lower_inverse_minus_eye(low, t_idx, s_idx):
    e = None
    size = 1
    while size < CHUNK:
        sibling = ((t_idx // (2 * size)) == (s_idx // (2 * size))) & ((t_idx // size) != (s_idx // size))
        off = jnp.where(sibling[None], low, 0.0)
        if e is None:
            e = -off
        else:
            a = off + _pair_nn(e, off)
            e = e - a - _pair_nn(a, e)
        size *= 2
    return e


def _rwkv_group(zin_ref, w, s_ref):
    rows = pl.ds(CARRY, TL)
    prev = pl.ds(CARRY - 1, TL)
    mu = w["rwkv_mu"][...]

    def shifted(lo, hi):
        z = zin_ref[rows, lo:hi]
        zp = zin_ref[prev, lo:hi]
        return z + mu[:, lo:hi] * (zp - z)

    r = shifted(0, BRANCH_W)
    k = shifted(BRANCH_W, 2 * BRANCH_W)
    v = shifted(2 * BRANCH_W, 3 * BRANCH_W)
    lora_in = shifted(3 * BRANCH_W, RWKV_SHIFT_W)
    lane = lax.broadcasted_iota(jnp.int32, lora_in.shape, 1)
    lora_in = jnp.where(lane < LORA, jnp.tanh(lora_in), lora_in)
    lora = _dot(lora_in, w["rwkv_lora"][...])
    wdec = -_softplus(-(w["rwkv_w0"][...] + lora[:, :BRANCH_W])) - 0.5
    logd = -jnp.exp(wdec)
    a = _sigmoid(w["rwkv_a0"][...] + lora[:, BRANCH_W:])

    ones_pair = w["ones_pair"][...]
    kk = k * w["rwkv_k_k"][...]
    kk = kk * lax.rsqrt(jnp.maximum(_head_sum(kk * kk, ones_pair), 1e-24))
    k = k * (1.0 + (a - 1.0) * w["rwkv_k_a"][...])
    b = kk * a

    g_inc = _dot_hilo(w["chunk_tri"][...], logd)
    g_inc3 = g_inc.reshape(NCH, CHUNK, BRANCH_W)
    g_rest = (g_inc3[:, CHUNK - 1:CHUNK, :] - g_inc3).reshape(TL, BRANCH_W)
    e_inc = jnp.exp(g_inc)
    e_neg = jnp.exp(-g_inc)
    e_rest = jnp.exp(g_rest)
    kkd = _split_pairs(kk * jnp.exp(g_inc - logd))
    rd = _split_pairs(r * e_inc)
    bi = _split_pairs(b * e_neg)
    ki = _split_pairs(k * e_neg)
    bd = _split_pairs(b * e_rest)
    kd = _split_pairs(k * e_rest)
    vp = _split_pairs(v)
    gamma_c = _split_pairs(e_inc)[:, CHUNK - 1:CHUNK, :]

    t_idx = lax.broadcasted_iota(jnp.int32, (CHUNK, GRP_W), 0)
    s_idx = lax.broadcasted_iota(jnp.int32, (CHUNK, GRP_W), 1) % HEAD_DIM
    strict = (s_idx < t_idx)[None]
    incl = (s_idx <= t_idx)[None]
    row = lax.broadcasted_iota(jnp.int32, (GRP_W, GRP_W), 0)
    col = lax.broadcasted_iota(jnp.int32, (GRP_W, GRP_W), 1)
    same_head = ((row // HEAD_DIM) == (col // HEAD_DIM))[None]
    eye = (row == col)[None]

    lhs = jnp.concatenate([kkd, rd], axis=1)
    a_b = _bdot(lhs, _bd(bi), 2, 2)
    a_k = _bdot(lhs, _bd(ki), 2, 2)
    lb = jnp.where(strict, a_b[:, :CHUNK], 0.0)
    mb = jnp.where(incl, a_b[:, CHUNK:], 0.0)
    lk = jnp.where(strict, a_k[:, :CHUNK], 0.0)
    mk = jnp.where(incl, a_k[:, CHUNK:], 0.0)

    e = _unit_lower_inverse_minus_eye(lb, t_idx, s_idx)
    lkv = _pair_nn(lk, vp)
    ex = _bdot(e, jnp.concatenate([_bd(kkd), _bd(lkv)], axis=2), 2, 1)
    wmat = kkd + ex[:, :, :GRP_W]
    uloc = -(lkv + ex[:, :, GRP_W:])
    pmat = jnp.where(eye, gamma_c, 0.0) - jnp.where(same_head, _bdot(wmat, bd, 1, 1), 0.0)
    qmat = jnp.where(same_head, _bdot(jnp.concatenate([uloc, vp], axis=1),
                                      jnp.concatenate([bd, kd], axis=1), 1, 1), 0.0)
    yloc = _bdot(jnp.concatenate([mb, mk], axis=2), jnp.concatenate([_bd(uloc), _bd(vp)], axis=1), 2, 1)
    reff = rd - _pair_nn(mb, wmat)

    state = s_ref[...]
    ys = []
    for c in range(NCH):
        ys.append(_per_chunk(yloc, c) + _bdot(_per_chunk(reff, c), state, 2, 2))
        state = _bdot(state, _per_chunk(pmat, c), 2, 1) + _per_chunk(qmat, c)
    s_ref[...] = state
    y = _merge_pairs(jnp.stack(ys, axis=1).reshape(GROUPS * NCH, CHUNK, GRP_W))

    inv_n = 1.0 / HEAD_DIM
    mean = _head_sum(y, ones_pair) * inv_n
    yc = y - mean
    var = _head_sum(yc * yc, ones_pair) * inv_n
    yn = yc * lax.rsqrt(var + RWKV_GN_EPS) * w["rwkv_ln_w"][...] + w["rwkv_ln_b"][...]
    bonus = _head_sum(r * k * w["rwkv_r_k"][...], ones_pair) * v
    gate = zin_ref[rows, C_GRW:C_GRW + BRANCH_W]
    return (yn + bonus) * _silu(gate)


def _s5_scan(zin_ref, w, xre_ref, xim_ref, st_re_ref, st_im_ref, start_re_ref, start_im_ref):
    u = zin_ref[pl.ds(CARRY, TL), C_US5:C_US5 + BRANCH_W]
    u_il = jnp.dot(w["perm"][...], u.astype(BF16), preferred_element_type=F32).astype(BF16)
    for blk in range(S5_BLOCKS):
        bu = jnp.dot(u_il[:, blk * S5_BLK_IN:(blk + 1) * S5_BLK_IN], w["s5_b"][blk],
                     preferred_element_type=F32)
        xre_ref[:, blk * S5_BLK_ST:(blk + 1) * S5_BLK_ST] = bu[:, :S5_BLK_ST]
        xim_ref[:, blk * S5_BLK_ST:(blk + 1) * S5_BLK_ST] = bu[:, S5_BLK_ST:]

    for lb in range(S5_N // SCAN_LANES):
        cols = slice(lb * SCAN_LANES, (lb + 1) * SCAN_LANES)
        lam_re = jnp.broadcast_to(w["s5_pow_re"][0:1, cols], (NSEG, SCAN_LANES))
        lam_im = jnp.broadcast_to(w["s5_pow_im"][0:1, cols], (NSEG, SCAN_LANES))

        def step(t, carry, cols=cols, lam_re=lam_re, lam_im=lam_im):
            sr, si = carry
            rws = pl.ds(pl.multiple_of(t * NSEG, NSEG), NSEG)
            nr = lam_re * sr - lam_im * si + xre_ref[rws, cols]
            ni = lam_re * si + lam_im * sr + xim_ref[rws, cols]
            xre_ref[rws, cols] = nr
            xim_ref[rws, cols] = ni
            return nr, ni

        zero = jnp.zeros((NSEG, SCAN_LANES), F32)
        lax.fori_loop(0, SEG, step, (zero, zero), unroll=SCAN_UNROLL)

    last = pl.ds(TL - NSEG, NSEG)
    end_re = xre_ref[last, :]
    end_im = xim_ref[last, :]
    seg_re = w["s5_pow_re"][SEG - 1:SEG, :]
    seg_im = w["s5_pow_im"][SEG - 1:SEG, :]
    cur_re = st_re_ref[...]
    cur_im = st_im_ref[...]
    for i in range(NSEG):
        start_re_ref[i:i + 1, :] = cur_re
        start_im_ref[i:i + 1, :] = cur_im
        nxt_re = end_re[i:i + 1, :] + seg_re * cur_re - seg_im * cur_im
        nxt_im = end_im[i:i + 1, :] + seg_re * cur_im + seg_im * cur_re
        cur_re, cur_im = nxt_re, nxt_im
    st_re_ref[...] = cur_re
    st_im_ref[...] = cur_im


def _s5_output(zin_ref, w, xre_ref, xim_ref, start_re_ref, start_im_ref):
    rows = pl.ds(CARRY, TL)
    u = zin_ref[rows, C_US5:C_US5 + BRANCH_W]
    ys = []
    seg3 = (SEG, NSEG, S5_BLK_ST)
    for blk in range(S5_BLOCKS):
        cols = slice(blk * S5_BLK_ST, (blk + 1) * S5_BLK_ST)
        s_re = start_re_ref[:, cols][None]
        s_im = start_im_ref[:, cols][None]
        p_re = w["s5_powb_re"][:, cols].reshape(seg3)
        p_im = w["s5_powb_im"][:, cols].reshape(seg3)
        x_re = xre_ref[:, cols].reshape(seg3) + (p_re * s_re - p_im * s_im)
        x_im = xim_ref[:, cols].reshape(seg3) + (p_re * s_im + p_im * s_re)
        x_ri = jnp.concatenate([x_re.reshape(TL, S5_BLK_ST).astype(BF16),
                                x_im.reshape(TL, S5_BLK_ST).astype(BF16)], axis=-1)
        ys.append(jnp.dot(x_ri, w["s5_c"][blk], preferred_element_type=F32))
    y_il = jnp.concatenate(ys, axis=-1)
    y = _dot(w["perm_t"][...], y_il) + w["s5_d"][...] * u
    zg = jax.nn.gelu(y)
    zg = zg * _sigmoid(_dot(zg, w["s5_glu_w"][...]) + w["s5_glu_b"][...])
    gate = zin_ref[rows, C_GS5:C_GS5 + BRANCH_W]
    return zg * _silu(gate)


def _lru_scan(zin_ref, w, h_ref, cp_ref, st_ref, start_ref):
    conv_w = w["lru_conv_w"][...]
    xc = w["lru_conv_b"][...] + zin_ref[pl.ds(CARRY, TL), C_XLRU:C_XLRU + BRANCH_W] * conv_w[CONV_WIDTH - 1:CONV_WIDTH, :]
    for back in range(1, CONV_WIDTH):
        tap = CONV_WIDTH - 1 - back
        xc = xc + zin_ref[pl.ds(CARRY - back, TL), C_XLRU:C_XLRU + BRANCH_W] * conv_w[tap:tap + 1, :]
    xc = _dot_hilo(w["perm"][...], xc)
    gates = [_dot(xc[:, j * PAIR_W:(j + 1) * PAIR_W], w["lru_gates"][j]) for j in range(PAIRS)]
    r = _sigmoid(jnp.concatenate([g[:, :PAIR_W] for g in gates], axis=-1) + w["lru_ba"][...])
    i = _sigmoid(jnp.concatenate([g[:, PAIR_W:] for g in gates], axis=-1) + w["lru_bx"][...])
    log_a = (-LRU_C * _softplus(-w["lru_lambda"][...])) * r
    a = jnp.exp(log_a)
    mult = jnp.sqrt(1.0 - jnp.exp(2.0 * log_a))
    h_ref[...] = mult * (i * xc)
    cp_ref[...] = a

    def step(t, carry):
        hs, cs = carry
        rws = pl.ds(pl.multiple_of(t * NSEG, NSEG), NSEG)
        a_t = cp_ref[rws, :]
        hs = a_t * hs + h_ref[rws, :]
        cs = a_t * cs
        h_ref[rws, :] = hs
        cp_ref[rws, :] = cs
        return hs, cs

    lax.fori_loop(0, SEG, step, (jnp.zeros((NSEG, BRANCH_W), F32), jnp.ones((NSEG, BRANCH_W), F32)),
                  unroll=SCAN_UNROLL)

    last = pl.ds(TL - NSEG, NSEG)
    end_h = h_ref[last, :]
    end_cp = cp_ref[last, :]
    cur = st_ref[...]
    for s in range(NSEG):
        start_ref[s:s + 1, :] = cur
        cur = end_h[s:s + 1, :] + end_cp[s:s + 1, :] * cur
    st_ref[...] = cur


def _lru_output(zin_ref, w, h_ref, cp_ref, start_ref):
    start = start_ref[...]
    h3 = h_ref[...].reshape(SEG, NSEG, BRANCH_W) + cp_ref[...].reshape(SEG, NSEG, BRANCH_W) * start[None]
    h = _dot(w["perm_t"][...], h3.reshape(TL, BRANCH_W))
    gate = zin_ref[pl.ds(CARRY, TL), C_GLRU:C_GLRU + BRANCH_W]
    return h * _silu(gate)


class _Rows:
    def __init__(self, ref, start, n=1):
        self.ref, self.start, self.n = ref, start, n

    def __getitem__(self, idx):
        assert idx is Ellipsis
        return self.ref[self.start:self.start + self.n, :]


def _layer_body(names, is_last, *refs):
    n_in = len(names)
    w = dict(zip(names, refs[:n_in]))
    for table, (members, _) in VEC_TABLES.items():
        start = 0
        for name, n_rows in members:
            w[name] = _Rows(w[table], start, n_rows)
            start += n_rows
    out_ref = refs[n_in]
    (zin_ref, s_ref, xre_ref, xim_ref, s5_re_ref, s5_im_ref, s5_start_re_ref, s5_start_im_ref,
     lru_h_ref, lru_cp_ref, lru_st_ref, lru_start_ref) = refs[n_in + 1:]

    @pl.when(pl.program_id(1) == 0)
    def _():
        zin_ref[0:CARRY, :] = jnp.zeros((CARRY, D_IN), F32)
        s_ref[...] = jnp.zeros_like(s_ref)
        s5_re_ref[...] = jnp.zeros_like(s5_re_ref)
        s5_im_ref[...] = jnp.zeros_like(s5_im_ref)
        lru_st_ref[...] = jnp.zeros_like(lru_st_ref)

    h = w["h"][0]
    xn = _rmsnorm(h, w["norm_g"][...]).astype(BF16)
    for c0, c1 in IN_COL_BLKS:
        zin_ref[pl.ds(CARRY, TL), c0:c1] = jnp.dot(xn, w["w_in"][:, c0:c1], preferred_element_type=F32)

    y_rw = _rwkv_group(zin_ref, w, s_ref)
    _s5_scan(zin_ref, w, xre_ref, xim_ref, s5_re_ref, s5_im_ref, s5_start_re_ref, s5_start_im_ref)
    y_s5 = _s5_output(zin_ref, w, xre_ref, xim_ref, s5_start_re_ref, s5_start_im_ref)
    _lru_scan(zin_ref, w, lru_h_ref, lru_cp_ref, lru_st_ref, lru_start_ref)
    y_lru = _lru_output(zin_ref, w, lru_h_ref, lru_cp_ref, lru_start_ref)
    zin_ref[0:CARRY, :] = zin_ref[TL:TL + CARRY, :]

    h = h + (_dot(y_rw, w["w_out"][0:BRANCH_W, :])
             + _dot(y_s5, w["w_out"][BRANCH_W:2 * BRANCH_W, :])
             + _dot(y_lru, w["w_out"][2 * BRANCH_W:3 * BRANCH_W, :]))
    e = _rmsnorm(_dot(w["p"][0], w["ple_w"][...]), w["ple_norm_g"][...])
    h = h + e * _sigmoid(_dot(h, w["ple_gate_w"][...]))
    if is_last:
        h = _rmsnorm(h, w["final_norm_g"][...])
    out_ref[0] = h


def _block_diag(blocks):
    n, a, b = blocks.shape[-3:]
    eye = jnp.eye(n, dtype=blocks.dtype)
    return jnp.einsum("...nab,nm->...namb", blocks, eye).reshape(blocks.shape[:-3] + (n * a, n * b))


def _constants():
    nat = np.arange(TL)
    interleaved_row = (nat % SEG) * NSEG + nat // SEG
    perm = np.zeros((TL, TL), np.float32)
    perm[interleaved_row, nat] = 1.0
    t = nat[:, None]
    s = nat[None, :]
    tri = ((t // CHUNK) == (s // CHUNK)) & (s <= t)
    lane = np.arange(PAIR_W)
    ones_pair = (lane[:, None] // HEAD_DIM) == (lane[None, :] // HEAD_DIM)
    as_bf16 = lambda m: jnp.asarray(m.astype(np.float32), dtype=BF16)
    return as_bf16(perm), as_bf16(perm.T), as_bf16(tri), as_bf16(ones_pair)


def _s5_params(a_re, a_im, log_dt, b_re, b_im, c_re, c_im):
    depth = a_re.shape[0]
    a_re, a_im = a_re.astype(F32), a_im.astype(F32)
    dt = jnp.exp(log_dt.astype(F32))[..., None]
    mag = jnp.exp(a_re * dt)
    lr, li = mag * jnp.cos(a_im * dt), mag * jnp.sin(a_im * dt)
    den = a_re * a_re + a_im * a_im
    f_re = ((lr - 1.0) * a_re + li * a_im) / den
    f_im = (li * a_re - (lr - 1.0) * a_im) / den
    b_re, b_im = b_re.astype(F32), b_im.astype(F32)
    bb_re = f_re[..., None] * b_re - f_im[..., None] * b_im
    bb_im = f_re[..., None] * b_im + f_im[..., None] * b_re
    steps = jnp.arange(1, SEG + 1, dtype=F32)[:, None, None]
    pmag = jnp.exp((a_re * dt)[:, None] * steps)
    ang = (a_im * dt)[:, None] * steps
    pow_re = (pmag * jnp.cos(ang)).reshape(depth, SEG, S5_N)
    pow_im = (pmag * jnp.sin(ang)).reshape(depth, SEG, S5_N)
    gpb = S5_GROUPS // S5_BLOCKS

    def blocks(x, rows, cols):
        return _block_diag(x.reshape(depth, S5_BLOCKS, gpb, rows, cols))

    s5_b = jnp.concatenate([blocks(jnp.swapaxes(bb_re, 2, 3), S5_GROUP, S5_STATE),
                            blocks(jnp.swapaxes(bb_im, 2, 3), S5_GROUP, S5_STATE)], axis=-1).astype(BF16)
    s5_c = jnp.concatenate([blocks(jnp.swapaxes(c_re.astype(F32), 2, 3), S5_STATE, S5_GROUP),
                            -blocks(jnp.swapaxes(c_im.astype(F32), 2, 3), S5_STATE, S5_GROUP)], axis=-2).astype(BF16)
    return s5_b, s5_c, pow_re, pow_im


def _layer_call(h, p, layer, stacked, shared, is_last):
    bsz, seq, _ = h.shape
    assert seq % TL == 0
    names = ["h", "p"] + list(stacked.keys()) + list(shared.keys())
    arrays = [h, p] + list(stacked.values()) + list(shared.values())
    in_specs = [
        pl.BlockSpec((1, TL, D_MODEL), lambda b, j: (b, j, 0)),
        pl.BlockSpec((None, 1, TL, D_PLE), lambda b, j: (layer, b, j, 0)),
    ]
    for arr in stacked.values():
        index = (layer,) + (0,) * (arr.ndim - 1)
        in_specs.append(pl.BlockSpec((None,) + arr.shape[1:], lambda b, j, index=index: index,
                                     pipeline_mode=pl.Buffered(1)))
    for arr in shared.values():
        index = (0,) * arr.ndim
        in_specs.append(pl.BlockSpec(arr.shape, lambda b, j, index=index: index,
                                     pipeline_mode=pl.Buffered(1)))
    scratch = [
        pltpu.VMEM((CARRY + TL, D_IN), F32),
        pltpu.VMEM((GROUPS, GRP_W, GRP_W), F32),
        pltpu.VMEM((TL, S5_N), F32),
        pltpu.VMEM((TL, S5_N), F32),
        pltpu.VMEM((1, S5_N), F32),
        pltpu.VMEM((1, S5_N), F32),
        pltpu.VMEM((NSEG, S5_N), F32),
        pltpu.VMEM((NSEG, S5_N), F32),
        pltpu.VMEM((TL, BRANCH_W), F32),
        pltpu.VMEM((TL, BRANCH_W), F32),
        pltpu.VMEM((1, BRANCH_W), F32),
        pltpu.VMEM((NSEG, BRANCH_W), F32),
    ]
    return pl.pallas_call(
        functools.partial(_layer_body, tuple(names), is_last),
        grid=(bsz, seq // TL),
        in_specs=in_specs,
        out_specs=pl.BlockSpec((1, TL, D_MODEL), lambda b, j: (b, j, 0)),
        out_shape=jax.ShapeDtypeStruct(h.shape, F32),
        scratch_shapes=scratch,
        compiler_params=pltpu.CompilerParams(
            dimension_semantics=("arbitrary", "arbitrary"),
            vmem_limit_bytes=VMEM_LIMIT),
        name="hybrid_layer",
    )(*arrays)


def kernel(x, p, norm_g, w_in, rwkv_mu, rwkv_w0, rwkv_w2, rwkv_a0, rwkv_a2, rwkv_k_k, rwkv_k_a, rwkv_r_k, rwkv_ln_w, rwkv_ln_b, s5_a_re, s5_a_im, s5_log_dt, s5_b_re, s5_b_im, s5_c_re, s5_c_im, s5_d, s5_glu_w, s5_glu_b, lru_conv_w, lru_conv_b, lru_wa, lru_ba, lru_wx, lru_bx, lru_lambda, w_out, ple_w, ple_norm_g, ple_gate_w, final_norm_g):
    depth = w_in.shape[0]
    perm, perm_t, tri, ones_pair = _constants()
    shared = {"ones_pair": ones_pair, "chunk_tri": tri, "perm": perm, "perm_t": perm_t}

    s5_b, s5_c, pow_re, pow_im = _s5_params(s5_a_re, s5_a_im, s5_log_dt, s5_b_re, s5_b_im, s5_c_re, s5_c_im)
    zero = jnp.zeros((depth, LORA, BRANCH_W), F32)
    lora = jnp.concatenate([jnp.concatenate([rwkv_w2, zero], axis=2),
                            jnp.concatenate([zero, rwkv_a2], axis=2)], axis=1)
    pair_blocks = lambda m: _block_diag(m.reshape(depth, PAIRS, 2, LRU_BLOCK_DIM, LRU_BLOCK_DIM))
    vectors = {
        "rwkv_w0": rwkv_w0, "rwkv_a0": rwkv_a0, "rwkv_k_k": rwkv_k_k, "rwkv_k_a": rwkv_k_a,
        "rwkv_r_k": rwkv_r_k.reshape(depth, BRANCH_W), "rwkv_ln_w": rwkv_ln_w, "rwkv_ln_b": rwkv_ln_b,
        "s5_d": s5_d, "s5_glu_b": s5_glu_b, "lru_conv_b": lru_conv_b, "lru_ba": lru_ba, "lru_bx": lru_bx,
        "lru_lambda": lru_lambda, "lru_conv_w": lru_conv_w,
        "norm_g": norm_g, "ple_norm_g": ple_norm_g,
        "final_norm_g": jnp.broadcast_to(final_norm_g, (depth, D_MODEL)),
    }
    stacked = {
        table: jnp.concatenate([vectors[name].astype(F32).reshape(depth, rows, width) for name, rows in members], axis=1)
        for table, (members, width) in VEC_TABLES.items()
    }
    stacked.update({
        "rwkv_mu": rwkv_mu.astype(F32)[:, None, :],
        "w_in": w_in.astype(BF16),
        "rwkv_lora": lora.astype(BF16),
        "s5_b": s5_b,
        "s5_c": s5_c,
        "s5_pow_re": pow_re,
        "s5_pow_im": pow_im,
        "s5_powb_re": jnp.repeat(pow_re, NSEG, axis=1),
        "s5_powb_im": jnp.repeat(pow_im, NSEG, axis=1),
        "s5_glu_w": s5_glu_w.astype(BF16),
        "lru_gates": jnp.concatenate([pair_blocks(lru_wa), pair_blocks(lru_wx)], axis=-1).astype(BF16),
        "w_out": w_out.astype(BF16),
        "ple_w": ple_w.astype(BF16),
        "ple_gate_w": ple_gate_w.astype(BF16),
    })
    h = x.astype(F32)
    p = p.astype(F32)
    for i in range(depth):
        h = _layer_call(h, p, i, stacked, shared, is_last=(i == depth - 1))
    return h.astype(x.dtype)
```

```python
import functools
import math

import jax
import jax.numpy as jnp
import numpy as np
from jax import lax
from jax.experimental import pallas as pl
from jax.experimental.pallas import tpu as pltpu

F32 = jnp.float32
BF16 = jnp.bfloat16

D_MODEL = 1024
D_PLE = 256
BRANCH_W = 512
RWKV_HEADS = 8
HEAD_DIM = 64
PAIR_W = 2 * HEAD_DIM
PAIRS = RWKV_HEADS // 2
GRP_HEADS = 2
GRP_W = GRP_HEADS * HEAD_DIM
GROUPS = RWKV_HEADS // GRP_HEADS
LORA = 64
RWKV_GN_EPS = 64e-5
S5_GROUPS = 32
S5_GROUP = 16
S5_STATE = 64
S5_N = S5_GROUPS * S5_STATE
S5_BLOCKS = 4
S5_BLK_IN = BRANCH_W // S5_BLOCKS
S5_BLK_ST = S5_N // S5_BLOCKS
LRU_BLOCKS = 8
LRU_BLOCK_DIM = 64
CONV_WIDTH = 4
LRU_C = 8.0
NORM_EPS = 1e-6
RWKV_SHIFT_W = 3 * BRANCH_W + 2 * LORA
C_GRW = RWKV_SHIFT_W
C_US5 = C_GRW + BRANCH_W
C_GS5 = C_US5 + BRANCH_W
C_XLRU = C_GS5 + BRANCH_W
C_GLRU = C_XLRU + BRANCH_W
D_IN = C_GLRU + BRANCH_W

TL = 256
CHUNK = 64
NCH = TL // CHUNK
NSEG = 8
SEG = TL // NSEG
CARRY = 8
IN_COL_BLKS = ((0, 1280), (1280, 2560), (2560, D_IN))
SCAN_LANES = 512
SCAN_UNROLL = 4
VMEM_LIMIT = 56 * 1024 * 1024

VEC_TABLES = {
    "vec512": ((("rwkv_w0", 1), ("rwkv_a0", 1), ("rwkv_k_k", 1), ("rwkv_k_a", 1), ("rwkv_r_k", 1),
                ("rwkv_ln_w", 1), ("rwkv_ln_b", 1), ("s5_d", 1), ("s5_glu_b", 1), ("lru_conv_b", 1),
                ("lru_ba", 1), ("lru_bx", 1), ("lru_lambda", 1), ("lru_conv_w", CONV_WIDTH)), BRANCH_W),
    "vec1024": ((("norm_g", 1), ("ple_norm_g", 1), ("final_norm_g", 1)), D_MODEL),
}


def _dot(a, b):
    return jnp.dot(a.astype(BF16), b.astype(BF16), preferred_element_type=F32)


def _bdot(a, b, lhs_c, rhs_c):
    dims = (((lhs_c,), (rhs_c,)), ((0,), (0,)))
    return lax.dot_general(a.astype(BF16), b.astype(BF16), dims, preferred_element_type=F32)


def _hi_lo(x):
    hi = x.astype(BF16)
    lo = (x - hi.astype(F32)).astype(BF16)
    return hi, lo


def _dot_hilo(m, x):
    hi, lo = _hi_lo(x)
    return (jnp.dot(m, hi, preferred_element_type=F32)
            + jnp.dot(m, lo, preferred_element_type=F32))


def _sigmoid(x):
    return 1.0 / (1.0 + jnp.exp(-x))


def _silu(x):
    return x * _sigmoid(x)


def _softplus(x):
    return jnp.maximum(x, 0.0) + jnp.log(1.0 + jnp.exp(-jnp.abs(x)))


def _rmsnorm(x, g):
    return x * lax.rsqrt(jnp.mean(x * x, axis=-1, keepdims=True) + NORM_EPS) * g


def _split_pairs(x):
    x3 = x.reshape(NCH, CHUNK, BRANCH_W)
    return jnp.concatenate([x3[:, :, j * GRP_W:(j + 1) * GRP_W] for j in range(GROUPS)], axis=0)


def _merge_pairs(y):
    return jnp.concatenate([y[j * NCH:(j + 1) * NCH].reshape(TL, GRP_W) for j in range(GROUPS)], axis=-1)


def _per_chunk(x, c):
    return x.reshape((GROUPS, NCH) + x.shape[1:])[:, c]


def _bd(x):
    x = x.astype(BF16)
    head = lax.broadcasted_iota(jnp.int32, x.shape, 2) // HEAD_DIM
    zero = jnp.zeros_like(x)
    return jnp.concatenate([jnp.where(head == g, x, zero) for g in range(GRP_HEADS)], axis=1)


def _pair_nn(x, y):
    return _bdot(x, _bd(y), 2, 1)


def _head_sum(x, ones_pair):
    return jnp.concatenate([_dot(x[:, j * PAIR_W:(j + 1) * PAIR_W], ones_pair) for j in range(PAIRS)], axis=-1)


def _unit_lower_inverse_minus_eye(low, t_idx, s_idx):
    e = None
    size = 1
    while size < CHUNK:
        sibling = ((t_idx // (2 * size)) == (s_idx // (2 * size))) & ((t_idx // size) != (s_idx // size))
        off = jnp.where(sibling[None], low, 0.0)
        if e is None:
            e = -off
        else:
            a = off + _pair_nn(e, off)
            e = e - a - _pair_nn(a, e)
        size *= 2
    return e


def _rwkv_group(zin_ref, w, s_ref):
    rows = pl.ds(CARRY, TL)
    prev = pl.ds(CARRY - 1, TL)
    mu = w["rwkv_mu"][...]

    def shifted(lo, hi):
        z = zin_ref[rows, lo:hi]
        zp = zin_ref[prev, lo:hi]
        return z + mu[:, lo:hi] * (zp - z)

    r = shifted(0, BRANCH_W)
    k = shifted(BRANCH_W, 2 * BRANCH_W)
    v = shifted(2 * BRANCH_W, 3 * BRANCH_W)
    lora_in = shifted(3 * BRANCH_W, RWKV_SHIFT_W)
    lane = lax.broadcasted_iota(jnp.int32, lora_in.shape, 1)
    lora_in = jnp.where(lane < LORA, jnp.tanh(lora_in), lora_in)
    lora = _dot(lora_in, w["rwkv_lora"][...])
    logd = -math.exp(-0.5) * _sigmoid(w["rwkv_w0"][...] + lora[:, :BRANCH_W])
    a = _sigmoid(w["rwkv_a0"][...] + lora[:, BRANCH_W:])

    ones_pair = w["ones_pair"][...]
    kk = k * w["rwkv_k_k"][...]
    kk = kk * lax.rsqrt(jnp.maximum(_head_sum(kk * kk, ones_pair), 1e-24))
    k = k * (1.0 + (a - 1.0) * w["rwkv_k_a"][...])
    b = kk * a

    g_inc = _dot_hilo(w["chunk_tri"][...], logd)
    e_inc = jnp.exp(g_inc)
    e_neg = jnp.exp(-g_inc)
    kkd = _split_pairs(kk * jnp.exp(g_inc - logd))
    rd = _split_pairs(r * e_inc)
    bi = _split_pairs(b * e_neg)
    ki = _split_pairs(k * e_neg)
    vp = _split_pairs(v)
    gamma_c = _split_pairs(e_inc)[:, CHUNK - 1:CHUNK, :]
    bd = bi * gamma_c
    kd = ki * gamma_c

    t_idx = lax.broadcasted_iota(jnp.int32, (CHUNK, GRP_W), 0)
    s_idx = lax.broadcasted_iota(jnp.int32, (CHUNK, GRP_W), 1) % HEAD_DIM
    strict = (s_idx < t_idx)[None]
    incl = (s_idx <= t_idx)[None]
    row = lax.broadcasted_iota(jnp.int32, (GRP_W, GRP_W), 0)
    col = lax.broadcasted_iota(jnp.int32, (GRP_W, GRP_W), 1)
    same_head = ((row // HEAD_DIM) == (col // HEAD_DIM))[None]
    eye = (row == col)[None]

    lhs = jnp.concatenate([kkd, rd], axis=1)
    a_b = _bdot(lhs, _bd(bi), 2, 2)
    a_k = _bdot(lhs, _bd(ki), 2, 2)
    lb = jnp.where(strict, a_b[:, :CHUNK], 0.0)
    mb = jnp.where(incl, a_b[:, CHUNK:], 0.0)
    lk = jnp.where(strict, a_k[:, :CHUNK], 0.0)
    mk = jnp.where(incl, a_k[:, CHUNK:], 0.0)

    e = _unit_lower_inverse_minus_eye(lb, t_idx, s_idx)
    lkv = _pair_nn(lk, vp)
    ex = _bdot(e, jnp.concatenate([_bd(kkd), _bd(lkv)], axis=2), 2, 1)
    wmat = kkd + ex[:, :, :GRP_W]
    uloc = -(lkv + ex[:, :, GRP_W:])
    pmat = jnp.where(eye, gamma_c, 0.0) - jnp.where(same_head, _bdot(wmat, bd, 1, 1), 0.0)
    qmat = jnp.where(same_head, _bdot(jnp.concatenate([uloc, vp], axis=1),
                                      jnp.concatenate([bd, kd], axis=1), 1, 1), 0.0)
    yloc = _bdot(jnp.concatenate([mb, mk], axis=2), jnp.concatenate([_bd(uloc), _bd(vp)], axis=1), 2, 1)
    reff = rd - _pair_nn(mb, wmat)

    state = s_ref[...]
    ys = []
    for c in range(NCH):
        ys.append(_per_chunk(yloc, c) + _bdot(_per_chunk(reff, c), state, 2, 2))
        state = _bdot(state, _per_chunk(pmat, c), 2, 1) + _per_chunk(qmat, c)
    s_ref[...] = state
    y = _merge_pairs(jnp.stack(ys, axis=1).reshape(GROUPS * NCH, CHUNK, GRP_W))

    inv_n = 1.0 / HEAD_DIM
    mean = _head_sum(y, ones_pair) * inv_n
    yc = y - mean
    var = _head_sum(yc * yc, ones_pair) * inv_n
    yn = yc * lax.rsqrt(var + RWKV_GN_EPS) * w["rwkv_ln_w"][...] + w["rwkv_ln_b"][...]
    bonus = _head_sum(r * k * w["rwkv_r_k"][...], ones_pair) * v
    gate = zin_ref[rows, C_GRW:C_GRW + BRANCH_W]
    return (yn + bonus) * _silu(gate)


def _s5_scan(zin_ref, w, xre_ref, xim_ref, st_re_ref, st_im_ref, start_re_ref, start_im_ref):
    u = zin_ref[pl.ds(CARRY, TL), C_US5:C_US5 + BRANCH_W]
    u_il = jnp.dot(w["perm"][...], u.astype(BF16), preferred_element_type=F32).astype(BF16)
    for blk in range(S5_BLOCKS):
        bu = jnp.dot(u_il[:, blk * S5_BLK_IN:(blk + 1) * S5_BLK_IN], w["s5_b"][blk],
                     preferred_element_type=F32)
        xre_ref[:, blk * S5_BLK_ST:(blk + 1) * S5_BLK_ST] = bu[:, :S5_BLK_ST]
        xim_ref[:, blk * S5_BLK_ST:(blk + 1) * S5_BLK_ST] = bu[:, S5_BLK_ST:]

    for lb in range(S5_N // SCAN_LANES):
        cols = slice(lb * SCAN_LANES, (lb + 1) * SCAN_LANES)
        lam_re = jnp.broadcast_to(w["s5_pow_re"][0:1, cols], (NSEG, SCAN_LANES))
        lam_im = jnp.broadcast_to(w["s5_pow_im"][0:1, cols], (NSEG, SCAN_LANES))

        def step(t, carry, cols=cols, lam_re=lam_re, lam_im=lam_im):
            sr, si = carry
            rws = pl.ds(pl.multiple_of(t * NSEG, NSEG), NSEG)
            nr = lam_re * sr - lam_im * si + xre_ref[rws, cols]
            ni = lam_re * si + lam_im * sr + xim_ref[rws, cols]
            xre_ref[rws, cols] = nr
            xim_ref[rws, cols] = ni
            return nr, ni

        zero = jnp.zeros((NSEG, SCAN_LANES), F32)
        lax.fori_loop(0, SEG, step, (zero, zero), unroll=SCAN_UNROLL)

    last = pl.ds(TL - NSEG, NSEG)
    end_re = xre_ref[last, :]
    end_im = xim_ref[last, :]
    seg_re = w["s5_pow_re"][SEG - 1:SEG, :]
    seg_im = w["s5_pow_im"][SEG - 1:SEG, :]
    cur_re = st_re_ref[...]
    cur_im = st_im_ref[...]
    for i in range(NSEG):
        start_re_ref[i:i + 1, :] = cur_re
        start_im_ref[i:i + 1, :] = cur_im
        nxt_re = end_re[i:i + 1, :] + seg_re * cur_re - seg_im * cur_im
        nxt_im = end_im[i:i + 1, :] + seg_re * cur_im + seg_im * cur_re
        cur_re, cur_im = nxt_re, nxt_im
    st_re_ref[...] = cur_re
    st_im_ref[...] = cur_im


def _s5_output(zin_ref, w, xre_ref, xim_ref, start_re_ref, start_im_ref):
    rows = pl.ds(CARRY, TL)
    u = zin_ref[rows, C_US5:C_US5 + BRANCH_W]
    ys = []
    seg3 = (SEG, NSEG, S5_BLK_ST)
    for blk in range(S5_BLOCKS):
        cols = slice(blk * S5_BLK_ST, (blk + 1) * S5_BLK_ST)
        s_re = start_re_ref[:, cols][None]
        s_im = start_im_ref[:, cols][None]
        p_re = w["s5_powb_re"][:, cols].reshape(seg3)
        p_im = w["s5_powb_im"][:, cols].reshape(seg3)
        x_re = xre_ref[:, cols].reshape(seg3) + (p_re * s_re - p_im * s_im)
        x_im = xim_ref[:, cols].reshape(seg3) + (p_re * s_im + p_im * s_re)
        x_ri = jnp.concatenate([x_re.reshape(TL, S5_BLK_ST).astype(BF16),
                                x_im.reshape(TL, S5_BLK_ST).astype(BF16)], axis=-1)
        ys.append(jnp.dot(x_ri, w["s5_c"][blk], preferred_element_type=F32))
    y_il = jnp.concatenate(ys, axis=-1)
    y = _dot(w["perm_t"][...], y_il) + w["s5_d"][...] * u
    zg = jax.nn.gelu(y)
    zg = zg * _sigmoid(_dot(zg, w["s5_glu_w"][...]) + w["s5_glu_b"][...])
    gate = zin_ref[rows, C_GS5:C_GS5 + BRANCH_W]
    return zg * _silu(gate)


def _lru_scan(zin_ref, w, h_ref, cp_ref, st_ref, start_ref):
    conv_w = w["lru_conv_w"][...]
    xc = w["lru_conv_b"][...] + zin_ref[pl.ds(CARRY, TL), C_XLRU:C_XLRU + BRANCH_W] * conv_w[CONV_WIDTH - 1:CONV_WIDTH, :]
    for back in range(1, CONV_WIDTH):
        tap = CONV_WIDTH - 1 - back
        xc = xc + zin_ref[pl.ds(CARRY - back, TL), C_XLRU:C_XLRU + BRANCH_W] * conv_w[tap:tap + 1, :]
    xc = _dot_hilo(w["perm"][...], xc)
    gates = [_dot(xc[:, j * PAIR_W:(j + 1) * PAIR_W], w["lru_gates"][j]) for j in range(PAIRS)]
    r = _sigmoid(jnp.concatenate([g[:, :PAIR_W] for g in gates], axis=-1) + w["lru_ba"][...])
    i = _sigmoid(jnp.concatenate([g[:, PAIR_W:] for g in gates], axis=-1) + w["lru_bx"][...])
    log_a = (-LRU_C * _softplus(-w["lru_lambda"][...])) * r
    a = jnp.exp(log_a)
    one_minus_a2 = 1.0 - a * a
    mult = jnp.where(one_minus_a2 > 0.0, one_minus_a2 * lax.rsqrt(one_minus_a2), 0.0)
    h_ref[...] = mult * (i * xc)
    cp_ref[...] = a

    def step(t, carry):
        hs, cs = carry
        rws = pl.ds(pl.multiple_of(t * NSEG, NSEG), NSEG)
        a_t = cp_ref[rws, :]
        hs = a_t * hs + h_ref[rws, :]
        cs = a_t * cs
        h_ref[rws, :] = hs
        cp_ref[rws, :] = cs
        return hs, cs

    lax.fori_loop(0, SEG, step, (jnp.zeros((NSEG, BRANCH_W), F32), jnp.ones((NSEG, BRANCH_W), F32)),
                  unroll=SCAN_UNROLL)

    last = pl.ds(TL - NSEG, NSEG)
    end_h = h_ref[last, :]
    end_cp = cp_ref[last, :]
    cur = st_ref[...]
    for s in range(NSEG):
        start_ref[s:s + 1, :] = cur
        cur = end_h[s:s + 1, :] + end_cp[s:s + 1, :] * cur
    st_ref[...] = cur


def _lru_output(zin_ref, w, h_ref, cp_ref, start_ref):
    start = start_ref[...]
    h3 = h_ref[...].reshape(SEG, NSEG, BRANCH_W) + cp_ref[...].reshape(SEG, NSEG, BRANCH_W) * start[None]
    h = _dot(w["perm_t"][...], h3.reshape(TL, BRANCH_W))
    gate = zin_ref[pl.ds(CARRY, TL), C_GLRU:C_GLRU + BRANCH_W]
    return h * _silu(gate)


class _Rows:
    def __init__(self, ref, start, n=1):
        self.ref, self.start, self.n = ref, start, n

    def __getitem__(self, idx):
        assert idx is Ellipsis
        return self.ref[self.start:self.start + self.n, :]


def _layer_body(names, is_last, *refs):
    n_in = len(names)
    w = dict(zip(names, refs[:n_in]))
    for table, (members, _) in VEC_TABLES.items():
        start = 0
        for name, n_rows in members:
            w[name] = _Rows(w[table], start, n_rows)
            start += n_rows
    out_ref = refs[n_in]
    (zin_ref, s_ref, xre_ref, xim_ref, s5_re_ref, s5_im_ref, s5_start_re_ref, s5_start_im_ref,
     lru_h_ref, lru_cp_ref, lru_st_ref, lru_start_ref) = refs[n_in + 1:]

    @pl.when(pl.program_id(1) == 0)
    def _():
        zin_ref[0:CARRY, :] = jnp.zeros((CARRY, D_IN), F32)
        s_ref[...] = jnp.zeros_like(s_ref)
        s5_re_ref[...] = jnp.zeros_like(s5_re_ref)
        s5_im_ref[...] = jnp.zeros_like(s5_im_ref)
        lru_st_ref[...] = jnp.zeros_like(lru_st_ref)

    h = w["h"][0]
    xn = _rmsnorm(h, w["norm_g"][...]).astype(BF16)
    for c0, c1 in IN_COL_BLKS:
        zin_ref[pl.ds(CARRY, TL), c0:c1] = jnp.dot(xn, w["w_in"][:, c0:c1], preferred_element_type=F32)

    y_rw = _rwkv_group(zin_ref, w, s_ref)
    _s5_scan(zin_ref, w, xre_ref, xim_ref, s5_re_ref, s5_im_ref, s5_start_re_ref, s5_start_im_ref)
    y_s5 = _s5_output(zin_ref, w, xre_ref, xim_ref, s5_start_re_ref, s5_start_im_ref)
    _lru_scan(zin_ref, w, lru_h_ref, lru_cp_ref, lru_st_ref, lru_start_ref)
    y_lru = _lru_output(zin_ref, w, lru_h_ref, lru_cp_ref, lru_start_ref)
    zin_ref[0:CARRY, :] = zin_ref[TL:TL + CARRY, :]

    h = h + (_dot(y_rw, w["w_out"][0:BRANCH_W, :])
             + _dot(y_s5, w["w_out"][BRANCH_W:2 * BRANCH_W, :])
             + _dot(y_lru, w["w_out"][2 * BRANCH_W:3 * BRANCH_W, :]))
    e = _rmsnorm(_dot(w["p"][0], w["ple_w"][...]), w["ple_norm_g"][...])
    h = h + e * _sigmoid(_dot(h, w["ple_gate_w"][...]))
    if is_last:
        h = _rmsnorm(h, w["final_norm_g"][...])
    out_ref[0] = h


def _block_diag(blocks):
    n, a, b = blocks.shape[-3:]
    eye = jnp.eye(n, dtype=blocks.dtype)
    return jnp.einsum("...nab,nm->...namb", blocks, eye).reshape(blocks.shape[:-3] + (n * a, n * b))


def _constants():
    nat = np.arange(TL)
    interleaved_row = (nat % SEG) * NSEG + nat // SEG
    perm = np.zeros((TL, TL), np.float32)
    perm[interleaved_row, nat] = 1.0
    t = nat[:, None]
    s = nat[None, :]
    tri = ((t // CHUNK) == (s // CHUNK)) & (s <= t)
    lane = np.arange(PAIR_W)
    ones_pair = (lane[:, None] // HEAD_DIM) == (lane[None, :] // HEAD_DIM)
    as_bf16 = lambda m: jnp.asarray(m.astype(np.float32), dtype=BF16)
    return as_bf16(perm), as_bf16(perm.T), as_bf16(tri), as_bf16(ones_pair)


def _s5_params(a_re, a_im, log_dt, b_re, b_im, c_re, c_im):
    depth = a_re.shape[0]
    a_re, a_im = a_re.astype(F32), a_im.astype(F32)
    dt = jnp.exp(log_dt.astype(F32))[..., None]
    mag = jnp.exp(a_re * dt)
    lr, li = mag * jnp.cos(a_im * dt), mag * jnp.sin(a_im * dt)
    den = a_re * a_re + a_im * a_im
    f_re = ((lr - 1.0) * a_re + li * a_im) / den
    f_im = (li * a_re - (lr - 1.0) * a_im) / den
    b_re, b_im = b_re.astype(F32), b_im.astype(F32)
    bb_re = f_re[..., None] * b_re - f_im[..., None] * b_im
    bb_im = f_re[..., None] * b_im + f_im[..., None] * b_re
    steps = jnp.arange(1, SEG + 1, dtype=F32)[:, None, None]
    pmag = jnp.exp((a_re * dt)[:, None] * steps)
    ang = (a_im * dt)[:, None] * steps
    pow_re = (pmag * jnp.cos(ang)).reshape(depth, SEG, S5_N)
    pow_im = (pmag * jnp.sin(ang)).reshape(depth, SEG, S5_N)
    gpb = S5_GROUPS // S5_BLOCKS

    def blocks(x, rows, cols):
        return _block_diag(x.reshape(depth, S5_BLOCKS, gpb, rows, cols))

    s5_b = jnp.concatenate([blocks(jnp.swapaxes(bb_re, 2, 3), S5_GROUP, S5_STATE),
                            blocks(jnp.swapaxes(bb_im, 2, 3), S5_GROUP, S5_STATE)], axis=-1).astype(BF16)
    s5_c = jnp.concatenate([blocks(jnp.swapaxes(c_re.astype(F32), 2, 3), S5_STATE, S5_GROUP),
                            -blocks(jnp.swapaxes(c_im.astype(F32), 2, 3), S5_STATE, S5_GROUP)], axis=-2).astype(BF16)
    return s5_b, s5_c, pow_re, pow_im


def _layer_call(h, p, layer, stacked, shared, is_last):
    bsz, seq, _ = h.shape
    assert seq % TL == 0
    names = ["h", "p"] + list(stacked.keys()) + list(shared.keys())
    arrays = [h, p] + list(stacked.values()) + list(shared.values())
    in_specs = [
        pl.BlockSpec((1, TL, D_MODEL), lambda b, j: (b, j, 0)),
        pl.BlockSpec((None, 1, TL, D_PLE), lambda b, j: (layer, b, j, 0)),
    ]
    for arr in stacked.values():
        index = (layer,) + (0,) * (arr.ndim - 1)
        in_specs.append(pl.BlockSpec((None,) + arr.shape[1:], lambda b, j, index=index: index,
                                     pipeline_mode=pl.Buffered(1)))
    for arr in shared.values():
        index = (0,) * arr.ndim
        in_specs.append(pl.BlockSpec(arr.shape, lambda b, j, index=index: index,
                                     pipeline_mode=pl.Buffered(1)))
    scratch = [
        pltpu.VMEM((CARRY + TL, D_IN), F32),
        pltpu.VMEM((GROUPS, GRP_W, GRP_W), F32),
        pltpu.VMEM((TL, S5_N), F32),
        pltpu.VMEM((TL, S5_N), F32),
        pltpu.VMEM((1, S5_N), F32),
        pltpu.VMEM((1, S5_N), F32),
        pltpu.VMEM((NSEG, S5_N), F32),
        pltpu.VMEM((NSEG, S5_N), F32),
        pltpu.VMEM((TL, BRANCH_W), F32),
        pltpu.VMEM((TL, BRANCH_W), F32),
        pltpu.VMEM((1, BRANCH_W), F32),
        pltpu.VMEM((NSEG, BRANCH_W), F32),
    ]
    return pl.pallas_call(
        functools.partial(_layer_body, tuple(names), is_last),
        grid=(bsz, seq // TL),
        in_specs=in_specs,
        out_specs=pl.BlockSpec((1, TL, D_MODEL), lambda b, j: (b, j, 0)),
        out_shape=jax.ShapeDtypeStruct(h.shape, F32),
        scratch_shapes=scratch,
        compiler_params=pltpu.CompilerParams(
            dimension_semantics=("arbitrary", "arbitrary"),
            vmem_limit_bytes=VMEM_LIMIT),
        name="hybrid_layer",
    )(*arrays)


def kernel(x, p, norm_g, w_in, rwkv_mu, rwkv_w0, rwkv_w2, rwkv_a0, rwkv_a2, rwkv_k_k, rwkv_k_a, rwkv_r_k, rwkv_ln_w, rwkv_ln_b, s5_a_re, s5_a_im, s5_log_dt, s5_b_re, s5_b_im, s5_c_re, s5_c_im, s5_d, s5_glu_w, s5_glu_b, lru_conv_w, lru_conv_b, lru_wa, lru_ba, lru_wx, lru_bx, lru_lambda, w_out, ple_w, ple_norm_g, ple_gate_w, final_norm_g):
    depth = w_in.shape[0]
    perm, perm_t, tri, ones_pair = _constants()
    shared = {"ones_pair": ones_pair, "chunk_tri": tri, "perm": perm, "perm_t": perm_t}

    s5_b, s5_c, pow_re, pow_im = _s5_params(s5_a_re, s5_a_im, s5_log_dt, s5_b_re, s5_b_im, s5_c_re, s5_c_im)
    zero = jnp.zeros((depth, LORA, BRANCH_W), F32)
    lora = jnp.concatenate([jnp.concatenate([rwkv_w2, zero], axis=2),
                            jnp.concatenate([zero, rwkv_a2], axis=2)], axis=1)
    pair_blocks = lambda m: _block_diag(m.reshape(depth, PAIRS, 2, LRU_BLOCK_DIM, LRU_BLOCK_DIM))
    vectors = {
        "rwkv_w0": rwkv_w0, "rwkv_a0": rwkv_a0, "rwkv_k_k": rwkv_k_k, "rwkv_k_a": rwkv_k_a,
        "rwkv_r_k": rwkv_r_k.reshape(depth, BRANCH_W), "rwkv_ln_w": rwkv_ln_w, "rwkv_ln_b": rwkv_ln_b,
        "s5_d": s5_d, "s5_glu_b": s5_glu_b, "lru_conv_b": lru_conv_b, "lru_ba": lru_ba, "lru_bx": lru_bx,
        "lru_lambda": lru_lambda, "lru_conv_w": lru_conv_w,
        "norm_g": norm_g, "ple_norm_g": ple_norm_g,
        "final_norm_g": jnp.broadcast_to(final_norm_g, (depth, D_MODEL)),
    }
    stacked = {
        table: jnp.concatenate([vectors[name].astype(F32).reshape(depth, rows, width) for name, rows in members], axis=1)
        for table, (members, width) in VEC_TABLES.items()
    }
    stacked.update({
        "rwkv_mu": rwkv_mu.astype(F32)[:, None, :],
        "w_in": w_in.astype(BF16),
        "rwkv_lora": lora.astype(BF16),
        "s5_b": s5_b,
        "s5_c": s5_c,
        "s5_pow_re": pow_re,
        "s5_pow_im": pow_im,
        "s5_powb_re": jnp.repeat(pow_re, NSEG, axis=1),
        "s5_powb_im": jnp.repeat(pow_im, NSEG, axis=1),
        "s5_glu_w": s5_glu_w.astype(BF16),
        "lru_gates": jnp.concatenate([pair_blocks(lru_wa), pair_blocks(lru_wx)], axis=-1).astype(BF16),
        "w_out": w_out.astype(BF16),
        "ple_w": ple_w.astype(BF16),
        "ple_gate_w": ple_gate_w.astype(BF16),
    })
    h = x.astype(F32)
    p = p.astype(F32)
    for i in range(depth):
        h = _layer_call(h, p, i, stacked, shared, is_last=(i == depth - 1))
    return h.astype(x.dtype)
```

```python
import functools
import math

import jax
import jax.numpy as jnp
import numpy as np
from jax import lax
from jax.experimental import pallas as pl
from jax.experimental.pallas import tpu as pltpu

F32 = jnp.float32
BF16 = jnp.bfloat16

D_MODEL = 1024
D_PLE = 256
BRANCH_W = 512
RWKV_HEADS = 8
HEAD_DIM = 64
PAIR_W = 2 * HEAD_DIM
PAIRS = RWKV_HEADS // 2
GRP_HEADS = 2
GRP_W = GRP_HEADS * HEAD_DIM
GROUPS = RWKV_HEADS // GRP_HEADS
LORA = 64
RWKV_GN_EPS = 64e-5
S5_GROUPS = 32
S5_GROUP = 16
S5_STATE = 64
S5_N = S5_GROUPS * S5_STATE
S5_BLOCKS = 4
S5_BLK_IN = BRANCH_W // S5_BLOCKS
S5_BLK_ST = S5_N // S5_BLOCKS
LRU_BLOCKS = 8
LRU_BLOCK_DIM = 64
CONV_WIDTH = 4
LRU_C = 8.0
NORM_EPS = 1e-6
RWKV_SHIFT_W = 3 * BRANCH_W + 2 * LORA
C_GRW = RWKV_SHIFT_W
C_US5 = C_GRW + BRANCH_W
C_GS5 = C_US5 + BRANCH_W
C_XLRU = C_GS5 + BRANCH_W
C_GLRU = C_XLRU + BRANCH_W
D_IN = C_GLRU + BRANCH_W

TL = 256
CHUNK = 64
NCH = TL // CHUNK
NSEG = 8
SEG = TL // NSEG
CARRY = 8
BF16_ROWS = 16
IN_COL_BLKS = ((0, 1280), (1280, 2560), (2560, D_IN))
SCAN_LANES = 512
SCAN_UNROLL = 4
VMEM_LIMIT = 56 * 1024 * 1024

VEC_TABLES = {
    "vec512": ((("rwkv_w0", 1), ("rwkv_a0", 1), ("rwkv_k_k", 1), ("rwkv_k_a", 1), ("rwkv_r_k", 1),
                ("rwkv_ln_w", 1), ("rwkv_ln_b", 1), ("s5_d", 1), ("s5_glu_b", 1), ("lru_conv_b", 1),
                ("lru_ba", 1), ("lru_bx", 1), ("lru_lambda", 1), ("lru_conv_w", CONV_WIDTH)), BRANCH_W),
    "vec1024": ((("norm_g", 1), ("ple_norm_g", 1), ("final_norm_g", 1)), D_MODEL),
}


def _dot(a, b):
    return jnp.dot(a.astype(BF16), b.astype(BF16), preferred_element_type=F32)


def _bdot(a, b, lhs_c, rhs_c):
    dims = (((lhs_c,), (rhs_c,)), ((0,), (0,)))
    return lax.dot_general(a.astype(BF16), b.astype(BF16), dims, preferred_element_type=F32)


def _hi_lo(x):
    hi = x.astype(BF16)
    lo = (x - hi.astype(F32)).astype(BF16)
    return hi, lo


def _dot_hilo(m, x):
    hi, lo = _hi_lo(x)
    return (jnp.dot(m, hi, preferred_element_type=F32)
            + jnp.dot(m, lo, preferred_element_type=F32))


def _sigmoid(x):
    return 1.0 / (1.0 + jnp.exp(-x))


def _silu(x):
    return x * _sigmoid(x)


def _softplus(x):
    return jnp.maximum(x, 0.0) + jnp.log(1.0 + jnp.exp(-jnp.abs(x)))


def _rmsnorm(x, g):
    return x * lax.rsqrt(jnp.mean(x * x, axis=-1, keepdims=True) + NORM_EPS) * g


def _split_pairs(x):
    x3 = x.reshape(NCH, CHUNK, BRANCH_W)
    return jnp.concatenate([x3[:, :, j * GRP_W:(j + 1) * GRP_W] for j in range(GROUPS)], axis=0)


def _merge_pairs(y):
    return jnp.concatenate([y[j * NCH:(j + 1) * NCH].reshape(TL, GRP_W) for j in range(GROUPS)], axis=-1)


def _per_chunk(x, c):
    return x.reshape((GROUPS, NCH) + x.shape[1:])[:, c]


def _bd(x):
    x = x.astype(BF16)
    head = lax.broadcasted_iota(jnp.int32, x.shape, 2) // HEAD_DIM
    zero = jnp.zeros_like(x)
    return jnp.concatenate([jnp.where(head == g, x, zero) for g in range(GRP_HEADS)], axis=1)


def _pair_nn(x, y):
    return _bdot(x, _bd(y), 2, 1)


def _head_sum(x, ones_pair):
    return jnp.concatenate([_dot(x[:, j * PAIR_W:(j + 1) * PAIR_W], ones_pair) for j in range(PAIRS)], axis=-1)


def _unit_lower_inverse_minus_eye(low, t_idx, s_idx):
    e = None
    size = 1
    while size < CHUNK:
        sibling = ((t_idx // (2 * size)) == (s_idx // (2 * size))) & ((t_idx // size) != (s_idx // size))
        off = jnp.where(sibling[None], low, 0.0)
        if e is None:
            e = -off
        else:
            a = off + _pair_nn(e, off)
            e = e - a - _pair_nn(a, e)
        size *= 2
    return e


def _rwkv_group(zin_ref, w, s_ref):
    rows = pl.ds(CARRY, TL)
    prev = pl.ds(CARRY - 1, TL)
    mu = w["rwkv_mu"][...]

    def shifted(lo, hi):
        z = zin_ref[rows, lo:hi]
        zp = zin_ref[prev, lo:hi]
        return z + mu[:, lo:hi] * (zp - z)

    r = shifted(0, BRANCH_W)
    k = shifted(BRANCH_W, 2 * BRANCH_W)
    v = shifted(2 * BRANCH_W, 3 * BRANCH_W)
    lora_in = shifted(3 * BRANCH_W, RWKV_SHIFT_W)
    lane = lax.broadcasted_iota(jnp.int32, lora_in.shape, 1)
    lora_in = jnp.where(lane < LORA, jnp.tanh(lora_in), lora_in)
    lora = _dot(lora_in, w["rwkv_lora"][...])
    logd = -math.exp(-0.5) * _sigmoid(w["rwkv_w0"][...] + lora[:, :BRANCH_W])
    a = _sigmoid(w["rwkv_a0"][...] + lora[:, BRANCH_W:])

    ones_pair = w["ones_pair"][...]
    kk = k * w["rwkv_k_k"][...]
    kk = kk * lax.rsqrt(jnp.maximum(_head_sum(kk * kk, ones_pair), 1e-24))
    k = k * (1.0 + (a - 1.0) * w["rwkv_k_a"][...])
    b = kk * a

    g_inc = _dot_hilo(w["chunk_tri"][...], logd)
    e_inc = jnp.exp(g_inc)
    e_neg = jnp.exp(-g_inc)
    kkd = _split_pairs(kk * jnp.exp(g_inc - logd))
    rd = _split_pairs(r * e_inc)
    bi = _split_pairs(b * e_neg)
    ki = _split_pairs(k * e_neg)
    vp = _split_pairs(v)
    gamma_c = _split_pairs(e_inc)[:, CHUNK - 1:CHUNK, :]
    bd = bi * gamma_c
    kd = ki * gamma_c

    t_idx = lax.broadcasted_iota(jnp.int32, (CHUNK, GRP_W), 0)
    s_idx = lax.broadcasted_iota(jnp.int32, (CHUNK, GRP_W), 1) % HEAD_DIM
    strict = (s_idx < t_idx)[None]
    incl = (s_idx <= t_idx)[None]
    row = lax.broadcasted_iota(jnp.int32, (GRP_W, GRP_W), 0)
    col = lax.broadcasted_iota(jnp.int32, (GRP_W, GRP_W), 1)
    same_head = ((row // HEAD_DIM) == (col // HEAD_DIM))[None]
    eye = (row == col)[None]

    lhs = jnp.concatenate([kkd, rd], axis=1)
    a_b = _bdot(lhs, _bd(bi), 2, 2)
    a_k = _bdot(lhs, _bd(ki), 2, 2)
    lb = jnp.where(strict, a_b[:, :CHUNK], 0.0)
    mb = jnp.where(incl, a_b[:, CHUNK:], 0.0)
    lk = jnp.where(strict, a_k[:, :CHUNK], 0.0)
    mk = jnp.where(incl, a_k[:, CHUNK:], 0.0)

    e = _unit_lower_inverse_minus_eye(lb, t_idx, s_idx)
    lkv = _pair_nn(lk, vp)
    ex = _bdot(e, jnp.concatenate([_bd(kkd), _bd(lkv)], axis=2), 2, 1)
    wmat = kkd + ex[:, :, :GRP_W]
    uloc = -(lkv + ex[:, :, GRP_W:])
    pmat = jnp.where(eye, gamma_c, 0.0) - jnp.where(same_head, _bdot(wmat, bd, 1, 1), 0.0)
    qmat = jnp.where(same_head, _bdot(jnp.concatenate([uloc, vp], axis=1),
                                      jnp.concatenate([bd, kd], axis=1), 1, 1), 0.0)
    yloc = _bdot(jnp.concatenate([mb, mk], axis=2), jnp.concatenate([_bd(uloc), _bd(vp)], axis=1), 2, 1)
    reff = rd - _pair_nn(mb, wmat)

    state = s_ref[...]
    ys = []
    for c in range(NCH):
        ys.append(_per_chunk(yloc, c) + _bdot(_per_chunk(reff, c), state, 2, 2))
        state = _bdot(state, _per_chunk(pmat, c), 2, 1) + _per_chunk(qmat, c)
    s_ref[...] = state
    y = _merge_pairs(jnp.stack(ys, axis=1).reshape(GROUPS * NCH, CHUNK, GRP_W))

    inv_n = 1.0 / HEAD_DIM
    mean = _head_sum(y, ones_pair) * inv_n
    yc = y - mean
    var = _head_sum(yc * yc, ones_pair) * inv_n
    yn = yc * lax.rsqrt(var + RWKV_GN_EPS) * w["rwkv_ln_w"][...] + w["rwkv_ln_b"][...]
    bonus = _head_sum(r * k * w["rwkv_r_k"][...], ones_pair) * v
    gate = zin_ref[rows, C_GRW:C_GRW + BRANCH_W]
    return (yn + bonus) * _silu(gate)


def _s5_scan(zin_ref, w, xre_ref, xim_ref, st_re_ref, st_im_ref, start_re_ref, start_im_ref):
    u = zin_ref[pl.ds(CARRY, TL), C_US5:C_US5 + BRANCH_W]
    u_il = jnp.dot(w["perm"][...], u.astype(BF16), preferred_element_type=F32).astype(BF16)
    for blk in range(S5_BLOCKS):
        bu = jnp.dot(u_il[:, blk * S5_BLK_IN:(blk + 1) * S5_BLK_IN], w["s5_b"][blk],
                     preferred_element_type=F32)
        xre_ref[:, blk * S5_BLK_ST:(blk + 1) * S5_BLK_ST] = bu[:, :S5_BLK_ST]
        xim_ref[:, blk * S5_BLK_ST:(blk + 1) * S5_BLK_ST] = bu[:, S5_BLK_ST:]

    for lb in range(S5_N // SCAN_LANES):
        cols = slice(lb * SCAN_LANES, (lb + 1) * SCAN_LANES)
        lam_re = jnp.broadcast_to(w["s5_pow_re"][0:1, cols], (NSEG, SCAN_LANES))
        lam_im = jnp.broadcast_to(w["s5_pow_im"][0:1, cols], (NSEG, SCAN_LANES))

        def step(t, carry, cols=cols, lam_re=lam_re, lam_im=lam_im):
            sr, si = carry
            rws = pl.ds(pl.multiple_of(t * NSEG, NSEG), NSEG)
            nr = lam_re * sr - lam_im * si + xre_ref[rws, cols]
            ni = lam_re * si + lam_im * sr + xim_ref[rws, cols]
            xre_ref[rws, cols] = nr
            xim_ref[rws, cols] = ni
            return nr, ni

        zero = jnp.zeros((NSEG, SCAN_LANES), F32)
        lax.fori_loop(0, SEG, step, (zero, zero), unroll=SCAN_UNROLL)

    last = pl.ds(TL - NSEG, NSEG)
    end_re = xre_ref[last, :]
    end_im = xim_ref[last, :]
    seg_re = w["s5_pow_re"][SEG - 1:SEG, :]
    seg_im = w["s5_pow_im"][SEG - 1:SEG, :]
    cur_re = st_re_ref[...]
    cur_im = st_im_ref[...]
    for i in range(NSEG):
        start_re_ref[i:i + 1, :] = cur_re
        start_im_ref[i:i + 1, :] = cur_im
        nxt_re = end_re[i:i + 1, :] + seg_re * cur_re - seg_im * cur_im
        nxt_im = end_im[i:i + 1, :] + seg_re * cur_im + seg_im * cur_re
        cur_re, cur_im = nxt_re, nxt_im
    st_re_ref[...] = cur_re
    st_im_ref[...] = cur_im


def _s5_output(zin_ref, w, xre_ref, xim_ref, start_re_ref, start_im_ref):
    rows = pl.ds(CARRY, TL)
    u = zin_ref[rows, C_US5:C_US5 + BRANCH_W]
    ys = []
    grp = (TL // BF16_ROWS, BF16_ROWS, S5_BLK_ST)
    twice = BF16_ROWS // NSEG
    for blk in range(S5_BLOCKS):
        cols = slice(blk * S5_BLK_ST, (blk + 1) * S5_BLK_ST)
        s_re = jnp.concatenate([start_re_ref[:, cols]] * twice, axis=0).astype(BF16)[None]
        s_im = jnp.concatenate([start_im_ref[:, cols]] * twice, axis=0).astype(BF16)[None]
        p_re = w["s5_powb_re"][:, cols].reshape(grp)
        p_im = w["s5_powb_im"][:, cols].reshape(grp)
        x_re = xre_ref[:, cols].astype(BF16).reshape(grp) + (p_re * s_re - p_im * s_im)
        x_im = xim_ref[:, cols].astype(BF16).reshape(grp) + (p_re * s_im + p_im * s_re)
        x_ri = jnp.concatenate([x_re.reshape(TL, S5_BLK_ST), x_im.reshape(TL, S5_BLK_ST)], axis=-1)
        ys.append(jnp.dot(x_ri, w["s5_c"][blk], preferred_element_type=F32))
    y_il = jnp.concatenate(ys, axis=-1)
    y = _dot(w["perm_t"][...], y_il) + w["s5_d"][...] * u
    zg = jax.nn.gelu(y)
    zg = zg * _sigmoid(_dot(zg, w["s5_glu_w"][...]) + w["s5_glu_b"][...])
    gate = zin_ref[rows, C_GS5:C_GS5 + BRANCH_W]
    return zg * _silu(gate)


def _lru_scan(zin_ref, w, h_ref, cp_ref, st_ref, start_ref):
    conv_w = w["lru_conv_w"][...]
    xc = w["lru_conv_b"][...] + zin_ref[pl.ds(CARRY, TL), C_XLRU:C_XLRU + BRANCH_W] * conv_w[CONV_WIDTH - 1:CONV_WIDTH, :]
    for back in range(1, CONV_WIDTH):
        tap = CONV_WIDTH - 1 - back
        xc = xc + zin_ref[pl.ds(CARRY - back, TL), C_XLRU:C_XLRU + BRANCH_W] * conv_w[tap:tap + 1, :]
    xc = _dot_hilo(w["perm"][...], xc)
    gates = [_dot(xc[:, j * PAIR_W:(j + 1) * PAIR_W], w["lru_gates"][j]) for j in range(PAIRS)]
    r = _sigmoid(jnp.concatenate([g[:, :PAIR_W] for g in gates], axis=-1) + w["lru_ba"][...])
    i = _sigmoid(jnp.concatenate([g[:, PAIR_W:] for g in gates], axis=-1) + w["lru_bx"][...])
    log_a = (-LRU_C * _softplus(-w["lru_lambda"][...])) * r
    a = jnp.exp(log_a)
    one_minus_a2 = 1.0 - a * a
    mult = jnp.where(one_minus_a2 > 0.0, one_minus_a2 * lax.rsqrt(one_minus_a2), 0.0)
    h_ref[...] = mult * (i * xc)
    cp_ref[...] = a

    def step(t, carry):
        hs, cs = carry
        rws = pl.ds(pl.multiple_of(t * NSEG, NSEG), NSEG)
        a_t = cp_ref[rws, :]
        hs = a_t * hs + h_ref[rws, :]
        cs = a_t * cs
        h_ref[rws, :] = hs
        cp_ref[rws, :] = cs
        return hs, cs

    lax.fori_loop(0, SEG, step, (jnp.zeros((NSEG, BRANCH_W), F32), jnp.ones((NSEG, BRANCH_W), F32)),
                  unroll=SCAN_UNROLL)

    last = pl.ds(TL - NSEG, NSEG)
    end_h = h_ref[last, :]
    end_cp = cp_ref[last, :]
    cur = st_ref[...]
    for s in range(NSEG):
        start_ref[s:s + 1, :] = cur
        cur = end_h[s:s + 1, :] + end_cp[s:s + 1, :] * cur
    st_ref[...] = cur


def _lru_output(zin_ref, w, h_ref, cp_ref, start_ref):
    start = start_ref[...]
    h3 = h_ref[...].reshape(SEG, NSEG, BRANCH_W) + cp_ref[...].reshape(SEG, NSEG, BRANCH_W) * start[None]
    h = _dot(w["perm_t"][...], h3.reshape(TL, BRANCH_W))
    gate = zin_ref[pl.ds(CARRY, TL), C_GLRU:C_GLRU + BRANCH_W]
    return h * _silu(gate)


class _Rows:
    def __init__(self, ref, start, n=1):
        self.ref, self.start, self.n = ref, start, n

    def __getitem__(self, idx):
        assert idx is Ellipsis
        return self.ref[self.start:self.start + self.n, :]


def _layer_body(names, is_last, *refs):
    n_in = len(names)
    w = dict(zip(names, refs[:n_in]))
    for table, (members, _) in VEC_TABLES.items():
        start = 0
        for name, n_rows in members:
            w[name] = _Rows(w[table], start, n_rows)
            start += n_rows
    out_ref = refs[n_in]
    (zin_ref, s_ref, xre_ref, xim_ref, s5_re_ref, s5_im_ref, s5_start_re_ref, s5_start_im_ref,
     lru_h_ref, lru_cp_ref, lru_st_ref, lru_start_ref) = refs[n_in + 1:]

    @pl.when(pl.program_id(1) == 0)
    def _():
        zin_ref[0:CARRY, :] = jnp.zeros((CARRY, D_IN), F32)
        s_ref[...] = jnp.zeros_like(s_ref)
        s5_re_ref[...] = jnp.zeros_like(s5_re_ref)
        s5_im_ref[...] = jnp.zeros_like(s5_im_ref)
        lru_st_ref[...] = jnp.zeros_like(lru_st_ref)

    h = w["h"][0]
    xn = _rmsnorm(h, w["norm_g"][...]).astype(BF16)
    for c0, c1 in IN_COL_BLKS:
        zin_ref[pl.ds(CARRY, TL), c0:c1] = jnp.dot(xn, w["w_in"][:, c0:c1], preferred_element_type=F32)

    y_rw = _rwkv_group(zin_ref, w, s_ref)
    _s5_scan(zin_ref, w, xre_ref, xim_ref, s5_re_ref, s5_im_ref, s5_start_re_ref, s5_start_im_ref)
    y_s5 = _s5_output(zin_ref, w, xre_ref, xim_ref, s5_start_re_ref, s5_start_im_ref)
    _lru_scan(zin_ref, w, lru_h_ref, lru_cp_ref, lru_st_ref, lru_start_ref)
    y_lru = _lru_output(zin_ref, w, lru_h_ref, lru_cp_ref, lru_start_ref)
    zin_ref[0:CARRY, :] = zin_ref[TL:TL + CARRY, :]

    h = h + (_dot(y_rw, w["w_out"][0:BRANCH_W, :])
             + _dot(y_s5, w["w_out"][BRANCH_W:2 * BRANCH_W, :])
             + _dot(y_lru, w["w_out"][2 * BRANCH_W:3 * BRANCH_W, :]))
    e = _rmsnorm(_dot(w["p"][0], w["ple_w"][...]), w["ple_norm_g"][...])
    h = h + e * _sigmoid(_dot(h, w["ple_gate_w"][...]))
    if is_last:
        h = _rmsnorm(h, w["final_norm_g"][...])
    out_ref[0] = h


def _block_diag(blocks):
    n, a, b = blocks.shape[-3:]
    eye = jnp.eye(n, dtype=blocks.dtype)
    return jnp.einsum("...nab,nm->...namb", blocks, eye).reshape(blocks.shape[:-3] + (n * a, n * b))


def _constants():
    nat = np.arange(TL)
    interleaved_row = (nat % SEG) * NSEG + nat // SEG
    perm = np.zeros((TL, TL), np.float32)
    perm[interleaved_row, nat] = 1.0
    t = nat[:, None]
    s = nat[None, :]
    tri = ((t // CHUNK) == (s // CHUNK)) & (s <= t)
    lane = np.arange(PAIR_W)
    ones_pair = (lane[:, None] // HEAD_DIM) == (lane[None, :] // HEAD_DIM)
    as_bf16 = lambda m: jnp.asarray(m.astype(np.float32), dtype=BF16)
    return as_bf16(perm), as_bf16(perm.T), as_bf16(tri), as_bf16(ones_pair)


def _s5_params(a_re, a_im, log_dt, b_re, b_im, c_re, c_im):
    depth = a_re.shape[0]
    a_re, a_im = a_re.astype(F32), a_im.astype(F32)
    dt = jnp.exp(log_dt.astype(F32))[..., None]
    mag = jnp.exp(a_re * dt)
    lr, li = mag * jnp.cos(a_im * dt), mag * jnp.sin(a_im * dt)
    den = a_re * a_re + a_im * a_im
    f_re = ((lr - 1.0) * a_re + li * a_im) / den
    f_im = (li * a_re - (lr - 1.0) * a_im) / den
    b_re, b_im = b_re.astype(F32), b_im.astype(F32)
    bb_re = f_re[..., None] * b_re - f_im[..., None] * b_im
    bb_im = f_re[..., None] * b_im + f_im[..., None] * b_re
    steps = jnp.arange(1, SEG + 1, dtype=F32)[:, None, None]
    pmag = jnp.exp((a_re * dt)[:, None] * steps)
    ang = (a_im * dt)[:, None] * steps
    pow_re = (pmag * jnp.cos(ang)).reshape(depth, SEG, S5_N)
    pow_im = (pmag * jnp.sin(ang)).reshape(depth, SEG, S5_N)
    gpb = S5_GROUPS // S5_BLOCKS

    def blocks(x, rows, cols):
        return _block_diag(x.reshape(depth, S5_BLOCKS, gpb, rows, cols))

    s5_b = jnp.concatenate([blocks(jnp.swapaxes(bb_re, 2, 3), S5_GROUP, S5_STATE),
                            blocks(jnp.swapaxes(bb_im, 2, 3), S5_GROUP, S5_STATE)], axis=-1).astype(BF16)
    s5_c = jnp.concatenate([blocks(jnp.swapaxes(c_re.astype(F32), 2, 3), S5_STATE, S5_GROUP),
                            -blocks(jnp.swapaxes(c_im.astype(F32), 2, 3), S5_STATE, S5_GROUP)], axis=-2).astype(BF16)
    return s5_b, s5_c, pow_re, pow_im


def _layer_call(h, p, layer, stacked, shared, is_last):
    bsz, seq, _ = h.shape
    assert seq % TL == 0
    names = ["h", "p"] + list(stacked.keys()) + list(shared.keys())
    arrays = [h, p] + list(stacked.values()) + list(shared.values())
    in_specs = [
        pl.BlockSpec((1, TL, D_MODEL), lambda b, j: (b, j, 0)),
        pl.BlockSpec((None, 1, TL, D_PLE), lambda b, j: (layer, b, j, 0)),
    ]
    for arr in stacked.values():
        index = (layer,) + (0,) * (arr.ndim - 1)
        in_specs.append(pl.BlockSpec((None,) + arr.shape[1:], lambda b, j, index=index: index,
                                     pipeline_mode=pl.Buffered(1)))
    for arr in shared.values():
        index = (0,) * arr.ndim
        in_specs.append(pl.BlockSpec(arr.shape, lambda b, j, index=index: index,
                                     pipeline_mode=pl.Buffered(1)))
    scratch = [
        pltpu.VMEM((CARRY + TL, D_IN), F32),
        pltpu.VMEM((GROUPS, GRP_W, GRP_W), F32),
        pltpu.VMEM((TL, S5_N), F32),
        pltpu.VMEM((TL, S5_N), F32),
        pltpu.VMEM((1, S5_N), F32),
        pltpu.VMEM((1, S5_N), F32),
        pltpu.VMEM((NSEG, S5_N), F32),
        pltpu.VMEM((NSEG, S5_N), F32),
        pltpu.VMEM((TL, BRANCH_W), F32),
        pltpu.VMEM((TL, BRANCH_W), F32),
        pltpu.VMEM((1, BRANCH_W), F32),
        pltpu.VMEM((NSEG, BRANCH_W), F32),
    ]
    return pl.pallas_call(
        functools.partial(_layer_body, tuple(names), is_last),
        grid=(bsz, seq // TL),
        in_specs=in_specs,
        out_specs=pl.BlockSpec((1, TL, D_MODEL), lambda b, j: (b, j, 0)),
        out_shape=jax.ShapeDtypeStruct(h.shape, F32),
        scratch_shapes=scratch,
        compiler_params=pltpu.CompilerParams(
            dimension_semantics=("arbitrary", "arbitrary"),
            vmem_limit_bytes=VMEM_LIMIT),
        name="hybrid_layer",
    )(*arrays)


def kernel(x, p, norm_g, w_in, rwkv_mu, rwkv_w0, rwkv_w2, rwkv_a0, rwkv_a2, rwkv_k_k, rwkv_k_a, rwkv_r_k, rwkv_ln_w, rwkv_ln_b, s5_a_re, s5_a_im, s5_log_dt, s5_b_re, s5_b_im, s5_c_re, s5_c_im, s5_d, s5_glu_w, s5_glu_b, lru_conv_w, lru_conv_b, lru_wa, lru_ba, lru_wx, lru_bx, lru_lambda, w_out, ple_w, ple_norm_g, ple_gate_w, final_norm_g):
    depth = w_in.shape[0]
    perm, perm_t, tri, ones_pair = _constants()
    shared = {"ones_pair": ones_pair, "chunk_tri": tri, "perm": perm, "perm_t": perm_t}

    s5_b, s5_c, pow_re, pow_im = _s5_params(s5_a_re, s5_a_im, s5_log_dt, s5_b_re, s5_b_im, s5_c_re, s5_c_im)
    zero = jnp.zeros((depth, LORA, BRANCH_W), F32)
    lora = jnp.concatenate([jnp.concatenate([rwkv_w2, zero], axis=2),
                            jnp.concatenate([zero, rwkv_a2], axis=2)], axis=1)
    pair_blocks = lambda m: _block_diag(m.reshape(depth, PAIRS, 2, LRU_BLOCK_DIM, LRU_BLOCK_DIM))
    vectors = {
        "rwkv_w0": rwkv_w0, "rwkv_a0": rwkv_a0, "rwkv_k_k": rwkv_k_k, "rwkv_k_a": rwkv_k_a,
        "rwkv_r_k": rwkv_r_k.reshape(depth, BRANCH_W), "rwkv_ln_w": rwkv_ln_w, "rwkv_ln_b": rwkv_ln_b,
        "s5_d": s5_d, "s5_glu_b": s5_glu_b, "lru_conv_b": lru_conv_b, "lru_ba": lru_ba, "lru_bx": lru_bx,
        "lru_lambda": lru_lambda, "lru_conv_w": lru_conv_w,
        "norm_g": norm_g, "ple_norm_g": ple_norm_g,
        "final_norm_g": jnp.broadcast_to(final_norm_g, (depth, D_MODEL)),
    }
    stacked = {
        table: jnp.concatenate([vectors[name].astype(F32).reshape(depth, rows, width) for name, rows in members], axis=1)
        for table, (members, width) in VEC_TABLES.items()
    }
    stacked.update({
        "rwkv_mu": rwkv_mu.astype(F32)[:, None, :],
        "w_in": w_in.astype(BF16),
        "rwkv_lora": lora.astype(BF16),
        "s5_b": s5_b,
        "s5_c": s5_c,
        "s5_pow_re": pow_re,
        "s5_pow_im": pow_im,
        "s5_powb_re": jnp.repeat(pow_re, NSEG, axis=1).astype(BF16),
        "s5_powb_im": jnp.repeat(pow_im, NSEG, axis=1).astype(BF16),
        "s5_glu_w": s5_glu_w.astype(BF16),
        "lru_gates": jnp.concatenate([pair_blocks(lru_wa), pair_blocks(lru_wx)], axis=-1).astype(BF16),
        "w_out": w_out.astype(BF16),
        "ple_w": ple_w.astype(BF16),
        "ple_gate_w": ple_gate_w.astype(BF16),
    })
    h = x.astype(F32)
    p = p.astype(F32)
    for i in range(depth):
        h = _layer_call(h, p, i, stacked, shared, is_last=(i == depth - 1))
    return h.astype(x.dtype)
```

```python
import functools
import math

import jax
import jax.numpy as jnp
import numpy as np
from jax import lax
from jax.experimental import pallas as pl
from jax.experimental.pallas import tpu as pltpu

F32 = jnp.float32
BF16 = jnp.bfloat16

D_MODEL = 1024
D_PLE = 256
BRANCH_W = 512
RWKV_HEADS = 8
HEAD_DIM = 64
PAIR_W = 2 * HEAD_DIM
PAIRS = RWKV_HEADS // 2
GRP_HEADS = 2
GRP_W = GRP_HEADS * HEAD_DIM
GROUPS = RWKV_HEADS // GRP_HEADS
LORA = 64
RWKV_GN_EPS = 64e-5
S5_GROUPS = 32
S5_GROUP = 16
S5_STATE = 64
S5_N = S5_GROUPS * S5_STATE
S5_BLOCKS = 4
S5_BLK_IN = BRANCH_W // S5_BLOCKS
S5_BLK_ST = S5_N // S5_BLOCKS
LRU_BLOCKS = 8
LRU_BLOCK_DIM = 64
CONV_WIDTH = 4
LRU_C = 8.0
NORM_EPS = 1e-6
RWKV_SHIFT_W = 3 * BRANCH_W + 2 * LORA
C_GRW = RWKV_SHIFT_W
C_US5 = C_GRW + BRANCH_W
C_GS5 = C_US5 + BRANCH_W
C_XLRU = C_GS5 + BRANCH_W
C_GLRU = C_XLRU + BRANCH_W
D_IN = C_GLRU + BRANCH_W

NB = 4
CHUNK = 64
TL = NB * CHUNK
NCH = NB
NSEG = 8
SEG = TL // NSEG
SEGS_PER_SEQ = NSEG // NB
CARRY = 8
ZB = CARRY + CHUNK
BF16_ROWS = 16
IN_COL_BLKS = ((0, 1280), (1280, 2560), (2560, D_IN))
SCAN_LANES = 512
SCAN_UNROLL = 4
VMEM_LIMIT = 56 * 1024 * 1024

VEC_TABLES = {
    "vec512": ((("rwkv_w0", 1), ("rwkv_a0", 1), ("rwkv_k_k", 1), ("rwkv_k_a", 1), ("rwkv_r_k", 1),
                ("rwkv_ln_w", 1), ("rwkv_ln_b", 1), ("s5_d", 1), ("s5_glu_b", 1), ("lru_conv_b", 1),
                ("lru_ba", 1), ("lru_bx", 1), ("lru_lambda", 1), ("lru_conv_w", CONV_WIDTH)), BRANCH_W),
    "vec1024": ((("norm_g", 1), ("ple_norm_g", 1), ("final_norm_g", 1)), D_MODEL),
}


def _dot(a, b):
    return jnp.dot(a.astype(BF16), b.astype(BF16), preferred_element_type=F32)


def _bdot(a, b, lhs_c, rhs_c):
    dims = (((lhs_c,), (rhs_c,)), ((0,), (0,)))
    return lax.dot_general(a.astype(BF16), b.astype(BF16), dims, preferred_element_type=F32)


def _hi_lo(x):
    hi = x.astype(BF16)
    lo = (x - hi.astype(F32)).astype(BF16)
    return hi, lo


def _dot_hilo(m, x):
    hi, lo = _hi_lo(x)
    return (jnp.dot(m, hi, preferred_element_type=F32)
            + jnp.dot(m, lo, preferred_element_type=F32))


def _sigmoid(x):
    return 1.0 / (1.0 + jnp.exp(-x))


def _silu(x):
    return x * _sigmoid(x)


def _softplus(x):
    return jnp.maximum(x, 0.0) + jnp.log(1.0 + jnp.exp(-jnp.abs(x)))


def _rmsnorm(x, g):
    return x * lax.rsqrt(jnp.mean(x * x, axis=-1, keepdims=True) + NORM_EPS) * g


def _split_pairs(x):
    x3 = x.reshape(NCH, CHUNK, BRANCH_W)
    return jnp.concatenate([x3[:, :, j * GRP_W:(j + 1) * GRP_W] for j in range(GROUPS)], axis=0)


def _merge_pairs(y):
    return jnp.concatenate([y[j * NCH:(j + 1) * NCH].reshape(TL, GRP_W) for j in range(GROUPS)], axis=-1)


def _bd(x):
    x = x.astype(BF16)
    head = lax.broadcasted_iota(jnp.int32, x.shape, 2) // HEAD_DIM
    zero = jnp.zeros_like(x)
    return jnp.concatenate([jnp.where(head == g, x, zero) for g in range(GRP_HEADS)], axis=1)


def _pair_nn(x, y):
    return _bdot(x, _bd(y), 2, 1)


def _head_sum(x, ones_pair):
    return jnp.concatenate([_dot(x[:, j * PAIR_W:(j + 1) * PAIR_W], ones_pair) for j in range(PAIRS)], axis=-1)


def _unit_lower_inverse_minus_eye(low, t_idx, s_idx):
    e = None
    size = 1
    while size < CHUNK:
        sibling = ((t_idx // (2 * size)) == (s_idx // (2 * size))) & ((t_idx // size) != (s_idx // size))
        off = jnp.where(sibling[None], low, 0.0)
        if e is None:
            e = -off
        else:
            a = off + _pair_nn(e, off)
            e = e - a - _pair_nn(a, e)
        size *= 2
    return e


def _zrows(zin_ref, lo, hi, back=0):
    return jnp.concatenate([zin_ref[pl.ds(b * ZB + CARRY - back, CHUNK), lo:hi] for b in range(NB)], axis=0)


def _rwkv_group(zin_ref, w, s_ref):
    mu = w["rwkv_mu"][...]

    def shifted(lo, hi):
        z = _zrows(zin_ref, lo, hi)
        zp = _zrows(zin_ref, lo, hi, back=1)
        return z + mu[:, lo:hi] * (zp - z)

    r = shifted(0, BRANCH_W)
    k = shifted(BRANCH_W, 2 * BRANCH_W)
    v = shifted(2 * BRANCH_W, 3 * BRANCH_W)
    lora_in = shifted(3 * BRANCH_W, RWKV_SHIFT_W)
    lane = lax.broadcasted_iota(jnp.int32, lora_in.shape, 1)
    lora_in = jnp.where(lane < LORA, jnp.tanh(lora_in), lora_in)
    lora = _dot(lora_in, w["rwkv_lora"][...])
    logd = -math.exp(-0.5) * _sigmoid(w["rwkv_w0"][...] + lora[:, :BRANCH_W])
    a = _sigmoid(w["rwkv_a0"][...] + lora[:, BRANCH_W:])

    ones_pair = w["ones_pair"][...]
    kk = k * w["rwkv_k_k"][...]
    kk = kk * lax.rsqrt(jnp.maximum(_head_sum(kk * kk, ones_pair), 1e-24))
    k = k * (1.0 + (a - 1.0) * w["rwkv_k_a"][...])
    b = kk * a

    g_inc = _dot_hilo(w["chunk_tri"][...], logd)
    e_inc = jnp.exp(g_inc)
    e_neg = jnp.exp(-g_inc)
    kkd = _split_pairs(kk * jnp.exp(g_inc - logd))
    rd = _split_pairs(r * e_inc)
    bi = _split_pairs(b * e_neg)
    ki = _split_pairs(k * e_neg)
    vp = _split_pairs(v)
    gamma_c = _split_pairs(e_inc)[:, CHUNK - 1:CHUNK, :]
    bd = bi * gamma_c
    kd = ki * gamma_c

    t_idx = lax.broadcasted_iota(jnp.int32, (CHUNK, GRP_W), 0)
    s_idx = lax.broadcasted_iota(jnp.int32, (CHUNK, GRP_W), 1) % HEAD_DIM
    strict = (s_idx < t_idx)[None]
    incl = (s_idx <= t_idx)[None]
    row = lax.broadcasted_iota(jnp.int32, (GRP_W, GRP_W), 0)
    col = lax.broadcasted_iota(jnp.int32, (GRP_W, GRP_W), 1)
    same_head = ((row // HEAD_DIM) == (col // HEAD_DIM))[None]
    eye = (row == col)[None]

    lhs = jnp.concatenate([kkd, rd], axis=1)
    a_b = _bdot(lhs, _bd(bi), 2, 2)
    a_k = _bdot(lhs, _bd(ki), 2, 2)
    lb = jnp.where(strict, a_b[:, :CHUNK], 0.0)
    mb = jnp.where(incl, a_b[:, CHUNK:], 0.0)
    lk = jnp.where(strict, a_k[:, :CHUNK], 0.0)
    mk = jnp.where(incl, a_k[:, CHUNK:], 0.0)

    e = _unit_lower_inverse_minus_eye(lb, t_idx, s_idx)
    lkv = _pair_nn(lk, vp)
    ex = _bdot(e, jnp.concatenate([_bd(kkd), _bd(lkv)], axis=2), 2, 1)
    wmat = kkd + ex[:, :, :GRP_W]
    uloc = -(lkv + ex[:, :, GRP_W:])
    pmat = jnp.where(eye, gamma_c, 0.0) - jnp.where(same_head, _bdot(wmat, bd, 1, 1), 0.0)
    qmat = jnp.where(same_head, _bdot(jnp.concatenate([uloc, vp], axis=1),
                                      jnp.concatenate([bd, kd], axis=1), 1, 1), 0.0)
    yloc = _bdot(jnp.concatenate([mb, mk], axis=2), jnp.concatenate([_bd(uloc), _bd(vp)], axis=1), 2, 1)
    reff = rd - _pair_nn(mb, wmat)

    state = s_ref[...]
    y = _merge_pairs(yloc + _bdot(reff, state, 2, 2))
    s_ref[...] = _bdot(state, pmat, 2, 1) + qmat

    inv_n = 1.0 / HEAD_DIM
    mean = _head_sum(y, ones_pair) * inv_n
    yc = y - mean
    var = _head_sum(yc * yc, ones_pair) * inv_n
    yn = yc * lax.rsqrt(var + RWKV_GN_EPS) * w["rwkv_ln_w"][...] + w["rwkv_ln_b"][...]
    bonus = _head_sum(r * k * w["rwkv_r_k"][...], ones_pair) * v
    gate = _zrows(zin_ref, C_GRW, C_GRW + BRANCH_W)
    return (yn + bonus) * _silu(gate)


def _first_segment_rows(width):
    assert SEGS_PER_SEQ == 2
    return lax.broadcasted_iota(jnp.int32, (NSEG, width), 0) % SEGS_PER_SEQ == 0


def _s5_scan(zin_ref, w, xre_ref, xim_ref, st_re_ref, st_im_ref, start_re_ref, start_im_ref):
    u = _zrows(zin_ref, C_US5, C_US5 + BRANCH_W)
    u_il = jnp.dot(w["perm"][...], u.astype(BF16), preferred_element_type=F32).astype(BF16)
    for blk in range(S5_BLOCKS):
        bu = jnp.dot(u_il[:, blk * S5_BLK_IN:(blk + 1) * S5_BLK_IN], w["s5_b"][blk],
                     preferred_element_type=F32)
        xre_ref[:, blk * S5_BLK_ST:(blk + 1) * S5_BLK_ST] = bu[:, :S5_BLK_ST]
        xim_ref[:, blk * S5_BLK_ST:(blk + 1) * S5_BLK_ST] = bu[:, S5_BLK_ST:]

    for lb in range(S5_N // SCAN_LANES):
        cols = slice(lb * SCAN_LANES, (lb + 1) * SCAN_LANES)
        lam_re = jnp.broadcast_to(w["s5_pow_re"][0:1, cols], (NSEG, SCAN_LANES))
        lam_im = jnp.broadcast_to(w["s5_pow_im"][0:1, cols], (NSEG, SCAN_LANES))

        def step(t, carry, cols=cols, lam_re=lam_re, lam_im=lam_im):
            sr, si = carry
            rws = pl.ds(pl.multiple_of(t * NSEG, NSEG), NSEG)
            nr = lam_re * sr - lam_im * si + xre_ref[rws, cols]
            ni = lam_re * si + lam_im * sr + xim_ref[rws, cols]
            xre_ref[rws, cols] = nr
            xim_ref[rws, cols] = ni
            return nr, ni

        zero = jnp.zeros((NSEG, SCAN_LANES), F32)
        lax.fori_loop(0, SEG, step, (zero, zero), unroll=SCAN_UNROLL)

    last = pl.ds(TL - NSEG, NSEG)
    end_re = xre_ref[last, :]
    end_im = xim_ref[last, :]
    seg_re = w["s5_pow_re"][SEG - 1:SEG, :]
    seg_im = w["s5_pow_im"][SEG - 1:SEG, :]
    first = _first_segment_rows(S5_N)
    c_re = st_re_ref[...]
    c_im = st_im_ref[...]
    e0_re = end_re + seg_re * c_re - seg_im * c_im
    e0_im = end_im + seg_re * c_im + seg_im * c_re
    start_re = jnp.where(first, c_re, pltpu.roll(e0_re, 1, axis=0))
    start_im = jnp.where(first, c_im, pltpu.roll(e0_im, 1, axis=0))
    start_re_ref[...] = start_re
    start_im_ref[...] = start_im
    e1_re = end_re + seg_re * start_re - seg_im * start_im
    e1_im = end_im + seg_re * start_im + seg_im * start_re
    st_re_ref[...] = jnp.where(first, pltpu.roll(e1_re, NSEG - 1, axis=0), 0.0)
    st_im_ref[...] = jnp.where(first, pltpu.roll(e1_im, NSEG - 1, axis=0), 0.0)


def _s5_output(zin_ref, w, xre_ref, xim_ref, start_re_ref, start_im_ref):
    u = _zrows(zin_ref, C_US5, C_US5 + BRANCH_W)
    ys = []
    grp = (TL // BF16_ROWS, BF16_ROWS, S5_BLK_ST)
    twice = BF16_ROWS // NSEG
    for blk in range(S5_BLOCKS):
        cols = slice(blk * S5_BLK_ST, (blk + 1) * S5_BLK_ST)
        s_re = jnp.concatenate([start_re_ref[:, cols]] * twice, axis=0).astype(BF16)[None]
        s_im = jnp.concatenate([start_im_ref[:, cols]] * twice, axis=0).astype(BF16)[None]
        p_re = w["s5_powb_re"][:, cols].reshape(grp)
        p_im = w["s5_powb_im"][:, cols].reshape(grp)
        x_re = xre_ref[:, cols].astype(BF16).reshape(grp) + (p_re * s_re - p_im * s_im)
        x_im = xim_ref[:, cols].astype(BF16).reshape(grp) + (p_re * s_im + p_im * s_re)
        x_ri = jnp.concatenate([x_re.reshape(TL, S5_BLK_ST), x_im.reshape(TL, S5_BLK_ST)], axis=-1)
        ys.append(jnp.dot(x_ri, w["s5_c"][blk], preferred_element_type=F32))
    y_il = jnp.concatenate(ys, axis=-1)
    y = _dot(w["perm_t"][...], y_il) + w["s5_d"][...] * u
    zg = jax.nn.gelu(y)
    zg = zg * _sigmoid(_dot(zg, w["s5_glu_w"][...]) + w["s5_glu_b"][...])
    gate = _zrows(zin_ref, C_GS5, C_GS5 + BRANCH_W)
    return zg * _silu(gate)


def _lru_scan(zin_ref, w, h_ref, cp_ref, st_ref, start_ref):
    conv_w = w["lru_conv_w"][...]
    xc = w["lru_conv_b"][...]
    for back in range(CONV_WIDTH):
        tap = CONV_WIDTH - 1 - back
        xc = xc + _zrows(zin_ref, C_XLRU, C_XLRU + BRANCH_W, back) * conv_w[tap:tap + 1, :]
    xc = _dot_hilo(w["perm"][...], xc)
    gates = [_dot(xc[:, j * PAIR_W:(j + 1) * PAIR_W], w["lru_gates"][j]) for j in range(PAIRS)]
    r = _sigmoid(jnp.concatenate([g[:, :PAIR_W] for g in gates], axis=-1) + w["lru_ba"][...])
    i = _sigmoid(jnp.concatenate([g[:, PAIR_W:] for g in gates], axis=-1) + w["lru_bx"][...])
    log_a = (-LRU_C * _softplus(-w["lru_lambda"][...])) * r
    a = jnp.exp(log_a)
    one_minus_a2 = 1.0 - a * a
    mult = jnp.where(one_minus_a2 > 0.0, one_minus_a2 * lax.rsqrt(one_minus_a2), 0.0)
    h_ref[...] = mult * (i * xc)
    cp_ref[...] = a

    def step(t, carry):
        hs, cs = carry
        rws = pl.ds(pl.multiple_of(t * NSEG, NSEG), NSEG)
        a_t = cp_ref[rws, :]
        hs = a_t * hs + h_ref[rws, :]
        cs = a_t * cs
        h_ref[rws, :] = hs
        cp_ref[rws, :] = cs
        return hs, cs

    lax.fori_loop(0, SEG, step, (jnp.zeros((NSEG, BRANCH_W), F32), jnp.ones((NSEG, BRANCH_W), F32)),
                  unroll=SCAN_UNROLL)

    last = pl.ds(TL - NSEG, NSEG)
    end_h = h_ref[last, :]
    end_cp = cp_ref[last, :]
    first = _first_segment_rows(BRANCH_W)
    carried = st_ref[...]
    start = jnp.where(first, carried, pltpu.roll(end_h + end_cp * carried, 1, axis=0))
    start_ref[...] = start
    st_ref[...] = jnp.where(first, pltpu.roll(end_h + end_cp * start, NSEG - 1, axis=0), 0.0)


def _lru_output(zin_ref, w, h_ref, cp_ref, start_ref):
    start = start_ref[...]
    h3 = h_ref[...].reshape(SEG, NSEG, BRANCH_W) + cp_ref[...].reshape(SEG, NSEG, BRANCH_W) * start[None]
    h = _dot(w["perm_t"][...], h3.reshape(TL, BRANCH_W))
    gate = _zrows(zin_ref, C_GLRU, C_GLRU + BRANCH_W)
    return h * _silu(gate)


class _Rows:
    def __init__(self, ref, start, n=1):
        self.ref, self.start, self.n = ref, start, n

    def __getitem__(self, idx):
        assert idx is Ellipsis
        return self.ref[self.start:self.start + self.n, :]


def _layer_body(names, is_last, *refs):
    n_in = len(names)
    w = dict(zip(names, refs[:n_in]))
    for table, (members, _) in VEC_TABLES.items():
        start = 0
        for name, n_rows in members:
            w[name] = _Rows(w[table], start, n_rows)
            start += n_rows
    out_ref = refs[n_in]
    (zin_ref, s_ref, xre_ref, xim_ref, s5_re_ref, s5_im_ref, s5_start_re_ref, s5_start_im_ref,
     lru_h_ref, lru_cp_ref, lru_st_ref, lru_start_ref) = refs[n_in + 1:]

    @pl.when(pl.program_id(0) == 0)
    def _():
        for b in range(NB):
            zin_ref[b * ZB:b * ZB + CARRY, :] = jnp.zeros((CARRY, D_IN), F32)
        s_ref[...] = jnp.zeros_like(s_ref)
        s5_re_ref[...] = jnp.zeros_like(s5_re_ref)
        s5_im_ref[...] = jnp.zeros_like(s5_im_ref)
        lru_st_ref[...] = jnp.zeros_like(lru_st_ref)

    h = w["h"][...].reshape(TL, D_MODEL)
    xn = _rmsnorm(h, w["norm_g"][...]).astype(BF16)
    for c0, c1 in IN_COL_BLKS:
        z = jnp.dot(xn, w["w_in"][:, c0:c1], preferred_element_type=F32)
        for b in range(NB):
            zin_ref[pl.ds(b * ZB + CARRY, CHUNK), c0:c1] = z[b * CHUNK:(b + 1) * CHUNK]

    y_rw = _rwkv_group(zin_ref, w, s_ref)
    _s5_scan(zin_ref, w, xre_ref, xim_ref, s5_re_ref, s5_im_ref, s5_start_re_ref, s5_start_im_ref)
    y_s5 = _s5_output(zin_ref, w, xre_ref, xim_ref, s5_start_re_ref, s5_start_im_ref)
    _lru_scan(zin_ref, w, lru_h_ref, lru_cp_ref, lru_st_ref, lru_start_ref)
    y_lru = _lru_output(zin_ref, w, lru_h_ref, lru_cp_ref, lru_start_ref)
    for b in range(NB):
        zin_ref[b * ZB:b * ZB + CARRY, :] = zin_ref[b * ZB + CHUNK:(b + 1) * ZB, :]

    h = h + (_dot(y_rw, w["w_out"][0:BRANCH_W, :])
             + _dot(y_s5, w["w_out"][BRANCH_W:2 * BRANCH_W, :])
             + _dot(y_lru, w["w_out"][2 * BRANCH_W:3 * BRANCH_W, :]))
    e = _rmsnorm(_dot(w["p"][...].reshape(TL, D_PLE), w["ple_w"][...]), w["ple_norm_g"][...])
    h = h + e * _sigmoid(_dot(h, w["ple_gate_w"][...]))
    if is_last:
        h = _rmsnorm(h, w["final_norm_g"][...])
    out_ref[...] = h.reshape(NB, CHUNK, D_MODEL)


def _block_diag(blocks):
    n, a, b = blocks.shape[-3:]
    eye = jnp.eye(n, dtype=blocks.dtype)
    return jnp.einsum("...nab,nm->...namb", blocks, eye).reshape(blocks.shape[:-3] + (n * a, n * b))


def _constants():
    nat = np.arange(TL)
    interleaved_row = (nat % SEG) * NSEG + nat // SEG
    perm = np.zeros((TL, TL), np.float32)
    perm[interleaved_row, nat] = 1.0
    t = nat[:, None]
    s = nat[None, :]
    tri = ((t // CHUNK) == (s // CHUNK)) & (s <= t)
    lane = np.arange(PAIR_W)
    ones_pair = (lane[:, None] // HEAD_DIM) == (lane[None, :] // HEAD_DIM)
    as_bf16 = lambda m: jnp.asarray(m.astype(np.float32), dtype=BF16)
    return as_bf16(perm), as_bf16(perm.T), as_bf16(tri), as_bf16(ones_pair)


def _s5_params(a_re, a_im, log_dt, b_re, b_im, c_re, c_im):
    depth = a_re.shape[0]
    a_re, a_im = a_re.astype(F32), a_im.astype(F32)
    dt = jnp.exp(log_dt.astype(F32))[..., None]
    mag = jnp.exp(a_re * dt)
    lr, li = mag * jnp.cos(a_im * dt), mag * jnp.sin(a_im * dt)
    den = a_re * a_re + a_im * a_im
    f_re = ((lr - 1.0) * a_re + li * a_im) / den
    f_im = (li * a_re - (lr - 1.0) * a_im) / den
    b_re, b_im = b_re.astype(F32), b_im.astype(F32)
    bb_re = f_re[..., None] * b_re - f_im[..., None] * b_im
    bb_im = f_re[..., None] * b_im + f_im[..., None] * b_re
    steps = jnp.arange(1, SEG + 1, dtype=F32)[:, None, None]
    pmag = jnp.exp((a_re * dt)[:, None] * steps)
    ang = (a_im * dt)[:, None] * steps
    pow_re = (pmag * jnp.cos(ang)).reshape(depth, SEG, S5_N)
    pow_im = (pmag * jnp.sin(ang)).reshape(depth, SEG, S5_N)
    gpb = S5_GROUPS // S5_BLOCKS

    def blocks(x, rows, cols):
        return _block_diag(x.reshape(depth, S5_BLOCKS, gpb, rows, cols))

    s5_b = jnp.concatenate([blocks(jnp.swapaxes(bb_re, 2, 3), S5_GROUP, S5_STATE),
                            blocks(jnp.swapaxes(bb_im, 2, 3), S5_GROUP, S5_STATE)], axis=-1).astype(BF16)
    s5_c = jnp.concatenate([blocks(jnp.swapaxes(c_re.astype(F32), 2, 3), S5_STATE, S5_GROUP),
                            -blocks(jnp.swapaxes(c_im.astype(F32), 2, 3), S5_STATE, S5_GROUP)], axis=-2).astype(BF16)
    return s5_b, s5_c, pow_re, pow_im


def _layer_call(h, p, layer, stacked, shared, is_last):
    bsz, seq, _ = h.shape
    assert bsz == NB and seq % CHUNK == 0
    names = ["h", "p"] + list(stacked.keys()) + list(shared.keys())
    arrays = [h, p] + list(stacked.values()) + list(shared.values())
    in_specs = [
        pl.BlockSpec((NB, CHUNK, D_MODEL), lambda j: (0, j, 0)),
        pl.BlockSpec((None, NB, CHUNK, D_PLE), lambda j: (layer, 0, j, 0)),
    ]
    for arr in stacked.values():
        index = (layer,) + (0,) * (arr.ndim - 1)
        in_specs.append(pl.BlockSpec((None,) + arr.shape[1:], lambda j, index=index: index,
                                     pipeline_mode=pl.Buffered(1)))
    for arr in shared.values():
        index = (0,) * arr.ndim
        in_specs.append(pl.BlockSpec(arr.shape, lambda j, index=index: index,
                                     pipeline_mode=pl.Buffered(1)))
    scratch = [
        pltpu.VMEM((NB * ZB, D_IN), F32),
        pltpu.VMEM((GROUPS * NB, GRP_W, GRP_W), F32),
        pltpu.VMEM((TL, S5_N), F32),
        pltpu.VMEM((TL, S5_N), F32),
        pltpu.VMEM((NSEG, S5_N), F32),
        pltpu.VMEM((NSEG, S5_N), F32),
        pltpu.VMEM((NSEG, S5_N), F32),
        pltpu.VMEM((NSEG, S5_N), F32),
        pltpu.VMEM((TL, BRANCH_W), F32),
        pltpu.VMEM((TL, BRANCH_W), F32),
        pltpu.VMEM((NSEG, BRANCH_W), F32),
        pltpu.VMEM((NSEG, BRANCH_W), F32),
    ]
    return pl.pallas_call(
        functools.partial(_layer_body, tuple(names), is_last),
        grid=(seq // CHUNK,),
        in_specs=in_specs,
        out_specs=pl.BlockSpec((NB, CHUNK, D_MODEL), lambda j: (0, j, 0)),
        out_shape=jax.ShapeDtypeStruct(h.shape, F32),
        scratch_shapes=scratch,
        compiler_params=pltpu.CompilerParams(
            dimension_semantics=("arbitrary",),
            vmem_limit_bytes=VMEM_LIMIT),
        name="hybrid_layer",
    )(*arrays)


def kernel(x, p, norm_g, w_in, rwkv_mu, rwkv_w0, rwkv_w2, rwkv_a0, rwkv_a2, rwkv_k_k, rwkv_k_a, rwkv_r_k, rwkv_ln_w, rwkv_ln_b, s5_a_re, s5_a_im, s5_log_dt, s5_b_re, s5_b_im, s5_c_re, s5_c_im, s5_d, s5_glu_w, s5_glu_b, lru_conv_w, lru_conv_b, lru_wa, lru_ba, lru_wx, lru_bx, lru_lambda, w_out, ple_w, ple_norm_g, ple_gate_w, final_norm_g):
    depth = w_in.shape[0]
    perm, perm_t, tri, ones_pair = _constants()
    shared = {"ones_pair": ones_pair, "chunk_tri": tri, "perm": perm, "perm_t": perm_t}

    s5_b, s5_c, pow_re, pow_im = _s5_params(s5_a_re, s5_a_im, s5_log_dt, s5_b_re, s5_b_im, s5_c_re, s5_c_im)
    zero = jnp.zeros((depth, LORA, BRANCH_W), F32)
    lora = jnp.concatenate([jnp.concatenate([rwkv_w2, zero], axis=2),
                            jnp.concatenate([zero, rwkv_a2], axis=2)], axis=1)
    pair_blocks = lambda m: _block_diag(m.reshape(depth, PAIRS, 2, LRU_BLOCK_DIM, LRU_BLOCK_DIM))
    vectors = {
        "rwkv_w0": rwkv_w0, "rwkv_a0": rwkv_a0, "rwkv_k_k": rwkv_k_k, "rwkv_k_a": rwkv_k_a,
        "rwkv_r_k": rwkv_r_k.reshape(depth, BRANCH_W), "rwkv_ln_w": rwkv_ln_w, "rwkv_ln_b": rwkv_ln_b,
        "s5_d": s5_d, "s5_glu_b": s5_glu_b, "lru_conv_b": lru_conv_b, "lru_ba": lru_ba, "lru_bx": lru_bx,
        "lru_lambda": lru_lambda, "lru_conv_w": lru_conv_w,
        "norm_g": norm_g, "ple_norm_g": ple_norm_g,
        "final_norm_g": jnp.broadcast_to(final_norm_g, (depth, D_MODEL)),
    }
    stacked = {
        table: jnp.concatenate([vectors[name].astype(F32).reshape(depth, rows, width) for name, rows in members], axis=1)
        for table, (members, width) in VEC_TABLES.items()
    }
    stacked.update({
        "rwkv_mu": rwkv_mu.astype(F32)[:, None, :],
        "w_in": w_in.astype(BF16),
        "rwkv_lora": lora.astype(BF16),
        "s5_b": s5_b,
        "s5_c": s5_c,
        "s5_pow_re": pow_re,
        "s5_pow_im": pow_im,
        "s5_powb_re": jnp.repeat(pow_re, NSEG, axis=1).astype(BF16),
        "s5_powb_im": jnp.repeat(pow_im, NSEG, axis=1).astype(BF16),
        "s5_glu_w": s5_glu_w.astype(BF16),
        "lru_gates": jnp.concatenate([pair_blocks(lru_wa), pair_blocks(lru_wx)], axis=-1).astype(BF16),
        "w_out": w_out.astype(BF16),
        "ple_w": ple_w.astype(BF16),
        "ple_gate_w": ple_gate_w.astype(BF16),
    })
    h = x.astype(F32)
    p = p.astype(F32)
    for i in range(depth):
        h = _layer_call(h, p, i, stacked, shared, is_last=(i == depth - 1))
    return h.astype(x.dtype)
```

```python
import functools
import math

import jax
import jax.numpy as jnp
import numpy as np
from jax import lax
from jax.experimental import pallas as pl
from jax.experimental.pallas import tpu as pltpu

F32 = jnp.float32
BF16 = jnp.bfloat16

D_MODEL = 1024
D_PLE = 256
BRANCH_W = 512
RWKV_HEADS = 8
HEAD_DIM = 64
PAIR_W = 2 * HEAD_DIM
PAIRS = RWKV_HEADS // 2
GRP_HEADS = 2
GRP_W = GRP_HEADS * HEAD_DIM
GROUPS = RWKV_HEADS // GRP_HEADS
LORA = 64
RWKV_GN_EPS = 64e-5
S5_GROUPS = 32
S5_GROUP = 16
S5_STATE = 64
S5_N = S5_GROUPS * S5_STATE
S5_BLOCKS = 4
S5_BLK_IN = BRANCH_W // S5_BLOCKS
S5_BLK_ST = S5_N // S5_BLOCKS
LRU_BLOCKS = 8
LRU_BLOCK_DIM = 64
CONV_WIDTH = 4
LRU_C = 8.0
NORM_EPS = 1e-6
RWKV_SHIFT_W = 3 * BRANCH_W + 2 * LORA
C_GRW = RWKV_SHIFT_W
C_US5 = C_GRW + BRANCH_W
C_GS5 = C_US5 + BRANCH_W
C_XLRU = C_GS5 + BRANCH_W
C_GLRU = C_XLRU + BRANCH_W
D_IN = C_GLRU + BRANCH_W

NB = 4
CHUNK = 64
TL = NB * CHUNK
NCH = NB
NSEG = 8
SEG = TL // NSEG
SEGS_PER_SEQ = NSEG // NB
CARRY = 8
ZB = CARRY + CHUNK
BF16_ROWS = 16
IN_COL_BLKS = ((0, 1280), (1280, 2560), (2560, D_IN))
SCAN_LANES = 512
SCAN_UNROLL = 4
VMEM_LIMIT = 56 * 1024 * 1024

VEC_TABLES = {
    "vec512": ((("rwkv_w0", 1), ("rwkv_a0", 1), ("rwkv_k_k", 1), ("rwkv_k_a", 1), ("rwkv_r_k", 1),
                ("rwkv_ln_w", 1), ("rwkv_ln_b", 1), ("s5_d", 1), ("s5_glu_b", 1), ("lru_conv_b", 1),
                ("lru_ba", 1), ("lru_bx", 1), ("lru_lambda", 1), ("lru_conv_w", CONV_WIDTH)), BRANCH_W),
    "vec1024": ((("norm_g", 1), ("ple_norm_g", 1), ("final_norm_g", 1)), D_MODEL),
}


def _dot(a, b):
    return jnp.dot(a.astype(BF16), b.astype(BF16), preferred_element_type=F32)


def _bdot(a, b, lhs_c, rhs_c):
    dims = (((lhs_c,), (rhs_c,)), ((0,), (0,)))
    return lax.dot_general(a.astype(BF16), b.astype(BF16), dims, preferred_element_type=F32)


def _hi_lo(x):
    hi = x.astype(BF16)
    lo = (x - hi.astype(F32)).astype(BF16)
    return hi, lo


def _dot_hilo(m, x):
    hi, lo = _hi_lo(x)
    return (jnp.dot(m, hi, preferred_element_type=F32)
            + jnp.dot(m, lo, preferred_element_type=F32))


def _sigmoid(x):
    return 0.5 * jnp.tanh(0.5 * x) + 0.5


def _silu(x):
    return x * _sigmoid(x)


def _softplus(x):
    return jnp.maximum(x, 0.0) + jnp.log(1.0 + jnp.exp(-jnp.abs(x)))


def _rmsnorm(x, g):
    return x * lax.rsqrt(jnp.mean(x * x, axis=-1, keepdims=True) + NORM_EPS) * g


def _split_pairs(x):
    x3 = x.reshape(NCH, CHUNK, BRANCH_W)
    return jnp.concatenate([x3[:, :, j * GRP_W:(j + 1) * GRP_W] for j in range(GROUPS)], axis=0)


def _merge_pairs(y):
    return jnp.concatenate([y[j * NCH:(j + 1) * NCH].reshape(TL, GRP_W) for j in range(GROUPS)], axis=-1)


def _bd(x):
    x = x.astype(BF16)
    head = lax.broadcasted_iota(jnp.int32, x.shape, 2) // HEAD_DIM
    zero = jnp.zeros_like(x)
    return jnp.concatenate([jnp.where(head == g, x, zero) for g in range(GRP_HEADS)], axis=1)


def _pair_nn(x, y):
    return _bdot(x, _bd(y), 2, 1)


def _head_sum(x, ones_pair):
    return jnp.concatenate([_dot(x[:, j * PAIR_W:(j + 1) * PAIR_W], ones_pair) for j in range(PAIRS)], axis=-1)


def _unit_lower_inverse_minus_eye(low, t_idx, s_idx):
    e = None
    size = 1
    while size < CHUNK:
        sibling = ((t_idx // (2 * size)) == (s_idx // (2 * size))) & ((t_idx // size) != (s_idx // size))
        off = jnp.where(sibling[None], low, 0.0)
        if e is None:
            e = -off
        else:
            a = off + _pair_nn(e, off)
            e = e - a - _pair_nn(a, e)
        size *= 2
    return e


def _zrows(zin_ref, lo, hi, back=0):
    return jnp.concatenate([zin_ref[pl.ds(b * ZB + CARRY - back, CHUNK), lo:hi] for b in range(NB)], axis=0)


def _rwkv_group(zin_ref, w, s_ref):
    mu = w["rwkv_mu"][...]

    def shifted(lo, hi):
        z = _zrows(zin_ref, lo, hi)
        zp = _zrows(zin_ref, lo, hi, back=1)
        return z + mu[:, lo:hi] * (zp - z)

    r = shifted(0, BRANCH_W)
    k = shifted(BRANCH_W, 2 * BRANCH_W)
    v = shifted(2 * BRANCH_W, 3 * BRANCH_W)
    lora_in = shifted(3 * BRANCH_W, RWKV_SHIFT_W)
    lane = lax.broadcasted_iota(jnp.int32, lora_in.shape, 1)
    lora_in = jnp.where(lane < LORA, jnp.tanh(lora_in), lora_in)
    lora = _dot(lora_in, w["rwkv_lora"][...])
    logd = -math.exp(-0.5) * _sigmoid(w["rwkv_w0"][...] + lora[:, :BRANCH_W])
    a = _sigmoid(w["rwkv_a0"][...] + lora[:, BRANCH_W:])

    ones_pair = w["ones_pair"][...]
    kk = k * w["rwkv_k_k"][...]
    kk = kk * lax.rsqrt(jnp.maximum(_head_sum(kk * kk, ones_pair), 1e-24))
    k = k * (1.0 + (a - 1.0) * w["rwkv_k_a"][...])
    b = kk * a

    g_inc = _dot_hilo(w["chunk_tri"][...], logd)
    e_inc = jnp.exp(g_inc)
    e_neg = jnp.exp(-g_inc)
    kkd = _split_pairs(kk * jnp.exp(g_inc - logd))
    rd = _split_pairs(r * e_inc)
    bi = _split_pairs(b * e_neg)
    ki = _split_pairs(k * e_neg)
    vp = _split_pairs(v)
    gamma_c = _split_pairs(e_inc)[:, CHUNK - 1:CHUNK, :]
    bd = bi * gamma_c
    kd = ki * gamma_c

    t_idx = lax.broadcasted_iota(jnp.int32, (CHUNK, GRP_W), 0)
    s_idx = lax.broadcasted_iota(jnp.int32, (CHUNK, GRP_W), 1) % HEAD_DIM
    strict = (s_idx < t_idx)[None]
    incl = (s_idx <= t_idx)[None]
    row = lax.broadcasted_iota(jnp.int32, (GRP_W, GRP_W), 0)
    col = lax.broadcasted_iota(jnp.int32, (GRP_W, GRP_W), 1)
    same_head = ((row // HEAD_DIM) == (col // HEAD_DIM))[None]
    eye = (row == col)[None]

    lhs = jnp.concatenate([kkd, rd], axis=1)
    a_b = _bdot(lhs, _bd(bi), 2, 2)
    a_k = _bdot(lhs, _bd(ki), 2, 2)
    lb = jnp.where(strict, a_b[:, :CHUNK], 0.0)
    mb = jnp.where(incl, a_b[:, CHUNK:], 0.0)
    lk = jnp.where(strict, a_k[:, :CHUNK], 0.0)
    mk = jnp.where(incl, a_k[:, CHUNK:], 0.0)

    e = _unit_lower_inverse_minus_eye(lb, t_idx, s_idx)
    lkv = _pair_nn(lk, vp)
    ex = _bdot(e, jnp.concatenate([_bd(kkd), _bd(lkv)], axis=2), 2, 1)
    wmat = kkd + ex[:, :, :GRP_W]
    uloc = -(lkv + ex[:, :, GRP_W:])
    pmat = jnp.where(eye, gamma_c, 0.0) - jnp.where(same_head, _bdot(wmat, bd, 1, 1), 0.0)
    qmat = jnp.where(same_head, _bdot(jnp.concatenate([uloc, vp], axis=1),
                                      jnp.concatenate([bd, kd], axis=1), 1, 1), 0.0)
    yloc = _bdot(jnp.concatenate([mb, mk], axis=2), jnp.concatenate([_bd(uloc), _bd(vp)], axis=1), 2, 1)
    reff = rd - _pair_nn(mb, wmat)

    state = s_ref[...]
    y = _merge_pairs(yloc + _bdot(reff, state, 2, 2))
    s_ref[...] = _bdot(state, pmat, 2, 1) + qmat

    inv_n = 1.0 / HEAD_DIM
    mean = _head_sum(y, ones_pair) * inv_n
    yc = y - mean
    var = _head_sum(yc * yc, ones_pair) * inv_n
    yn = yc * lax.rsqrt(var + RWKV_GN_EPS) * w["rwkv_ln_w"][...] + w["rwkv_ln_b"][...]
    bonus = _head_sum(r * k * w["rwkv_r_k"][...], ones_pair) * v
    gate = _zrows(zin_ref, C_GRW, C_GRW + BRANCH_W)
    return (yn + bonus) * _silu(gate)


def _first_segment_rows(width):
    assert SEGS_PER_SEQ == 2
    return lax.broadcasted_iota(jnp.int32, (NSEG, width), 0) % SEGS_PER_SEQ == 0


def _s5_scan(zin_ref, w, xre_ref, xim_ref, st_re_ref, st_im_ref, start_re_ref, start_im_ref):
    u = _zrows(zin_ref, C_US5, C_US5 + BRANCH_W)
    u_il = jnp.dot(w["perm"][...], u.astype(BF16), preferred_element_type=F32).astype(BF16)
    for blk in range(S5_BLOCKS):
        bu = jnp.dot(u_il[:, blk * S5_BLK_IN:(blk + 1) * S5_BLK_IN], w["s5_b"][blk],
                     preferred_element_type=F32)
        xre_ref[:, blk * S5_BLK_ST:(blk + 1) * S5_BLK_ST] = bu[:, :S5_BLK_ST]
        xim_ref[:, blk * S5_BLK_ST:(blk + 1) * S5_BLK_ST] = bu[:, S5_BLK_ST:]

    for lb in range(S5_N // SCAN_LANES):
        cols = slice(lb * SCAN_LANES, (lb + 1) * SCAN_LANES)
        lam_re = jnp.broadcast_to(w["s5_pow_re"][0:1, cols], (NSEG, SCAN_LANES))
        lam_im = jnp.broadcast_to(w["s5_pow_im"][0:1, cols], (NSEG, SCAN_LANES))

        def step(t, carry, cols=cols, lam_re=lam_re, lam_im=lam_im):
            sr, si = carry
            rws = pl.ds(pl.multiple_of(t * NSEG, NSEG), NSEG)
            nr = lam_re * sr - lam_im * si + xre_ref[rws, cols]
            ni = lam_re * si + lam_im * sr + xim_ref[rws, cols]
            xre_ref[rws, cols] = nr
            xim_ref[rws, cols] = ni
            return nr, ni

        zero = jnp.zeros((NSEG, SCAN_LANES), F32)
        lax.fori_loop(0, SEG, step, (zero, zero), unroll=SCAN_UNROLL)

    last = pl.ds(TL - NSEG, NSEG)
    end_re = xre_ref[last, :]
    end_im = xim_ref[last, :]
    seg_re = w["s5_pow_re"][SEG - 1:SEG, :]
    seg_im = w["s5_pow_im"][SEG - 1:SEG, :]
    first = _first_segment_rows(S5_N)
    c_re = st_re_ref[...]
    c_im = st_im_ref[...]
    e0_re = end_re + seg_re * c_re - seg_im * c_im
    e0_im = end_im + seg_re * c_im + seg_im * c_re
    start_re = jnp.where(first, c_re, pltpu.roll(e0_re, 1, axis=0))
    start_im = jnp.where(first, c_im, pltpu.roll(e0_im, 1, axis=0))
    start_re_ref[...] = start_re
    start_im_ref[...] = start_im
    e1_re = end_re + seg_re * start_re - seg_im * start_im
    e1_im = end_im + seg_re * start_im + seg_im * start_re
    st_re_ref[...] = jnp.where(first, pltpu.roll(e1_re, NSEG - 1, axis=0), 0.0)
    st_im_ref[...] = jnp.where(first, pltpu.roll(e1_im, NSEG - 1, axis=0), 0.0)


def _s5_output(zin_ref, w, xre_ref, xim_ref, start_re_ref, start_im_ref):
    u = _zrows(zin_ref, C_US5, C_US5 + BRANCH_W)
    ys = []
    grp = (TL // BF16_ROWS, BF16_ROWS, S5_BLK_ST)
    twice = BF16_ROWS // NSEG
    for blk in range(S5_BLOCKS):
        cols = slice(blk * S5_BLK_ST, (blk + 1) * S5_BLK_ST)
        s_re = jnp.concatenate([start_re_ref[:, cols]] * twice, axis=0).astype(BF16)[None]
        s_im = jnp.concatenate([start_im_ref[:, cols]] * twice, axis=0).astype(BF16)[None]
        p_re = w["s5_powb_re"][:, cols].reshape(grp)
        p_im = w["s5_powb_im"][:, cols].reshape(grp)
        x_re = xre_ref[:, cols].astype(BF16).reshape(grp) + (p_re * s_re - p_im * s_im)
        x_im = xim_ref[:, cols].astype(BF16).reshape(grp) + (p_re * s_im + p_im * s_re)
        x_ri = jnp.concatenate([x_re.reshape(TL, S5_BLK_ST), x_im.reshape(TL, S5_BLK_ST)], axis=-1)
        ys.append(jnp.dot(x_ri, w["s5_c"][blk], preferred_element_type=F32))
    y_il = jnp.concatenate(ys, axis=-1)
    y = _dot(w["perm_t"][...], y_il) + w["s5_d"][...] * u
    zg = jax.nn.gelu(y)
    zg = zg * _sigmoid(_dot(zg, w["s5_glu_w"][...]) + w["s5_glu_b"][...])
    gate = _zrows(zin_ref, C_GS5, C_GS5 + BRANCH_W)
    return zg * _silu(gate)


def _lru_scan(zin_ref, w, h_ref, cp_ref, st_ref, start_ref):
    conv_w = w["lru_conv_w"][...]
    xc = w["lru_conv_b"][...]
    for back in range(CONV_WIDTH):
        tap = CONV_WIDTH - 1 - back
        xc = xc + _zrows(zin_ref, C_XLRU, C_XLRU + BRANCH_W, back) * conv_w[tap:tap + 1, :]
    xc = _dot(w["perm"][...], xc)
    gates = [_dot(xc[:, j * PAIR_W:(j + 1) * PAIR_W], w["lru_gates"][j]) for j in range(PAIRS)]
    r = _sigmoid(jnp.concatenate([g[:, :PAIR_W] for g in gates], axis=-1) + w["lru_ba"][...])
    i = _sigmoid(jnp.concatenate([g[:, PAIR_W:] for g in gates], axis=-1) + w["lru_bx"][...])
    log_a = (-LRU_C * _softplus(-w["lru_lambda"][...])) * r
    a = jnp.exp(log_a)
    one_minus_a2 = 1.0 - a * a
    mult = jnp.where(one_minus_a2 > 0.0, one_minus_a2 * lax.rsqrt(one_minus_a2), 0.0)
    h_ref[...] = mult * (i * xc)
    cp_ref[...] = a

    def step(t, carry):
        hs, cs = carry
        rws = pl.ds(pl.multiple_of(t * NSEG, NSEG), NSEG)
        a_t = cp_ref[rws, :]
        hs = a_t * hs + h_ref[rws, :]
        cs = a_t * cs
        h_ref[rws, :] = hs
        cp_ref[rws, :] = cs
        return hs, cs

    lax.fori_loop(0, SEG, step, (jnp.zeros((NSEG, BRANCH_W), F32), jnp.ones((NSEG, BRANCH_W), F32)),
                  unroll=SCAN_UNROLL)

    last = pl.ds(TL - NSEG, NSEG)
    end_h = h_ref[last, :]
    end_cp = cp_ref[last, :]
    first = _first_segment_rows(BRANCH_W)
    carried = st_ref[...]
    start = jnp.where(first, carried, pltpu.roll(end_h + end_cp * carried, 1, axis=0))
    start_ref[...] = start
    st_ref[...] = jnp.where(first, pltpu.roll(end_h + end_cp * start, NSEG - 1, axis=0), 0.0)


def _lru_output(zin_ref, w, h_ref, cp_ref, start_ref):
    start = start_ref[...]
    h3 = h_ref[...].reshape(SEG, NSEG, BRANCH_W) + cp_ref[...].reshape(SEG, NSEG, BRANCH_W) * start[None]
    h = _dot(w["perm_t"][...], h3.reshape(TL, BRANCH_W))
    gate = _zrows(zin_ref, C_GLRU, C_GLRU + BRANCH_W)
    return h * _silu(gate)


class _Rows:
    def __init__(self, ref, start, n=1):
        self.ref, self.start, self.n = ref, start, n

    def __getitem__(self, idx):
        assert idx is Ellipsis
        return self.ref[self.start:self.start + self.n, :]


def _layer_body(names, is_last, *refs):
    n_in = len(names)
    w = dict(zip(names, refs[:n_in]))
    for table, (members, _) in VEC_TABLES.items():
        start = 0
        for name, n_rows in members:
            w[name] = _Rows(w[table], start, n_rows)
            start += n_rows
    out_ref = refs[n_in]
    (zin_ref, s_ref, xre_ref, xim_ref, s5_re_ref, s5_im_ref, s5_start_re_ref, s5_start_im_ref,
     lru_h_ref, lru_cp_ref, lru_st_ref, lru_start_ref) = refs[n_in + 1:]

    @pl.when(pl.program_id(0) == 0)
    def _():
        for b in range(NB):
            zin_ref[b * ZB:b * ZB + CARRY, :] = jnp.zeros((CARRY, D_IN), F32)
        s_ref[...] = jnp.zeros_like(s_ref)
        s5_re_ref[...] = jnp.zeros_like(s5_re_ref)
        s5_im_ref[...] = jnp.zeros_like(s5_im_ref)
        lru_st_ref[...] = jnp.zeros_like(lru_st_ref)

    h = w["h"][...].reshape(TL, D_MODEL)
    xn = _rmsnorm(h, w["norm_g"][...]).astype(BF16)
    for c0, c1 in IN_COL_BLKS:
        z = jnp.dot(xn, w["w_in"][:, c0:c1], preferred_element_type=F32)
        for b in range(NB):
            zin_ref[pl.ds(b * ZB + CARRY, CHUNK), c0:c1] = z[b * CHUNK:(b + 1) * CHUNK]

    y_rw = _rwkv_group(zin_ref, w, s_ref)
    _s5_scan(zin_ref, w, xre_ref, xim_ref, s5_re_ref, s5_im_ref, s5_start_re_ref, s5_start_im_ref)
    y_s5 = _s5_output(zin_ref, w, xre_ref, xim_ref, s5_start_re_ref, s5_start_im_ref)
    _lru_scan(zin_ref, w, lru_h_ref, lru_cp_ref, lru_st_ref, lru_start_ref)
    y_lru = _lru_output(zin_ref, w, lru_h_ref, lru_cp_ref, lru_start_ref)
    for b in range(NB):
        zin_ref[b * ZB:b * ZB + CARRY, :] = zin_ref[b * ZB + CHUNK:(b + 1) * ZB, :]

    h = h + (_dot(y_rw, w["w_out"][0:BRANCH_W, :])
             + _dot(y_s5, w["w_out"][BRANCH_W:2 * BRANCH_W, :])
             + _dot(y_lru, w["w_out"][2 * BRANCH_W:3 * BRANCH_W, :]))
    e = _rmsnorm(_dot(w["p"][...].reshape(TL, D_PLE), w["ple_w"][...]), w["ple_norm_g"][...])
    h = h + e * _sigmoid(_dot(h, w["ple_gate_w"][...]))
    if is_last:
        h = _rmsnorm(h, w["final_norm_g"][...])
    out_ref[...] = h.reshape(NB, CHUNK, D_MODEL)


def _block_diag(blocks):
    n, a, b = blocks.shape[-3:]
    eye = jnp.eye(n, dtype=blocks.dtype)
    return jnp.einsum("...nab,nm->...namb", blocks, eye).reshape(blocks.shape[:-3] + (n * a, n * b))


def _constants():
    nat = np.arange(TL)
    interleaved_row = (nat % SEG) * NSEG + nat // SEG
    perm = np.zeros((TL, TL), np.float32)
    perm[interleaved_row, nat] = 1.0
    t = nat[:, None]
    s = nat[None, :]
    tri = ((t // CHUNK) == (s // CHUNK)) & (s <= t)
    lane = np.arange(PAIR_W)
    ones_pair = (lane[:, None] // HEAD_DIM) == (lane[None, :] // HEAD_DIM)
    as_bf16 = lambda m: jnp.asarray(m.astype(np.float32), dtype=BF16)
    return as_bf16(perm), as_bf16(perm.T), as_bf16(tri), as_bf16(ones_pair)


def _s5_params(a_re, a_im, log_dt, b_re, b_im, c_re, c_im):
    depth = a_re.shape[0]
    a_re, a_im = a_re.astype(F32), a_im.astype(F32)
    dt = jnp.exp(log_dt.astype(F32))[..., None]
    mag = jnp.exp(a_re * dt)
    lr, li = mag * jnp.cos(a_im * dt), mag * jnp.sin(a_im * dt)
    den = a_re * a_re + a_im * a_im
    f_re = ((lr - 1.0) * a_re + li * a_im) / den
    f_im = (li * a_re - (lr - 1.0) * a_im) / den
    b_re, b_im = b_re.astype(F32), b_im.astype(F32)
    bb_re = f_re[..., None] * b_re - f_im[..., None] * b_im
    bb_im = f_re[..., None] * b_im + f_im[..., None] * b_re
    steps = jnp.arange(1, SEG + 1, dtype=F32)[:, None, None]
    pmag = jnp.exp((a_re * dt)[:, None] * steps)
    ang = (a_im * dt)[:, None] * steps
    pow_re = (pmag * jnp.cos(ang)).reshape(depth, SEG, S5_N)
    pow_im = (pmag * jnp.sin(ang)).reshape(depth, SEG, S5_N)
    gpb = S5_GROUPS // S5_BLOCKS

    def blocks(x, rows, cols):
        return _block_diag(x.reshape(depth, S5_BLOCKS, gpb, rows, cols))

    s5_b = jnp.concatenate([blocks(jnp.swapaxes(bb_re, 2, 3), S5_GROUP, S5_STATE),
                            blocks(jnp.swapaxes(bb_im, 2, 3), S5_GROUP, S5_STATE)], axis=-1).astype(BF16)
    s5_c = jnp.concatenate([blocks(jnp.swapaxes(c_re.astype(F32), 2, 3), S5_STATE, S5_GROUP),
                            -blocks(jnp.swapaxes(c_im.astype(F32), 2, 3), S5_STATE, S5_GROUP)], axis=-2).astype(BF16)
    return s5_b, s5_c, pow_re, pow_im


def _layer_call(h, p, layer, stacked, shared, is_last):
    bsz, seq, _ = h.shape
    assert bsz == NB and seq % CHUNK == 0
    names = ["h", "p"] + list(stacked.keys()) + list(shared.keys())
    arrays = [h, p] + list(stacked.values()) + list(shared.values())
    in_specs = [
        pl.BlockSpec((NB, CHUNK, D_MODEL), lambda j: (0, j, 0)),
        pl.BlockSpec((None, NB, CHUNK, D_PLE), lambda j: (layer, 0, j, 0)),
    ]
    for arr in stacked.values():
        index = (layer,) + (0,) * (arr.ndim - 1)
        in_specs.append(pl.BlockSpec((None,) + arr.shape[1:], lambda j, index=index: index,
                                     pipeline_mode=pl.Buffered(1)))
    for arr in shared.values():
        index = (0,) * arr.ndim
        in_specs.append(pl.BlockSpec(arr.shape, lambda j, index=index: index,
                                     pipeline_mode=pl.Buffered(1)))
    scratch = [
        pltpu.VMEM((NB * ZB, D_IN), F32),
        pltpu.VMEM((GROUPS * NB, GRP_W, GRP_W), F32),
        pltpu.VMEM((TL, S5_N), F32),
        pltpu.VMEM((TL, S5_N), F32),
        pltpu.VMEM((NSEG, S5_N), F32),
        pltpu.VMEM((NSEG, S5_N), F32),
        pltpu.VMEM((NSEG, S5_N), F32),
        pltpu.VMEM((NSEG, S5_N), F32),
        pltpu.VMEM((TL, BRANCH_W), F32),
        pltpu.VMEM((TL, BRANCH_W), F32),
        pltpu.VMEM((NSEG, BRANCH_W), F32),
        pltpu.VMEM((NSEG, BRANCH_W), F32),
    ]
    return pl.pallas_call(
        functools.partial(_layer_body, tuple(names), is_last),
        grid=(seq // CHUNK,),
        in_specs=in_specs,
        out_specs=pl.BlockSpec((NB, CHUNK, D_MODEL), lambda j: (0, j, 0)),
        out_shape=jax.ShapeDtypeStruct(h.shape, F32),
        scratch_shapes=scratch,
        compiler_params=pltpu.CompilerParams(
            dimension_semantics=("arbitrary",),
            vmem_limit_bytes=VMEM_LIMIT),
        name="hybrid_layer",
    )(*arrays)


def kernel(x, p, norm_g, w_in, rwkv_mu, rwkv_w0, rwkv_w2, rwkv_a0, rwkv_a2, rwkv_k_k, rwkv_k_a, rwkv_r_k, rwkv_ln_w, rwkv_ln_b, s5_a_re, s5_a_im, s5_log_dt, s5_b_re, s5_b_im, s5_c_re, s5_c_im, s5_d, s5_glu_w, s5_glu_b, lru_conv_w, lru_conv_b, lru_wa, lru_ba, lru_wx, lru_bx, lru_lambda, w_out, ple_w, ple_norm_g, ple_gate_w, final_norm_g):
    depth = w_in.shape[0]
    perm, perm_t, tri, ones_pair = _constants()
    shared = {"ones_pair": ones_pair, "chunk_tri": tri, "perm": perm, "perm_t": perm_t}

    s5_b, s5_c, pow_re, pow_im = _s5_params(s5_a_re, s5_a_im, s5_log_dt, s5_b_re, s5_b_im, s5_c_re, s5_c_im)
    zero = jnp.zeros((depth, LORA, BRANCH_W), F32)
    lora = jnp.concatenate([jnp.concatenate([rwkv_w2, zero], axis=2),
                            jnp.concatenate([zero, rwkv_a2], axis=2)], axis=1)
    pair_blocks = lambda m: _block_diag(m.reshape(depth, PAIRS, 2, LRU_BLOCK_DIM, LRU_BLOCK_DIM))
    vectors = {
        "rwkv_w0": rwkv_w0, "rwkv_a0": rwkv_a0, "rwkv_k_k": rwkv_k_k, "rwkv_k_a": rwkv_k_a,
        "rwkv_r_k": rwkv_r_k.reshape(depth, BRANCH_W), "rwkv_ln_w": rwkv_ln_w, "rwkv_ln_b": rwkv_ln_b,
        "s5_d": s5_d, "s5_glu_b": s5_glu_b, "lru_conv_b": lru_conv_b, "lru_ba": lru_ba, "lru_bx": lru_bx,
        "lru_lambda": lru_lambda, "lru_conv_w": lru_conv_w,
        "norm_g": norm_g, "ple_norm_g": ple_norm_g,
        "final_norm_g": jnp.broadcast_to(final_norm_g, (depth, D_MODEL)),
    }
    stacked = {
        table: jnp.concatenate([vectors[name].astype(F32).reshape(depth, rows, width) for name, rows in members], axis=1)
        for table, (members, width) in VEC_TABLES.items()
    }
    stacked.update({
        "rwkv_mu": rwkv_mu.astype(F32)[:, None, :],
        "w_in": w_in.astype(BF16),
        "rwkv_lora": lora.astype(BF16),
        "s5_b": s5_b,
        "s5_c": s5_c,
        "s5_pow_re": pow_re,
        "s5_pow_im": pow_im,
        "s5_powb_re": jnp.repeat(pow_re, NSEG, axis=1).astype(BF16),
        "s5_powb_im": jnp.repeat(pow_im, NSEG, axis=1).astype(BF16),
        "s5_glu_w": s5_glu_w.astype(BF16),
        "lru_gates": jnp.concatenate([pair_blocks(lru_wa), pair_blocks(lru_wx)], axis=-1).astype(BF16),
        "w_out": w_out.astype(BF16),
        "ple_w": ple_w.astype(BF16),
        "ple_gate_w": ple_gate_w.astype(BF16),
    })
    h = x.astype(F32)
    p = p.astype(F32)
    for i in range(depth):
        h = _layer_call(h, p, i, stacked, shared, is_last=(i == depth - 1))
    return h.astype(x.dtype)
```

```python
import functools
import math

import jax
import jax.numpy as jnp
import numpy as np
from jax import lax
from jax.experimental import pallas as pl
from jax.experimental.pallas import tpu as pltpu

F32 = jnp.float32
BF16 = jnp.bfloat16

D_MODEL = 1024
D_PLE = 256
BRANCH_W = 512
RWKV_HEADS = 8
HEAD_DIM = 64
PAIR_W = 2 * HEAD_DIM
PAIRS = RWKV_HEADS // 2
GRP_HEADS = 2
GRP_W = GRP_HEADS * HEAD_DIM
GROUPS = RWKV_HEADS // GRP_HEADS
LORA = 64
RWKV_GN_EPS = 64e-5
S5_GROUPS = 32
S5_GROUP = 16
S5_STATE = 64
S5_N = S5_GROUPS * S5_STATE
S5_BLOCKS = 4
S5_BLK_IN = BRANCH_W // S5_BLOCKS
S5_BLK_ST = S5_N // S5_BLOCKS
LRU_BLOCKS = 8
LRU_BLOCK_DIM = 64
CONV_WIDTH = 4
LRU_C = 8.0
NORM_EPS = 1e-6
RWKV_SHIFT_W = 3 * BRANCH_W + 2 * LORA
C_GRW = RWKV_SHIFT_W
C_US5 = C_GRW + BRANCH_W
C_GS5 = C_US5 + BRANCH_W
C_XLRU = C_GS5 + BRANCH_W
C_GLRU = C_XLRU + BRANCH_W
D_IN = C_GLRU + BRANCH_W

NB = 4
CHUNK = 64
TL = NB * CHUNK
NCH = NB
NSEG = 8
SEG = TL // NSEG
SEGS_PER_SEQ = NSEG // NB
CARRY = 8
ZB = CARRY + CHUNK
BF16_ROWS = 16
IN_COL_BLKS = ((0, 1280), (1280, 2560), (2560, D_IN))
SCAN_LANES = 1024
SCAN_UNROLL = 4
VMEM_LIMIT = 56 * 1024 * 1024

VEC_TABLES = {
    "vec512": ((("rwkv_w0", 1), ("rwkv_a0", 1), ("rwkv_k_k", 1), ("rwkv_k_a", 1), ("rwkv_r_k", 1),
                ("rwkv_ln_w", 1), ("rwkv_ln_b", 1), ("s5_d", 1), ("s5_glu_b", 1), ("lru_conv_b", 1),
                ("lru_ba", 1), ("lru_bx", 1), ("lru_lambda", 1), ("lru_conv_w", CONV_WIDTH)), BRANCH_W),
    "vec1024": ((("norm_g", 1), ("ple_norm_g", 1), ("final_norm_g", 1)), D_MODEL),
}


def _dot(a, b):
    return jnp.dot(a.astype(BF16), b.astype(BF16), preferred_element_type=F32)


def _bdot(a, b, lhs_c, rhs_c):
    dims = (((lhs_c,), (rhs_c,)), ((0,), (0,)))
    return lax.dot_general(a.astype(BF16), b.astype(BF16), dims, preferred_element_type=F32)


def _hi_lo(x):
    hi = x.astype(BF16)
    lo = (x - hi.astype(F32)).astype(BF16)
    return hi, lo


def _dot_hilo(m, x):
    hi, lo = _hi_lo(x)
    return (jnp.dot(m, hi, preferred_element_type=F32)
            + jnp.dot(m, lo, preferred_element_type=F32))


def _sigmoid(x):
    return 0.5 * jnp.tanh(0.5 * x) + 0.5


def _silu(x):
    return x * _sigmoid(x)


def _softplus(x):
    return jnp.maximum(x, 0.0) + jnp.log(1.0 + jnp.exp(-jnp.abs(x)))


def _rmsnorm(x, g):
    return x * lax.rsqrt(jnp.mean(x * x, axis=-1, keepdims=True) + NORM_EPS) * g


def _split_pairs(x):
    x3 = x.reshape(NCH, CHUNK, BRANCH_W)
    return jnp.concatenate([x3[:, :, j * GRP_W:(j + 1) * GRP_W] for j in range(GROUPS)], axis=0)


def _merge_pairs(y):
    return jnp.concatenate([y[j * NCH:(j + 1) * NCH].reshape(TL, GRP_W) for j in range(GROUPS)], axis=-1)


def _bd(x):
    x = x.astype(BF16)
    head = lax.broadcasted_iota(jnp.int32, x.shape, 2) // HEAD_DIM
    zero = jnp.zeros_like(x)
    return jnp.concatenate([jnp.where(head == g, x, zero) for g in range(GRP_HEADS)], axis=1)


def _pair_nn(x, y):
    return _bdot(x, _bd(y), 2, 1)


def _head_sum(x, ones_pair):
    return jnp.concatenate([_dot(x[:, j * PAIR_W:(j + 1) * PAIR_W], ones_pair) for j in range(PAIRS)], axis=-1)


def _unit_lower_inverse_minus_eye(low, t_idx, s_idx):
    e = None
    size = 1
    while size < CHUNK:
        sibling = ((t_idx // (2 * size)) == (s_idx // (2 * size))) & ((t_idx // size) != (s_idx // size))
        off = jnp.where(sibling[None], low, 0.0)
        if e is None:
            e = -off
        else:
            a = off + _pair_nn(e, off)
            e = e - a - _pair_nn(a, e)
        size *= 2
    return e


def _zrows(zin_ref, lo, hi, back=0):
    return jnp.concatenate([zin_ref[pl.ds(b * ZB + CARRY - back, CHUNK), lo:hi] for b in range(NB)], axis=0)


def _rwkv_group(zin_ref, w, s_ref):
    mu = w["rwkv_mu"][...]

    def shifted(lo, hi):
        z = _zrows(zin_ref, lo, hi)
        zp = _zrows(zin_ref, lo, hi, back=1)
        return z + mu[:, lo:hi] * (zp - z)

    r = shifted(0, BRANCH_W)
    k = shifted(BRANCH_W, 2 * BRANCH_W)
    v = shifted(2 * BRANCH_W, 3 * BRANCH_W)
    lora_in = shifted(3 * BRANCH_W, RWKV_SHIFT_W)
    lane = lax.broadcasted_iota(jnp.int32, lora_in.shape, 1)
    lora_in = jnp.where(lane < LORA, jnp.tanh(lora_in), lora_in)
    lora = _dot(lora_in, w["rwkv_lora"][...])
    logd = -math.exp(-0.5) * _sigmoid(w["rwkv_w0"][...] + lora[:, :BRANCH_W])
    a = _sigmoid(w["rwkv_a0"][...] + lora[:, BRANCH_W:])

    ones_pair = w["ones_pair"][...]
    kk = k * w["rwkv_k_k"][...]
    kk = kk * lax.rsqrt(jnp.maximum(_head_sum(kk * kk, ones_pair), 1e-24))
    k = k * (1.0 + (a - 1.0) * w["rwkv_k_a"][...])
    b = kk * a

    g_inc = _dot_hilo(w["chunk_tri"][...], logd)
    e_inc = jnp.exp(g_inc)
    e_neg = jnp.exp(-g_inc)
    kkd = _split_pairs(kk * jnp.exp(g_inc - logd))
    rd = _split_pairs(r * e_inc)
    bi = _split_pairs(b * e_neg)
    ki = _split_pairs(k * e_neg)
    vp = _split_pairs(v)
    gamma_c = _split_pairs(e_inc)[:, CHUNK - 1:CHUNK, :]
    bd = bi * gamma_c
    kd = ki * gamma_c

    t_idx = lax.broadcasted_iota(jnp.int32, (CHUNK, GRP_W), 0)
    s_idx = lax.broadcasted_iota(jnp.int32, (CHUNK, GRP_W), 1) % HEAD_DIM
    strict = (s_idx < t_idx)[None]
    incl = (s_idx <= t_idx)[None]
    row = lax.broadcasted_iota(jnp.int32, (GRP_W, GRP_W), 0)
    col = lax.broadcasted_iota(jnp.int32, (GRP_W, GRP_W), 1)
    same_head = ((row // HEAD_DIM) == (col // HEAD_DIM))[None]
    eye = (row == col)[None]

    lhs = jnp.concatenate([kkd, rd], axis=1)
    a_b = _bdot(lhs, _bd(bi), 2, 2)
    a_k = _bdot(lhs, _bd(ki), 2, 2)
    lb = jnp.where(strict, a_b[:, :CHUNK], 0.0)
    mb = jnp.where(incl, a_b[:, CHUNK:], 0.0)
    lk = jnp.where(strict, a_k[:, :CHUNK], 0.0)
    mk = jnp.where(incl, a_k[:, CHUNK:], 0.0)

    e = _unit_lower_inverse_minus_eye(lb, t_idx, s_idx)
    lkv = _pair_nn(lk, vp)
    ex = _bdot(e, jnp.concatenate([_bd(kkd), _bd(lkv)], axis=2), 2, 1)
    wmat = kkd + ex[:, :, :GRP_W]
    uloc = -(lkv + ex[:, :, GRP_W:])
    pmat = jnp.where(eye, gamma_c, 0.0) - jnp.where(same_head, _bdot(wmat, bd, 1, 1), 0.0)
    qmat = jnp.where(same_head, _bdot(jnp.concatenate([uloc, vp], axis=1),
                                      jnp.concatenate([bd, kd], axis=1), 1, 1), 0.0)
    yloc = _bdot(jnp.concatenate([mb, mk], axis=2), jnp.concatenate([_bd(uloc), _bd(vp)], axis=1), 2, 1)
    reff = rd - _pair_nn(mb, wmat)

    state = s_ref[...]
    y = _merge_pairs(yloc + _bdot(reff, state, 2, 2))
    s_ref[...] = _bdot(state, pmat, 2, 1) + qmat

    inv_n = 1.0 / HEAD_DIM
    mean = _head_sum(y, ones_pair) * inv_n
    yc = y - mean
    var = _head_sum(yc * yc, ones_pair) * inv_n
    yn = yc * lax.rsqrt(var + RWKV_GN_EPS) * w["rwkv_ln_w"][...] + w["rwkv_ln_b"][...]
    bonus = _head_sum(r * k * w["rwkv_r_k"][...], ones_pair) * v
    gate = _zrows(zin_ref, C_GRW, C_GRW + BRANCH_W)
    return (yn + bonus) * _silu(gate)


def _first_segment_rows(width):
    assert SEGS_PER_SEQ == 2
    return lax.broadcasted_iota(jnp.int32, (NSEG, width), 0) % SEGS_PER_SEQ == 0


def _s5_expand(zin_ref, w, xre_ref, xim_ref):
    u = _zrows(zin_ref, C_US5, C_US5 + BRANCH_W)
    u_il = jnp.dot(w["perm"][...], u.astype(BF16), preferred_element_type=F32).astype(BF16)
    for blk in range(S5_BLOCKS):
        bu = jnp.dot(u_il[:, blk * S5_BLK_IN:(blk + 1) * S5_BLK_IN], w["s5_b"][blk],
                     preferred_element_type=F32)
        xre_ref[:, blk * S5_BLK_ST:(blk + 1) * S5_BLK_ST] = bu[:, :S5_BLK_ST]
        xim_ref[:, blk * S5_BLK_ST:(blk + 1) * S5_BLK_ST] = bu[:, S5_BLK_ST:]


def _segment_scans(w, xre_ref, xim_ref, h_ref, cp_ref):
    n_loops = S5_N // SCAN_LANES
    for lb in range(n_loops):
        cols = slice(lb * SCAN_LANES, (lb + 1) * SCAN_LANES)
        lam_re = jnp.broadcast_to(w["s5_pow_re"][0:1, cols], (NSEG, SCAN_LANES))
        lam_im = jnp.broadcast_to(w["s5_pow_im"][0:1, cols], (NSEG, SCAN_LANES))
        with_lru = lb == n_loops - 1

        def step(t, carry, cols=cols, lam_re=lam_re, lam_im=lam_im, with_lru=with_lru):
            rws = pl.ds(pl.multiple_of(t * NSEG, NSEG), NSEG)
            sr, si = carry[:2]
            nr = lam_re * sr - lam_im * si + xre_ref[rws, cols]
            ni = lam_re * si + lam_im * sr + xim_ref[rws, cols]
            xre_ref[rws, cols] = nr
            xim_ref[rws, cols] = ni
            if not with_lru:
                return nr, ni
            hs, cs = carry[2:]
            a_t = cp_ref[rws, :]
            hs = a_t * hs + h_ref[rws, :]
            cs = a_t * cs
            h_ref[rws, :] = hs
            cp_ref[rws, :] = cs
            return nr, ni, hs, cs

        zero = jnp.zeros((NSEG, SCAN_LANES), F32)
        init = (zero, zero)
        if with_lru:
            init += (jnp.zeros((NSEG, BRANCH_W), F32), jnp.ones((NSEG, BRANCH_W), F32))
        lax.fori_loop(0, SEG, step, init, unroll=SCAN_UNROLL)


def _s5_chain(w, xre_ref, xim_ref, st_re_ref, st_im_ref, start_re_ref, start_im_ref):
    last = pl.ds(TL - NSEG, NSEG)
    end_re = xre_ref[last, :]
    end_im = xim_ref[last, :]
    seg_re = w["s5_pow_re"][SEG - 1:SEG, :]
    seg_im = w["s5_pow_im"][SEG - 1:SEG, :]
    first = _first_segment_rows(S5_N)
    c_re = st_re_ref[...]
    c_im = st_im_ref[...]
    e0_re = end_re + seg_re * c_re - seg_im * c_im
    e0_im = end_im + seg_re * c_im + seg_im * c_re
    start_re = jnp.where(first, c_re, pltpu.roll(e0_re, 1, axis=0))
    start_im = jnp.where(first, c_im, pltpu.roll(e0_im, 1, axis=0))
    start_re_ref[...] = start_re
    start_im_ref[...] = start_im
    e1_re = end_re + seg_re * start_re - seg_im * start_im
    e1_im = end_im + seg_re * start_im + seg_im * start_re
    st_re_ref[...] = jnp.where(first, pltpu.roll(e1_re, NSEG - 1, axis=0), 0.0)
    st_im_ref[...] = jnp.where(first, pltpu.roll(e1_im, NSEG - 1, axis=0), 0.0)


def _s5_output(zin_ref, w, xre_ref, xim_ref, start_re_ref, start_im_ref):
    u = _zrows(zin_ref, C_US5, C_US5 + BRANCH_W)
    ys = []
    grp = (TL // BF16_ROWS, BF16_ROWS, S5_BLK_ST)
    twice = BF16_ROWS // NSEG
    for blk in range(S5_BLOCKS):
        cols = slice(blk * S5_BLK_ST, (blk + 1) * S5_BLK_ST)
        s_re = jnp.concatenate([start_re_ref[:, cols]] * twice, axis=0).astype(BF16)[None]
        s_im = jnp.concatenate([start_im_ref[:, cols]] * twice, axis=0).astype(BF16)[None]
        p_re = w["s5_powb_re"][:, cols].reshape(grp)
        p_im = w["s5_powb_im"][:, cols].reshape(grp)
        x_re = xre_ref[:, cols].astype(BF16).reshape(grp) + (p_re * s_re - p_im * s_im)
        x_im = xim_ref[:, cols].astype(BF16).reshape(grp) + (p_re * s_im + p_im * s_re)
        x_ri = jnp.concatenate([x_re.reshape(TL, S5_BLK_ST), x_im.reshape(TL, S5_BLK_ST)], axis=-1)
        ys.append(jnp.dot(x_ri, w["s5_c"][blk], preferred_element_type=F32))
    y_il = jnp.concatenate(ys, axis=-1)
    y = _dot(w["perm_t"][...], y_il) + w["s5_d"][...] * u
    zg = jax.nn.gelu(y)
    zg = zg * _sigmoid(_dot(zg, w["s5_glu_w"][...]) + w["s5_glu_b"][...])
    gate = _zrows(zin_ref, C_GS5, C_GS5 + BRANCH_W)
    return zg * _silu(gate)


def _lru_prep(zin_ref, w, h_ref, cp_ref):
    conv_w = w["lru_conv_w"][...]
    xc = w["lru_conv_b"][...]
    for back in range(CONV_WIDTH):
        tap = CONV_WIDTH - 1 - back
        xc = xc + _zrows(zin_ref, C_XLRU, C_XLRU + BRANCH_W, back) * conv_w[tap:tap + 1, :]
    xc = _dot(w["perm"][...], xc)
    gates = [_dot(xc[:, j * PAIR_W:(j + 1) * PAIR_W], w["lru_gates"][j]) for j in range(PAIRS)]
    r = _sigmoid(jnp.concatenate([g[:, :PAIR_W] for g in gates], axis=-1) + w["lru_ba"][...])
    i = _sigmoid(jnp.concatenate([g[:, PAIR_W:] for g in gates], axis=-1) + w["lru_bx"][...])
    log_a = (-LRU_C * _softplus(-w["lru_lambda"][...])) * r
    a = jnp.exp(log_a)
    one_minus_a2 = 1.0 - a * a
    mult = jnp.where(one_minus_a2 > 0.0, one_minus_a2 * lax.rsqrt(one_minus_a2), 0.0)
    h_ref[...] = mult * (i * xc)
    cp_ref[...] = a


def _lru_chain(h_ref, cp_ref, st_ref, start_ref):
    last = pl.ds(TL - NSEG, NSEG)
    end_h = h_ref[last, :]
    end_cp = cp_ref[last, :]
    first = _first_segment_rows(BRANCH_W)
    carried = st_ref[...]
    start = jnp.where(first, carried, pltpu.roll(end_h + end_cp * carried, 1, axis=0))
    start_ref[...] = start
    st_ref[...] = jnp.where(first, pltpu.roll(end_h + end_cp * start, NSEG - 1, axis=0), 0.0)


def _lru_output(zin_ref, w, h_ref, cp_ref, start_ref):
    start = start_ref[...]
    h3 = h_ref[...].reshape(SEG, NSEG, BRANCH_W) + cp_ref[...].reshape(SEG, NSEG, BRANCH_W) * start[None]
    h = _dot(w["perm_t"][...], h3.reshape(TL, BRANCH_W))
    gate = _zrows(zin_ref, C_GLRU, C_GLRU + BRANCH_W)
    return h * _silu(gate)


class _Rows:
    def __init__(self, ref, start, n=1):
        self.ref, self.start, self.n = ref, start, n

    def __getitem__(self, idx):
        assert idx is Ellipsis
        return self.ref[self.start:self.start + self.n, :]


def _layer_body(names, is_last, *refs):
    n_in = len(names)
    w = dict(zip(names, refs[:n_in]))
    for table, (members, _) in VEC_TABLES.items():
        start = 0
        for name, n_rows in members:
            w[name] = _Rows(w[table], start, n_rows)
            start += n_rows
    out_ref = refs[n_in]
    (zin_ref, s_ref, xre_ref, xim_ref, s5_re_ref, s5_im_ref, s5_start_re_ref, s5_start_im_ref,
     lru_h_ref, lru_cp_ref, lru_st_ref, lru_start_ref) = refs[n_in + 1:]

    @pl.when(pl.program_id(0) == 0)
    def _():
        for b in range(NB):
            zin_ref[b * ZB:b * ZB + CARRY, :] = jnp.zeros((CARRY, D_IN), F32)
        s_ref[...] = jnp.zeros_like(s_ref)
        s5_re_ref[...] = jnp.zeros_like(s5_re_ref)
        s5_im_ref[...] = jnp.zeros_like(s5_im_ref)
        lru_st_ref[...] = jnp.zeros_like(lru_st_ref)

    h = w["h"][...].reshape(TL, D_MODEL)
    xn = _rmsnorm(h, w["norm_g"][...]).astype(BF16)
    for c0, c1 in IN_COL_BLKS:
        z = jnp.dot(xn, w["w_in"][:, c0:c1], preferred_element_type=F32)
        for b in range(NB):
            zin_ref[pl.ds(b * ZB + CARRY, CHUNK), c0:c1] = z[b * CHUNK:(b + 1) * CHUNK]

    y_rw = _rwkv_group(zin_ref, w, s_ref)
    _s5_expand(zin_ref, w, xre_ref, xim_ref)
    _lru_prep(zin_ref, w, lru_h_ref, lru_cp_ref)
    _segment_scans(w, xre_ref, xim_ref, lru_h_ref, lru_cp_ref)
    _s5_chain(w, xre_ref, xim_ref, s5_re_ref, s5_im_ref, s5_start_re_ref, s5_start_im_ref)
    _lru_chain(lru_h_ref, lru_cp_ref, lru_st_ref, lru_start_ref)
    y_s5 = _s5_output(zin_ref, w, xre_ref, xim_ref, s5_start_re_ref, s5_start_im_ref)
    y_lru = _lru_output(zin_ref, w, lru_h_ref, lru_cp_ref, lru_start_ref)
    for b in range(NB):
        zin_ref[b * ZB:b * ZB + CARRY, :] = zin_ref[b * ZB + CHUNK:(b + 1) * ZB, :]

    y_mix = jnp.concatenate([y_rw.astype(BF16), y_s5.astype(BF16), y_lru.astype(BF16)], axis=-1)
    h = h + jnp.dot(y_mix, w["w_out"][...], preferred_element_type=F32)
    e = _rmsnorm(_dot(w["p"][...].reshape(TL, D_PLE), w["ple_w"][...]), w["ple_norm_g"][...])
    h = h + e * _sigmoid(_dot(h, w["ple_gate_w"][...]))
    if is_last:
        h = _rmsnorm(h, w["final_norm_g"][...])
    out_ref[...] = h.reshape(NB, CHUNK, D_MODEL)


def _block_diag(blocks):
    n, a, b = blocks.shape[-3:]
    eye = jnp.eye(n, dtype=blocks.dtype)
    return jnp.einsum("...nab,nm->...namb", blocks, eye).reshape(blocks.shape[:-3] + (n * a, n * b))


def _constants():
    nat = np.arange(TL)
    interleaved_row = (nat % SEG) * NSEG + nat // SEG
    perm = np.zeros((TL, TL), np.float32)
    perm[interleaved_row, nat] = 1.0
    t = nat[:, None]
    s = nat[None, :]
    tri = ((t // CHUNK) == (s // CHUNK)) & (s <= t)
    lane = np.arange(PAIR_W)
    ones_pair = (lane[:, None] // HEAD_DIM) == (lane[None, :] // HEAD_DIM)
    as_bf16 = lambda m: jnp.asarray(m.astype(np.float32), dtype=BF16)
    return as_bf16(perm), as_bf16(perm.T), as_bf16(tri), as_bf16(ones_pair)


def _s5_params(a_re, a_im, log_dt, b_re, b_im, c_re, c_im):
    depth = a_re.shape[0]
    a_re, a_im = a_re.astype(F32), a_im.astype(F32)
    dt = jnp.exp(log_dt.astype(F32))[..., None]
    mag = jnp.exp(a_re * dt)
    lr, li = mag * jnp.cos(a_im * dt), mag * jnp.sin(a_im * dt)
    den = a_re * a_re + a_im * a_im
    f_re = ((lr - 1.0) * a_re + li * a_im) / den
    f_im = (li * a_re - (lr - 1.0) * a_im) / den
    b_re, b_im = b_re.astype(F32), b_im.astype(F32)
    bb_re = f_re[..., None] * b_re - f_im[..., None] * b_im
    bb_im = f_re[..., None] * b_im + f_im[..., None] * b_re
    steps = jnp.arange(1, SEG + 1, dtype=F32)[:, None, None]
    pmag = jnp.exp((a_re * dt)[:, None] * steps)
    ang = (a_im * dt)[:, None] * steps
    pow_re = (pmag * jnp.cos(ang)).reshape(depth, SEG, S5_N)
    pow_im = (pmag * jnp.sin(ang)).reshape(depth, SEG, S5_N)
    gpb = S5_GROUPS // S5_BLOCKS

    def blocks(x, rows, cols):
        return _block_diag(x.reshape(depth, S5_BLOCKS, gpb, rows, cols))

    s5_b = jnp.concatenate([blocks(jnp.swapaxes(bb_re, 2, 3), S5_GROUP, S5_STATE),
                            blocks(jnp.swapaxes(bb_im, 2, 3), S5_GROUP, S5_STATE)], axis=-1).astype(BF16)
    s5_c = jnp.concatenate([blocks(jnp.swapaxes(c_re.astype(F32), 2, 3), S5_STATE, S5_GROUP),
                            -blocks(jnp.swapaxes(c_im.astype(F32), 2, 3), S5_STATE, S5_GROUP)], axis=-2).astype(BF16)
    return s5_b, s5_c, pow_re, pow_im


def _layer_call(h, p, layer, stacked, shared, is_last):
    bsz, seq, _ = h.shape
    assert bsz == NB and seq % CHUNK == 0
    names = ["h", "p"] + list(stacked.keys()) + list(shared.keys())
    arrays = [h, p] + list(stacked.values()) + list(shared.values())
    in_specs = [
        pl.BlockSpec((NB, CHUNK, D_MODEL), lambda j: (0, j, 0)),
        pl.BlockSpec((None, NB, CHUNK, D_PLE), lambda j: (layer, 0, j, 0)),
    ]
    for arr in stacked.values():
        index = (layer,) + (0,) * (arr.ndim - 1)
        in_specs.append(pl.BlockSpec((None,) + arr.shape[1:], lambda j, index=index: index,
                                     pipeline_mode=pl.Buffered(1)))
    for arr in shared.values():
        index = (0,) * arr.ndim
        in_specs.append(pl.BlockSpec(arr.shape, lambda j, index=index: index,
                                     pipeline_mode=pl.Buffered(1)))
    scratch = [
        pltpu.VMEM((NB * ZB, D_IN), F32),
        pltpu.VMEM((GROUPS * NB, GRP_W, GRP_W), F32),
        pltpu.VMEM((TL, S5_N), F32),
        pltpu.VMEM((TL, S5_N), F32),
        pltpu.VMEM((NSEG, S5_N), F32),
        pltpu.VMEM((NSEG, S5_N), F32),
        pltpu.VMEM((NSEG, S5_N), F32),
        pltpu.VMEM((NSEG, S5_N), F32),
        pltpu.VMEM((TL, BRANCH_W), F32),
        pltpu.VMEM((TL, BRANCH_W), F32),
        pltpu.VMEM((NSEG, BRANCH_W), F32),
        pltpu.VMEM((NSEG, BRANCH_W), F32),
    ]
    return pl.pallas_call(
        functools.partial(_layer_body, tuple(names), is_last),
        grid=(seq // CHUNK,),
        in_specs=in_specs,
        out_specs=pl.BlockSpec((NB, CHUNK, D_MODEL), lambda j: (0, j, 0)),
        out_shape=jax.ShapeDtypeStruct(h.shape, F32),
        scratch_shapes=scratch,
        compiler_params=pltpu.CompilerParams(
            dimension_semantics=("arbitrary",),
            vmem_limit_bytes=VMEM_LIMIT),
        name="hybrid_layer",
    )(*arrays)


def kernel(x, p, norm_g, w_in, rwkv_mu, rwkv_w0, rwkv_w2, rwkv_a0, rwkv_a2, rwkv_k_k, rwkv_k_a, rwkv_r_k, rwkv_ln_w, rwkv_ln_b, s5_a_re, s5_a_im, s5_log_dt, s5_b_re, s5_b_im, s5_c_re, s5_c_im, s5_d, s5_glu_w, s5_glu_b, lru_conv_w, lru_conv_b, lru_wa, lru_ba, lru_wx, lru_bx, lru_lambda, w_out, ple_w, ple_norm_g, ple_gate_w, final_norm_g):
    depth = w_in.shape[0]
    perm, perm_t, tri, ones_pair = _constants()
    shared = {"ones_pair": ones_pair, "chunk_tri": tri, "perm": perm, "perm_t": perm_t}

    s5_b, s5_c, pow_re, pow_im = _s5_params(s5_a_re, s5_a_im, s5_log_dt, s5_b_re, s5_b_im, s5_c_re, s5_c_im)
    zero = jnp.zeros((depth, LORA, BRANCH_W), F32)
    lora = jnp.concatenate([jnp.concatenate([rwkv_w2, zero], axis=2),
                            jnp.concatenate([zero, rwkv_a2], axis=2)], axis=1)
    pair_blocks = lambda m: _block_diag(m.reshape(depth, PAIRS, 2, LRU_BLOCK_DIM, LRU_BLOCK_DIM))
    vectors = {
        "rwkv_w0": rwkv_w0, "rwkv_a0": rwkv_a0, "rwkv_k_k": rwkv_k_k, "rwkv_k_a": rwkv_k_a,
        "rwkv_r_k": rwkv_r_k.reshape(depth, BRANCH_W), "rwkv_ln_w": rwkv_ln_w, "rwkv_ln_b": rwkv_ln_b,
        "s5_d": s5_d, "s5_glu_b": s5_glu_b, "lru_conv_b": lru_conv_b, "lru_ba": lru_ba, "lru_bx": lru_bx,
        "lru_lambda": lru_lambda, "lru_conv_w": lru_conv_w,
        "norm_g": norm_g, "ple_norm_g": ple_norm_g,
        "final_norm_g": jnp.broadcast_to(final_norm_g, (depth, D_MODEL)),
    }
    stacked = {
        table: jnp.concatenate([vectors[name].astype(F32).reshape(depth, rows, width) for name, rows in members], axis=1)
        for table, (members, width) in VEC_TABLES.items()
    }
    stacked.update({
        "rwkv_mu": rwkv_mu.astype(F32)[:, None, :],
        "w_in": w_in.astype(BF16),
        "rwkv_lora": lora.astype(BF16),
        "s5_b": s5_b,
        "s5_c": s5_c,
        "s5_pow_re": pow_re,
        "s5_pow_im": pow_im,
        "s5_powb_re": jnp.repeat(pow_re, NSEG, axis=1).astype(BF16),
        "s5_powb_im": jnp.repeat(pow_im, NSEG, axis=1).astype(BF16),
        "s5_glu_w": s5_glu_w.astype(BF16),
        "lru_gates": jnp.concatenate([pair_blocks(lru_wa), pair_blocks(lru_wx)], axis=-1).astype(BF16),
        "w_out": w_out.astype(BF16),
        "ple_w": ple_w.astype(BF16),
        "ple_gate_w": ple_gate_w.astype(BF16),
    })
    h = x.astype(F32)
    p = p.astype(F32)
    for i in range(depth):
        h = _layer_call(h, p, i, stacked, shared, is_last=(i == depth - 1))
    return h.astype(x.dtype)
```

```python
import functools
import math

import jax
import jax.numpy as jnp
import numpy as np
from jax import lax
from jax.experimental import pallas as pl
from jax.experimental.pallas import tpu as pltpu

F32 = jnp.float32
BF16 = jnp.bfloat16

D_MODEL = 1024
D_PLE = 256
BRANCH_W = 512
RWKV_HEADS = 8
HEAD_DIM = 64
PAIR_W = 2 * HEAD_DIM
PAIRS = RWKV_HEADS // 2
GRP_HEADS = 2
GRP_W = GRP_HEADS * HEAD_DIM
GROUPS = RWKV_HEADS // GRP_HEADS
LORA = 64
RWKV_GN_EPS = 64e-5
S5_GROUPS = 32
S5_GROUP = 16
S5_STATE = 64
S5_N = S5_GROUPS * S5_STATE
S5_BLOCKS = 4
S5_BLK_IN = BRANCH_W // S5_BLOCKS
S5_BLK_ST = S5_N // S5_BLOCKS
LRU_BLOCKS = 8
LRU_BLOCK_DIM = 64
CONV_WIDTH = 4
LRU_C = 8.0
NORM_EPS = 1e-6
RWKV_SHIFT_W = 3 * BRANCH_W + 2 * LORA
C_GRW = RWKV_SHIFT_W
C_US5 = C_GRW + BRANCH_W
C_GS5 = C_US5 + BRANCH_W
C_XLRU = C_GS5 + BRANCH_W
C_GLRU = C_XLRU + BRANCH_W
D_IN = C_GLRU + BRANCH_W

NB = 4
CHUNK = 64
TL = NB * CHUNK
NCH = NB
NSEG = 8
SEG = TL // NSEG
SEGS_PER_SEQ = NSEG // NB
CARRY = 8
ZB = CARRY + CHUNK
BF16_ROWS = 16
IN_COL_BLKS = ((0, 1280), (1280, 2560), (2560, D_IN))
SCAN_LANES = 1024
SCAN_UNROLL = SEG
VMEM_LIMIT = 56 * 1024 * 1024

VEC_TABLES = {
    "vec512": ((("rwkv_w0", 1), ("rwkv_a0", 1), ("rwkv_k_k", 1), ("rwkv_k_a", 1), ("rwkv_r_k", 1),
                ("rwkv_ln_w", 1), ("rwkv_ln_b", 1), ("s5_d", 1), ("s5_glu_b", 1), ("lru_conv_b", 1),
                ("lru_ba", 1), ("lru_bx", 1), ("lru_lambda", 1), ("lru_conv_w", CONV_WIDTH)), BRANCH_W),
    "vec1024": ((("norm_g", 1), ("ple_norm_g", 1), ("final_norm_g", 1)), D_MODEL),
}


def _dot(a, b):
    return jnp.dot(a.astype(BF16), b.astype(BF16), preferred_element_type=F32)


def _bdot(a, b, lhs_c, rhs_c):
    dims = (((lhs_c,), (rhs_c,)), ((0,), (0,)))
    return lax.dot_general(a.astype(BF16), b.astype(BF16), dims, preferred_element_type=F32)


def _hi_lo(x):
    hi = x.astype(BF16)
    lo = (x - hi.astype(F32)).astype(BF16)
    return hi, lo


def _dot_hilo(m, x):
    hi, lo = _hi_lo(x)
    return (jnp.dot(m, hi, preferred_element_type=F32)
            + jnp.dot(m, lo, preferred_element_type=F32))


def _sigmoid(x):
    return 0.5 * jnp.tanh(0.5 * x) + 0.5


def _silu(x):
    return x * _sigmoid(x)


def _softplus(x):
    return jnp.maximum(x, 0.0) + jnp.log(1.0 + jnp.exp(-jnp.abs(x)))


def _rmsnorm(x, g):
    return x * lax.rsqrt(jnp.mean(x * x, axis=-1, keepdims=True) + NORM_EPS) * g


def _split_pairs(x):
    x3 = x.reshape(NCH, CHUNK, BRANCH_W)
    return jnp.concatenate([x3[:, :, j * GRP_W:(j + 1) * GRP_W] for j in range(GROUPS)], axis=0)


def _merge_pairs(y):
    return jnp.concatenate([y[j * NCH:(j + 1) * NCH].reshape(TL, GRP_W) for j in range(GROUPS)], axis=-1)


def _bd(x):
    x = x.astype(BF16)
    head = lax.broadcasted_iota(jnp.int32, x.shape, 2) // HEAD_DIM
    zero = jnp.zeros_like(x)
    return jnp.concatenate([jnp.where(head == g, x, zero) for g in range(GRP_HEADS)], axis=1)


def _pair_nn(x, y):
    return _bdot(x, _bd(y), 2, 1)


def _head_sum(x, ones_pair):
    return jnp.concatenate([_dot(x[:, j * PAIR_W:(j + 1) * PAIR_W], ones_pair) for j in range(PAIRS)], axis=-1)


def _unit_lower_inverse_minus_eye(low, t_idx, s_idx):
    e = None
    size = 1
    while size < CHUNK:
        sibling = ((t_idx // (2 * size)) == (s_idx // (2 * size))) & ((t_idx // size) != (s_idx // size))
        off = jnp.where(sibling[None], low, 0.0)
        if e is None:
            e = -off
        else:
            a = off + _pair_nn(e, off)
            e = e - a - _pair_nn(a, e)
        size *= 2
    return e


def _zrows(zin_ref, lo, hi, back=0):
    return jnp.concatenate([zin_ref[pl.ds(b * ZB + CARRY - back, CHUNK), lo:hi] for b in range(NB)], axis=0)


def _rwkv_group(zin_ref, w, s_ref):
    mu = w["rwkv_mu"][...]

    def shifted(lo, hi):
        z = _zrows(zin_ref, lo, hi)
        zp = _zrows(zin_ref, lo, hi, back=1)
        return z + mu[:, lo:hi] * (zp - z)

    r = shifted(0, BRANCH_W)
    k = shifted(BRANCH_W, 2 * BRANCH_W)
    v = shifted(2 * BRANCH_W, 3 * BRANCH_W)
    lora_in = shifted(3 * BRANCH_W, RWKV_SHIFT_W)
    lane = lax.broadcasted_iota(jnp.int32, lora_in.shape, 1)
    lora_in = jnp.where(lane < LORA, jnp.tanh(lora_in), lora_in)
    lora = _dot(lora_in, w["rwkv_lora"][...])
    logd = -math.exp(-0.5) * _sigmoid(w["rwkv_w0"][...] + lora[:, :BRANCH_W])
    a = _sigmoid(w["rwkv_a0"][...] + lora[:, BRANCH_W:])

    ones_pair = w["ones_pair"][...]
    kk = k * w["rwkv_k_k"][...]
    kk = kk * lax.rsqrt(jnp.maximum(_head_sum(kk * kk, ones_pair), 1e-24))
    k = k * (1.0 + (a - 1.0) * w["rwkv_k_a"][...])
    b = kk * a

    g_inc = _dot_hilo(w["chunk_tri"][...], logd)
    e_inc = jnp.exp(g_inc)
    e_neg = jnp.exp(-g_inc)
    kkd = _split_pairs(kk * jnp.exp(g_inc - logd))
    rd = _split_pairs(r * e_inc)
    bi = _split_pairs(b * e_neg)
    ki = _split_pairs(k * e_neg)
    vp = _split_pairs(v)
    gamma_c = _split_pairs(e_inc)[:, CHUNK - 1:CHUNK, :]
    bd = bi * gamma_c
    kd = ki * gamma_c

    t_idx = lax.broadcasted_iota(jnp.int32, (CHUNK, GRP_W), 0)
    s_idx = lax.broadcasted_iota(jnp.int32, (CHUNK, GRP_W), 1) % HEAD_DIM
    strict = (s_idx < t_idx)[None]
    incl = (s_idx <= t_idx)[None]
    row = lax.broadcasted_iota(jnp.int32, (GRP_W, GRP_W), 0)
    col = lax.broadcasted_iota(jnp.int32, (GRP_W, GRP_W), 1)
    same_head = ((row // HEAD_DIM) == (col // HEAD_DIM))[None]
    eye = (row == col)[None]

    lhs = jnp.concatenate([kkd, rd], axis=1)
    a_b = _bdot(lhs, _bd(bi), 2, 2)
    a_k = _bdot(lhs, _bd(ki), 2, 2)
    lb = jnp.where(strict, a_b[:, :CHUNK], 0.0)
    mb = jnp.where(incl, a_b[:, CHUNK:], 0.0)
    lk = jnp.where(strict, a_k[:, :CHUNK], 0.0)
    mk = jnp.where(incl, a_k[:, CHUNK:], 0.0)

    e = _unit_lower_inverse_minus_eye(lb, t_idx, s_idx)
    lkv = _pair_nn(lk, vp)
    ex = _bdot(e, jnp.concatenate([_bd(kkd), _bd(lkv)], axis=2), 2, 1)
    wmat = kkd + ex[:, :, :GRP_W]
    uloc = -(lkv + ex[:, :, GRP_W:])
    pmat = jnp.where(eye, gamma_c, 0.0) - jnp.where(same_head, _bdot(wmat, bd, 1, 1), 0.0)
    qmat = jnp.where(same_head, _bdot(jnp.concatenate([uloc, vp], axis=1),
                                      jnp.concatenate([bd, kd], axis=1), 1, 1), 0.0)
    yloc = _bdot(jnp.concatenate([mb, mk], axis=2), jnp.concatenate([_bd(uloc), _bd(vp)], axis=1), 2, 1)
    reff = rd - _pair_nn(mb, wmat)

    state = s_ref[...]
    y = _merge_pairs(yloc + _bdot(reff, state, 2, 2))
    s_ref[...] = _bdot(state, pmat, 2, 1) + qmat

    inv_n = 1.0 / HEAD_DIM
    mean = _head_sum(y, ones_pair) * inv_n
    yc = y - mean
    var = _head_sum(yc * yc, ones_pair) * inv_n
    yn = yc * lax.rsqrt(var + RWKV_GN_EPS) * w["rwkv_ln_w"][...] + w["rwkv_ln_b"][...]
    bonus = _head_sum(r * k * w["rwkv_r_k"][...], ones_pair) * v
    gate = _zrows(zin_ref, C_GRW, C_GRW + BRANCH_W)
    return (yn + bonus) * _silu(gate)


def _first_segment_rows(width):
    assert SEGS_PER_SEQ == 2
    return lax.broadcasted_iota(jnp.int32, (NSEG, width), 0) % SEGS_PER_SEQ == 0


def _s5_expand(zin_ref, w, xre_ref, xim_ref):
    u = _zrows(zin_ref, C_US5, C_US5 + BRANCH_W)
    u_il = jnp.dot(w["perm"][...], u.astype(BF16), preferred_element_type=F32).astype(BF16)
    for blk in range(S5_BLOCKS):
        bu = jnp.dot(u_il[:, blk * S5_BLK_IN:(blk + 1) * S5_BLK_IN], w["s5_b"][blk],
                     preferred_element_type=F32)
        xre_ref[:, blk * S5_BLK_ST:(blk + 1) * S5_BLK_ST] = bu[:, :S5_BLK_ST]
        xim_ref[:, blk * S5_BLK_ST:(blk + 1) * S5_BLK_ST] = bu[:, S5_BLK_ST:]


def _segment_scans(w, xre_ref, xim_ref, h_ref, cp_ref):
    n_loops = S5_N // SCAN_LANES
    for lb in range(n_loops):
        cols = slice(lb * SCAN_LANES, (lb + 1) * SCAN_LANES)
        lam_re = jnp.broadcast_to(w["s5_pow_re"][0:1, cols], (NSEG, SCAN_LANES))
        lam_im = jnp.broadcast_to(w["s5_pow_im"][0:1, cols], (NSEG, SCAN_LANES))
        with_lru = lb == n_loops - 1

        def step(t, carry, cols=cols, lam_re=lam_re, lam_im=lam_im, with_lru=with_lru):
            rws = pl.ds(pl.multiple_of(t * NSEG, NSEG), NSEG)
            sr, si = carry[:2]
            nr = lam_re * sr - lam_im * si + xre_ref[rws, cols]
            ni = lam_re * si + lam_im * sr + xim_ref[rws, cols]
            xre_ref[rws, cols] = nr
            xim_ref[rws, cols] = ni
            if not with_lru:
                return nr, ni
            hs, cs = carry[2:]
            a_t = cp_ref[rws, :]
            hs = a_t * hs + h_ref[rws, :]
            cs = a_t * cs
            h_ref[rws, :] = hs
            cp_ref[rws, :] = cs
            return nr, ni, hs, cs

        zero = jnp.zeros((NSEG, SCAN_LANES), F32)
        init = (zero, zero)
        if with_lru:
            init += (jnp.zeros((NSEG, BRANCH_W), F32), jnp.ones((NSEG, BRANCH_W), F32))
        lax.fori_loop(0, SEG, step, init, unroll=SCAN_UNROLL)


def _s5_chain(w, xre_ref, xim_ref, st_re_ref, st_im_ref, start_re_ref, start_im_ref):
    last = pl.ds(TL - NSEG, NSEG)
    end_re = xre_ref[last, :]
    end_im = xim_ref[last, :]
    seg_re = w["s5_pow_re"][SEG - 1:SEG, :]
    seg_im = w["s5_pow_im"][SEG - 1:SEG, :]
    first = _first_segment_rows(S5_N)
    c_re = st_re_ref[...]
    c_im = st_im_ref[...]
    e0_re = end_re + seg_re * c_re - seg_im * c_im
    e0_im = end_im + seg_re * c_im + seg_im * c_re
    start_re = jnp.where(first, c_re, pltpu.roll(e0_re, 1, axis=0))
    start_im = jnp.where(first, c_im, pltpu.roll(e0_im, 1, axis=0))
    start_re_ref[...] = start_re
    start_im_ref[...] = start_im
    e1_re = end_re + seg_re * start_re - seg_im * start_im
    e1_im = end_im + seg_re * start_im + seg_im * start_re
    st_re_ref[...] = jnp.where(first, pltpu.roll(e1_re, NSEG - 1, axis=0), 0.0)
    st_im_ref[...] = jnp.where(first, pltpu.roll(e1_im, NSEG - 1, axis=0), 0.0)


def _s5_output(zin_ref, w, xre_ref, xim_ref, start_re_ref, start_im_ref):
    u = _zrows(zin_ref, C_US5, C_US5 + BRANCH_W)
    ys = []
    grp = (TL // BF16_ROWS, BF16_ROWS, S5_BLK_ST)
    twice = BF16_ROWS // NSEG
    for blk in range(S5_BLOCKS):
        cols = slice(blk * S5_BLK_ST, (blk + 1) * S5_BLK_ST)
        s_re = jnp.concatenate([start_re_ref[:, cols]] * twice, axis=0).astype(BF16)[None]
        s_im = jnp.concatenate([start_im_ref[:, cols]] * twice, axis=0).astype(BF16)[None]
        p_re = w["s5_powb_re"][:, cols].reshape(grp)
        p_im = w["s5_powb_im"][:, cols].reshape(grp)
        x_re = xre_ref[:, cols].astype(BF16).reshape(grp) + (p_re * s_re - p_im * s_im)
        x_im = xim_ref[:, cols].astype(BF16).reshape(grp) + (p_re * s_im + p_im * s_re)
        x_ri = jnp.concatenate([x_re.reshape(TL, S5_BLK_ST), x_im.reshape(TL, S5_BLK_ST)], axis=-1)
        ys.append(jnp.dot(x_ri, w["s5_c"][blk], preferred_element_type=F32))
    y_il = jnp.concatenate(ys, axis=-1)
    y = _dot(w["perm_t"][...], y_il) + w["s5_d"][...] * u
    zg = jax.nn.gelu(y)
    zg = zg * _sigmoid(_dot(zg, w["s5_glu_w"][...]) + w["s5_glu_b"][...])
    gate = _zrows(zin_ref, C_GS5, C_GS5 + BRANCH_W)
    return zg * _silu(gate)


def _lru_prep(zin_ref, w, h_ref, cp_ref):
    conv_w = w["lru_conv_w"][...]
    xc = w["lru_conv_b"][...]
    for back in range(CONV_WIDTH):
        tap = CONV_WIDTH - 1 - back
        xc = xc + _zrows(zin_ref, C_XLRU, C_XLRU + BRANCH_W, back) * conv_w[tap:tap + 1, :]
    xc = _dot(w["perm"][...], xc)
    gates = [_dot(xc[:, j * PAIR_W:(j + 1) * PAIR_W], w["lru_gates"][j]) for j in range(PAIRS)]
    r = _sigmoid(jnp.concatenate([g[:, :PAIR_W] for g in gates], axis=-1) + w["lru_ba"][...])
    i = _sigmoid(jnp.concatenate([g[:, PAIR_W:] for g in gates], axis=-1) + w["lru_bx"][...])
    log_a = (-LRU_C * _softplus(-w["lru_lambda"][...])) * r
    a = jnp.exp(log_a)
    one_minus_a2 = 1.0 - a * a
    mult = jnp.where(one_minus_a2 > 0.0, one_minus_a2 * lax.rsqrt(one_minus_a2), 0.0)
    h_ref[...] = mult * (i * xc)
    cp_ref[...] = a


def _lru_chain(h_ref, cp_ref, st_ref, start_ref):
    last = pl.ds(TL - NSEG, NSEG)
    end_h = h_ref[last, :]
    end_cp = cp_ref[last, :]
    first = _first_segment_rows(BRANCH_W)
    carried = st_ref[...]
    start = jnp.where(first, carried, pltpu.roll(end_h + end_cp * carried, 1, axis=0))
    start_ref[...] = start
    st_ref[...] = jnp.where(first, pltpu.roll(end_h + end_cp * start, NSEG - 1, axis=0), 0.0)


def _lru_output(zin_ref, w, h_ref, cp_ref, start_ref):
    start = start_ref[...]
    h3 = h_ref[...].reshape(SEG, NSEG, BRANCH_W) + cp_ref[...].reshape(SEG, NSEG, BRANCH_W) * start[None]
    h = _dot(w["perm_t"][...], h3.reshape(TL, BRANCH_W))
    gate = _zrows(zin_ref, C_GLRU, C_GLRU + BRANCH_W)
    return h * _silu(gate)


class _Rows:
    def __init__(self, ref, start, n=1):
        self.ref, self.start, self.n = ref, start, n

    def __getitem__(self, idx):
        assert idx is Ellipsis
        return self.ref[self.start:self.start + self.n, :]


def _layer_body(names, is_last, *refs):
    n_in = len(names)
    w = dict(zip(names, refs[:n_in]))
    for table, (members, _) in VEC_TABLES.items():
        start = 0
        for name, n_rows in members:
            w[name] = _Rows(w[table], start, n_rows)
            start += n_rows
    out_ref = refs[n_in]
    (zin_ref, s_ref, xre_ref, xim_ref, s5_re_ref, s5_im_ref, s5_start_re_ref, s5_start_im_ref,
     lru_h_ref, lru_cp_ref, lru_st_ref, lru_start_ref) = refs[n_in + 1:]

    @pl.when(pl.program_id(0) == 0)
    def _():
        for b in range(NB):
            zin_ref[b * ZB:b * ZB + CARRY, :] = jnp.zeros((CARRY, D_IN), F32)
        s_ref[...] = jnp.zeros_like(s_ref)
        s5_re_ref[...] = jnp.zeros_like(s5_re_ref)
        s5_im_ref[...] = jnp.zeros_like(s5_im_ref)
        lru_st_ref[...] = jnp.zeros_like(lru_st_ref)

    h = w["h"][...].reshape(TL, D_MODEL)
    xn = _rmsnorm(h, w["norm_g"][...]).astype(BF16)
    for c0, c1 in IN_COL_BLKS:
        z = jnp.dot(xn, w["w_in"][:, c0:c1], preferred_element_type=F32)
        for b in range(NB):
            zin_ref[pl.ds(b * ZB + CARRY, CHUNK), c0:c1] = z[b * CHUNK:(b + 1) * CHUNK]

    y_rw = _rwkv_group(zin_ref, w, s_ref)
    _s5_expand(zin_ref, w, xre_ref, xim_ref)
    _lru_prep(zin_ref, w, lru_h_ref, lru_cp_ref)
    _segment_scans(w, xre_ref, xim_ref, lru_h_ref, lru_cp_ref)
    _s5_chain(w, xre_ref, xim_ref, s5_re_ref, s5_im_ref, s5_start_re_ref, s5_start_im_ref)
    _lru_chain(lru_h_ref, lru_cp_ref, lru_st_ref, lru_start_ref)
    y_s5 = _s5_output(zin_ref, w, xre_ref, xim_ref, s5_start_re_ref, s5_start_im_ref)
    y_lru = _lru_output(zin_ref, w, lru_h_ref, lru_cp_ref, lru_start_ref)
    for b in range(NB):
        zin_ref[b * ZB:b * ZB + CARRY, :] = zin_ref[b * ZB + CHUNK:(b + 1) * ZB, :]

    y_mix = jnp.concatenate([y_rw.astype(BF16), y_s5.astype(BF16), y_lru.astype(BF16)], axis=-1)
    h = h + jnp.dot(y_mix, w["w_out"][...], preferred_element_type=F32)
    e = _rmsnorm(_dot(w["p"][...].reshape(TL, D_PLE), w["ple_w"][...]), w["ple_norm_g"][...])
    h = h + e * _sigmoid(_dot(h, w["ple_gate_w"][...]))
    if is_last:
        h = _rmsnorm(h, w["final_norm_g"][...])
    out_ref[...] = h.reshape(NB, CHUNK, D_MODEL)


def _block_diag(blocks):
    n, a, b = blocks.shape[-3:]
    eye = jnp.eye(n, dtype=blocks.dtype)
    return jnp.einsum("...nab,nm->...namb", blocks, eye).reshape(blocks.shape[:-3] + (n * a, n * b))


def _constants():
    nat = np.arange(TL)
    interleaved_row = (nat % SEG) * NSEG + nat // SEG
    perm = np.zeros((TL, TL), np.float32)
    perm[interleaved_row, nat] = 1.0
    t = nat[:, None]
    s = nat[None, :]
    tri = ((t // CHUNK) == (s // CHUNK)) & (s <= t)
    lane = np.arange(PAIR_W)
    ones_pair = (lane[:, None] // HEAD_DIM) == (lane[None, :] // HEAD_DIM)
    as_bf16 = lambda m: jnp.asarray(m.astype(np.float32), dtype=BF16)
    return as_bf16(perm), as_bf16(perm.T), as_bf16(tri), as_bf16(ones_pair)


def _s5_params(a_re, a_im, log_dt, b_re, b_im, c_re, c_im):
    depth = a_re.shape[0]
    a_re, a_im = a_re.astype(F32), a_im.astype(F32)
    dt = jnp.exp(log_dt.astype(F32))[..., None]
    mag = jnp.exp(a_re * dt)
    lr, li = mag * jnp.cos(a_im * dt), mag * jnp.sin(a_im * dt)
    den = a_re * a_re + a_im * a_im
    f_re = ((lr - 1.0) * a_re + li * a_im) / den
    f_im = (li * a_re - (lr - 1.0) * a_im) / den
    b_re, b_im = b_re.astype(F32), b_im.astype(F32)
    bb_re = f_re[..., None] * b_re - f_im[..., None] * b_im
    bb_im = f_re[..., None] * b_im + f_im[..., None] * b_re
    steps = jnp.arange(1, SEG + 1, dtype=F32)[:, None, None]
    pmag = jnp.exp((a_re * dt)[:, None] * steps)
    ang = (a_im * dt)[:, None] * steps
    pow_re = (pmag * jnp.cos(ang)).reshape(depth, SEG, S5_N)
    pow_im = (pmag * jnp.sin(ang)).reshape(depth, SEG, S5_N)
    gpb = S5_GROUPS // S5_BLOCKS

    def blocks(x, rows, cols):
        return _block_diag(x.reshape(depth, S5_BLOCKS, gpb, rows, cols))

    s5_b = jnp.concatenate([blocks(jnp.swapaxes(bb_re, 2, 3), S5_GROUP, S5_STATE),
                            blocks(jnp.swapaxes(bb_im, 2, 3), S5_GROUP, S5_STATE)], axis=-1).astype(BF16)
    s5_c = jnp.concatenate([blocks(jnp.swapaxes(c_re.astype(F32), 2, 3), S5_STATE, S5_GROUP),
                            -blocks(jnp.swapaxes(c_im.astype(F32), 2, 3), S5_STATE, S5_GROUP)], axis=-2).astype(BF16)
    return s5_b, s5_c, pow_re, pow_im


def _layer_call(h, p, layer, stacked, shared, is_last):
    bsz, seq, _ = h.shape
    assert bsz == NB and seq % CHUNK == 0
    names = ["h", "p"] + list(stacked.keys()) + list(shared.keys())
    arrays = [h, p] + list(stacked.values()) + list(shared.values())
    in_specs = [
        pl.BlockSpec((NB, CHUNK, D_MODEL), lambda j: (0, j, 0)),
        pl.BlockSpec((None, NB, CHUNK, D_PLE), lambda j: (layer, 0, j, 0)),
    ]
    for arr in stacked.values():
        index = (layer,) + (0,) * (arr.ndim - 1)
        in_specs.append(pl.BlockSpec((None,) + arr.shape[1:], lambda j, index=index: index,
                                     pipeline_mode=pl.Buffered(1)))
    for arr in shared.values():
        index = (0,) * arr.ndim
        in_specs.append(pl.BlockSpec(arr.shape, lambda j, index=index: index,
                                     pipeline_mode=pl.Buffered(1)))
    scratch = [
        pltpu.VMEM((NB * ZB, D_IN), F32),
        pltpu.VMEM((GROUPS * NB, GRP_W, GRP_W), F32),
        pltpu.VMEM((TL, S5_N), F32),
        pltpu.VMEM((TL, S5_N), F32),
        pltpu.VMEM((NSEG, S5_N), F32),
        pltpu.VMEM((NSEG, S5_N), F32),
        pltpu.VMEM((NSEG, S5_N), F32),
        pltpu.VMEM((NSEG, S5_N), F32),
        pltpu.VMEM((TL, BRANCH_W), F32),
        pltpu.VMEM((TL, BRANCH_W), F32),
        pltpu.VMEM((NSEG, BRANCH_W), F32),
        pltpu.VMEM((NSEG, BRANCH_W), F32),
    ]
    return pl.pallas_call(
        functools.partial(_layer_body, tuple(names), is_last),
        grid=(seq // CHUNK,),
        in_specs=in_specs,
        out_specs=pl.BlockSpec((NB, CHUNK, D_MODEL), lambda j: (0, j, 0)),
        out_shape=jax.ShapeDtypeStruct(h.shape, F32),
        scratch_shapes=scratch,
        compiler_params=pltpu.CompilerParams(
            dimension_semantics=("arbitrary",),
            vmem_limit_bytes=VMEM_LIMIT),
        name="hybrid_layer",
    )(*arrays)


def kernel(x, p, norm_g, w_in, rwkv_mu, rwkv_w0, rwkv_w2, rwkv_a0, rwkv_a2, rwkv_k_k, rwkv_k_a, rwkv_r_k, rwkv_ln_w, rwkv_ln_b, s5_a_re, s5_a_im, s5_log_dt, s5_b_re, s5_b_im, s5_c_re, s5_c_im, s5_d, s5_glu_w, s5_glu_b, lru_conv_w, lru_conv_b, lru_wa, lru_ba, lru_wx, lru_bx, lru_lambda, w_out, ple_w, ple_norm_g, ple_gate_w, final_norm_g):
    depth = w_in.shape[0]
    perm, perm_t, tri, ones_pair = _constants()
    shared = {"ones_pair": ones_pair, "chunk_tri": tri, "perm": perm, "perm_t": perm_t}

    s5_b, s5_c, pow_re, pow_im = _s5_params(s5_a_re, s5_a_im, s5_log_dt, s5_b_re, s5_b_im, s5_c_re, s5_c_im)
    zero = jnp.zeros((depth, LORA, BRANCH_W), F32)
    lora = jnp.concatenate([jnp.concatenate([rwkv_w2, zero], axis=2),
                            jnp.concatenate([zero, rwkv_a2], axis=2)], axis=1)
    pair_blocks = lambda m: _block_diag(m.reshape(depth, PAIRS, 2, LRU_BLOCK_DIM, LRU_BLOCK_DIM))
    vectors = {
        "rwkv_w0": rwkv_w0, "rwkv_a0": rwkv_a0, "rwkv_k_k": rwkv_k_k, "rwkv_k_a": rwkv_k_a,
        "rwkv_r_k": rwkv_r_k.reshape(depth, BRANCH_W), "rwkv_ln_w": rwkv_ln_w, "rwkv_ln_b": rwkv_ln_b,
        "s5_d": s5_d, "s5_glu_b": s5_glu_b, "lru_conv_b": lru_conv_b, "lru_ba": lru_ba, "lru_bx": lru_bx,
        "lru_lambda": lru_lambda, "lru_conv_w": lru_conv_w,
        "norm_g": norm_g, "ple_norm_g": ple_norm_g,
        "final_norm_g": jnp.broadcast_to(final_norm_g, (depth, D_MODEL)),
    }
    stacked = {
        table: jnp.concatenate([vectors[name].astype(F32).reshape(depth, rows, width) for name, rows in members], axis=1)
        for table, (members, width) in VEC_TABLES.items()
    }
    stacked.update({
        "rwkv_mu": rwkv_mu.astype(F32)[:, None, :],
        "w_in": w_in.astype(BF16),
        "rwkv_lora": lora.astype(BF16),
        "s5_b": s5_b,
        "s5_c": s5_c,
        "s5_pow_re": pow_re,
        "s5_pow_im": pow_im,
        "s5_powb_re": jnp.repeat(pow_re, NSEG, axis=1).astype(BF16),
        "s5_powb_im": jnp.repeat(pow_im, NSEG, axis=1).astype(BF16),
        "s5_glu_w": s5_glu_w.astype(BF16),
        "lru_gates": jnp.concatenate([pair_blocks(lru_wa), pair_blocks(lru_wx)], axis=-1).astype(BF16),
        "w_out": w_out.astype(BF16),
        "ple_w": ple_w.astype(BF16),
        "ple_gate_w": ple_gate_w.astype(BF16),
    })
    h = x.astype(F32)
    p = p.astype(F32)
    for i in range(depth):
        h = _layer_call(h, p, i, stacked, shared, is_last=(i == depth - 1))
    return h.astype(x.dtype)
```

```python
import functools
import math

import jax
import jax.numpy as jnp
import numpy as np
from jax import lax
from jax.experimental import pallas as pl
from jax.experimental.pallas import tpu as pltpu

F32 = jnp.float32
BF16 = jnp.bfloat16

D_MODEL = 1024
D_PLE = 256
BRANCH_W = 512
RWKV_HEADS = 8
HEAD_DIM = 64
PAIR_W = 2 * HEAD_DIM
PAIRS = RWKV_HEADS // 2
GRP_HEADS = 2
GRP_W = GRP_HEADS * HEAD_DIM
GROUPS = RWKV_HEADS // GRP_HEADS
LORA = 64
RWKV_GN_EPS = 64e-5
S5_GROUPS = 32
S5_GROUP = 16
S5_STATE = 64
S5_N = S5_GROUPS * S5_STATE
S5_BLOCKS = 4
S5_BLK_IN = BRANCH_W // S5_BLOCKS
S5_BLK_ST = S5_N // S5_BLOCKS
LRU_BLOCK_DIM = 64
CONV_WIDTH = 4
LRU_C = 8.0
NORM_EPS = 1e-6
RWKV_SHIFT_W = 3 * BRANCH_W + 2 * LORA
C_GRW = RWKV_SHIFT_W
C_US5 = C_GRW + BRANCH_W
C_GS5 = C_US5 + BRANCH_W
C_XLRU = C_GS5 + BRANCH_W
C_GLRU = C_XLRU + BRANCH_W
D_IN = C_GLRU + BRANCH_W

NB = 4
CHUNK = 64
TL = NB * CHUNK
NCH = NB
NSEG = 8
SEG = TL // NSEG
SEGS_PER_SEQ = NSEG // NB
CARRY = 8
ZB = CARRY + CHUNK
BF16_ROWS = 16
IN_COL_BLKS = ((0, 1280), (1280, 2560), (2560, D_IN))
VMEM_LIMIT = 56 * 1024 * 1024

VEC_TABLES = {
    "vec512": ((("rwkv_w0", 1), ("rwkv_a0", 1), ("rwkv_k_k", 1), ("rwkv_k_a", 1), ("rwkv_r_k", 1),
                ("rwkv_ln_w", 1), ("rwkv_ln_b", 1), ("s5_d", 1), ("s5_glu_b", 1), ("lru_conv_b", 1),
                ("lru_ba", 1), ("lru_bx", 1), ("lru_lambda", 1), ("lru_conv_w", CONV_WIDTH)), BRANCH_W),
    "vec1024": ((("norm_g", 1), ("ple_norm_g", 1), ("final_norm_g", 1)), D_MODEL),
}


def _dot(a, b):
    return jnp.dot(a.astype(BF16), b.astype(BF16), preferred_element_type=F32)


def _bdot(a, b, lhs_c, rhs_c):
    dims = (((lhs_c,), (rhs_c,)), ((0,), (0,)))
    return lax.dot_general(a.astype(BF16), b.astype(BF16), dims, preferred_element_type=F32)


def _hi_lo(x):
    hi = x.astype(BF16)
    lo = (x - hi.astype(F32)).astype(BF16)
    return hi, lo


def _dot_hilo(m, x):
    hi, lo = _hi_lo(x)
    return (jnp.dot(m, hi, preferred_element_type=F32)
            + jnp.dot(m, lo, preferred_element_type=F32))


def _sigmoid(x):
    return 0.5 * jnp.tanh(0.5 * x) + 0.5


def _silu(x):
    return x * _sigmoid(x)


def _softplus(x):
    return jnp.maximum(x, 0.0) + jnp.log(1.0 + jnp.exp(-jnp.abs(x)))


def _rmsnorm(x, g):
    return x * lax.rsqrt(jnp.mean(x * x, axis=-1, keepdims=True) + NORM_EPS) * g


def _split_pairs(x):
    x3 = x.reshape(NCH, CHUNK, BRANCH_W)
    return jnp.concatenate([x3[:, :, j * GRP_W:(j + 1) * GRP_W] for j in range(GROUPS)], axis=0)


def _merge_pairs(y):
    return jnp.concatenate([y[j * NCH:(j + 1) * NCH].reshape(TL, GRP_W) for j in range(GROUPS)], axis=-1)


def _bd(x):
    x = x.astype(BF16)
    head = lax.broadcasted_iota(jnp.int32, x.shape, 2) // HEAD_DIM
    zero = jnp.zeros_like(x)
    return jnp.concatenate([jnp.where(head == g, x, zero) for g in range(GRP_HEADS)], axis=1)


def _pair_nn(x, y):
    return _bdot(x, _bd(y), 2, 1)


def _head_sum(x, ones_pair):
    return jnp.concatenate([_dot(x[:, j * PAIR_W:(j + 1) * PAIR_W], ones_pair) for j in range(PAIRS)], axis=-1)


def _unit_lower_inverse_minus_eye(low, t_idx, s_idx, side_work):
    n_dots = 2 * (int(math.log2(CHUNK)) - 1)
    per_dot = -(-len(side_work) // n_dots)
    side_work = list(side_work)

    def run_side_work():
        for thunk in side_work[:per_dot]:
            thunk()
        del side_work[:per_dot]

    e = None
    size = 1
    while size < CHUNK:
        sibling = ((t_idx // (2 * size)) == (s_idx // (2 * size))) & ((t_idx // size) != (s_idx // size))
        off = jnp.where(sibling[None], low, 0.0)
        if e is None:
            e = -off
        else:
            a = off + _pair_nn(e, off)
            run_side_work()
            e = e - a - _pair_nn(a, e)
            run_side_work()
        size *= 2
    assert not side_work
    return e


def _zrows(zin_ref, lo, hi, back=0):
    return jnp.concatenate([zin_ref[pl.ds(b * ZB + CARRY - back, CHUNK), lo:hi] for b in range(NB)], axis=0)


def _rwkv_group(zin_ref, w, s_ref, side_work):
    mu = w["rwkv_mu"][...]

    def shifted(lo, hi):
        z = _zrows(zin_ref, lo, hi)
        zp = _zrows(zin_ref, lo, hi, back=1)
        return z + mu[:, lo:hi] * (zp - z)

    r = shifted(0, BRANCH_W)
    k = shifted(BRANCH_W, 2 * BRANCH_W)
    v = shifted(2 * BRANCH_W, 3 * BRANCH_W)
    lora_in = shifted(3 * BRANCH_W, RWKV_SHIFT_W)
    lane = lax.broadcasted_iota(jnp.int32, lora_in.shape, 1)
    lora_in = jnp.where(lane < LORA, jnp.tanh(lora_in), lora_in)
    lora = _dot(lora_in, w["rwkv_lora"][...])
    logd = -math.exp(-0.5) * _sigmoid(w["rwkv_w0"][...] + lora[:, :BRANCH_W])
    a = _sigmoid(w["rwkv_a0"][...] + lora[:, BRANCH_W:])

    ones_pair = w["ones_pair"][...]
    kk = k * w["rwkv_k_k"][...]
    kk = kk * lax.rsqrt(jnp.maximum(_head_sum(kk * kk, ones_pair), 1e-24))
    k = k * (1.0 + (a - 1.0) * w["rwkv_k_a"][...])
    b = kk * a

    g_inc = _dot_hilo(w["chunk_tri"][...], logd)
    e_inc = jnp.exp(g_inc)
    e_neg = jnp.exp(-g_inc)
    kkd = _split_pairs(kk * jnp.exp(g_inc - logd))
    rd = _split_pairs(r * e_inc)
    bi = _split_pairs(b * e_neg)
    ki = _split_pairs(k * e_neg)
    vp = _split_pairs(v)
    gamma_c = _split_pairs(e_inc)[:, CHUNK - 1:CHUNK, :]
    bd = bi * gamma_c
    kd = ki * gamma_c

    t_idx = lax.broadcasted_iota(jnp.int32, (CHUNK, GRP_W), 0)
    s_idx = lax.broadcasted_iota(jnp.int32, (CHUNK, GRP_W), 1) % HEAD_DIM
    strict = (s_idx < t_idx)[None]
    incl = (s_idx <= t_idx)[None]
    row = lax.broadcasted_iota(jnp.int32, (GRP_W, GRP_W), 0)
    col = lax.broadcasted_iota(jnp.int32, (GRP_W, GRP_W), 1)
    same_head = ((row // HEAD_DIM) == (col // HEAD_DIM))[None]
    eye = (row == col)[None]

    lhs = jnp.concatenate([kkd, rd], axis=1)
    a_b = _bdot(lhs, _bd(bi), 2, 2)
    a_k = _bdot(lhs, _bd(ki), 2, 2)
    lb = jnp.where(strict, a_b[:, :CHUNK], 0.0)
    mb = jnp.where(incl, a_b[:, CHUNK:], 0.0)
    lk = jnp.where(strict, a_k[:, :CHUNK], 0.0)
    mk = jnp.where(incl, a_k[:, CHUNK:], 0.0)

    e = _unit_lower_inverse_minus_eye(lb, t_idx, s_idx, side_work)
    lkv = _pair_nn(lk, vp)
    ex = _bdot(e, jnp.concatenate([_bd(kkd), _bd(lkv)], axis=2), 2, 1)
    wmat = kkd + ex[:, :, :GRP_W]
    uloc = -(lkv + ex[:, :, GRP_W:])
    pmat = jnp.where(eye, gamma_c, 0.0) - jnp.where(same_head, _bdot(wmat, bd, 1, 1), 0.0)
    qmat = jnp.where(same_head, _bdot(jnp.concatenate([uloc, vp], axis=1),
                                      jnp.concatenate([bd, kd], axis=1), 1, 1), 0.0)
    yloc = _bdot(jnp.concatenate([mb, mk], axis=2), jnp.concatenate([_bd(uloc), _bd(vp)], axis=1), 2, 1)
    reff = rd - _pair_nn(mb, wmat)

    state = s_ref[...]
    y = _merge_pairs(yloc + _bdot(reff, state, 2, 2))
    s_ref[...] = _bdot(state, pmat, 2, 1) + qmat

    inv_n = 1.0 / HEAD_DIM
    mean = _head_sum(y, ones_pair) * inv_n
    yc = y - mean
    var = _head_sum(yc * yc, ones_pair) * inv_n
    yn = yc * lax.rsqrt(var + RWKV_GN_EPS) * w["rwkv_ln_w"][...] + w["rwkv_ln_b"][...]
    bonus = _head_sum(r * k * w["rwkv_r_k"][...], ones_pair) * v
    gate = _zrows(zin_ref, C_GRW, C_GRW + BRANCH_W)
    return (yn + bonus) * _silu(gate)


def _first_segment_rows(width):
    assert SEGS_PER_SEQ == 2
    return lax.broadcasted_iota(jnp.int32, (NSEG, width), 0) % SEGS_PER_SEQ == 0


def _s5_expand(zin_ref, w, xre_ref, xim_ref):
    u = _zrows(zin_ref, C_US5, C_US5 + BRANCH_W)
    u_il = jnp.dot(w["perm"][...], u.astype(BF16), preferred_element_type=F32).astype(BF16)
    for blk in range(S5_BLOCKS):
        bu = jnp.dot(u_il[:, blk * S5_BLK_IN:(blk + 1) * S5_BLK_IN], w["s5_b"][blk],
                     preferred_element_type=F32)
        xre_ref[:, blk * S5_BLK_ST:(blk + 1) * S5_BLK_ST] = bu[:, :S5_BLK_ST]
        xim_ref[:, blk * S5_BLK_ST:(blk + 1) * S5_BLK_ST] = bu[:, S5_BLK_ST:]


def _segment_scan_steps(w, xre_ref, xim_ref, h_ref, cp_ref):
    state = {}

    def step(t):
        rws = slice(t * NSEG, (t + 1) * NSEG)
        if t == 0:
            state["lam_re"] = jnp.broadcast_to(w["s5_pow_re"][0:1, :], (NSEG, S5_N))
            state["lam_im"] = jnp.broadcast_to(w["s5_pow_im"][0:1, :], (NSEG, S5_N))
            nr, ni = xre_ref[rws, :], xim_ref[rws, :]
            hs, cs = h_ref[rws, :], cp_ref[rws, :]
        else:
            lam_re, lam_im, sr, si = state["lam_re"], state["lam_im"], state["sr"], state["si"]
            nr = lam_re * sr - lam_im * si + xre_ref[rws, :]
            ni = lam_re * si + lam_im * sr + xim_ref[rws, :]
            xre_ref[rws, :] = nr
            xim_ref[rws, :] = ni
            a_t = cp_ref[rws, :]
            hs = a_t * state["hs"] + h_ref[rws, :]
            cs = a_t * state["cs"]
            h_ref[rws, :] = hs
            cp_ref[rws, :] = cs
        state.update(sr=nr, si=ni, hs=hs, cs=cs)

    return [functools.partial(step, t) for t in range(SEG)]


def _s5_chain(w, xre_ref, xim_ref, st_re_ref, st_im_ref, start_re_ref, start_im_ref):
    last = pl.ds(TL - NSEG, NSEG)
    end_re = xre_ref[last, :]
    end_im = xim_ref[last, :]
    seg_re = w["s5_pow_re"][SEG - 1:SEG, :]
    seg_im = w["s5_pow_im"][SEG - 1:SEG, :]
    first = _first_segment_rows(S5_N)
    c_re = st_re_ref[...]
    c_im = st_im_ref[...]
    e0_re = end_re + seg_re * c_re - seg_im * c_im
    e0_im = end_im + seg_re * c_im + seg_im * c_re
    start_re = jnp.where(first, c_re, pltpu.roll(e0_re, 1, axis=0))
    start_im = jnp.where(first, c_im, pltpu.roll(e0_im, 1, axis=0))
    start_re_ref[...] = start_re
    start_im_ref[...] = start_im
    e1_re = end_re + seg_re * start_re - seg_im * start_im
    e1_im = end_im + seg_re * start_im + seg_im * start_re
    st_re_ref[...] = jnp.where(first, pltpu.roll(e1_re, NSEG - 1, axis=0), 0.0)
    st_im_ref[...] = jnp.where(first, pltpu.roll(e1_im, NSEG - 1, axis=0), 0.0)


def _s5_output(zin_ref, w, xre_ref, xim_ref, start_re_ref, start_im_ref):
    u = _zrows(zin_ref, C_US5, C_US5 + BRANCH_W)
    ys = []
    grp = (TL // BF16_ROWS, BF16_ROWS, S5_BLK_ST)
    twice = BF16_ROWS // NSEG
    for blk in range(S5_BLOCKS):
        cols = slice(blk * S5_BLK_ST, (blk + 1) * S5_BLK_ST)
        s_re = jnp.concatenate([start_re_ref[:, cols]] * twice, axis=0).astype(BF16)[None]
        s_im = jnp.concatenate([start_im_ref[:, cols]] * twice, axis=0).astype(BF16)[None]
        p_re = w["s5_powb_re"][:, cols].reshape(grp)
        p_im = w["s5_powb_im"][:, cols].reshape(grp)
        x_re = xre_ref[:, cols].astype(BF16).reshape(grp) + (p_re * s_re - p_im * s_im)
        x_im = xim_ref[:, cols].astype(BF16).reshape(grp) + (p_re * s_im + p_im * s_re)
        x_ri = jnp.concatenate([x_re.reshape(TL, S5_BLK_ST), x_im.reshape(TL, S5_BLK_ST)], axis=-1)
        ys.append(jnp.dot(x_ri, w["s5_c"][blk], preferred_element_type=F32))
    y_il = jnp.concatenate(ys, axis=-1)
    y = _dot(w["perm_t"][...], y_il) + w["s5_d"][...] * u
    zg = jax.nn.gelu(y)
    zg = zg * _sigmoid(_dot(zg, w["s5_glu_w"][...]) + w["s5_glu_b"][...])
    gate = _zrows(zin_ref, C_GS5, C_GS5 + BRANCH_W)
    return zg * _silu(gate)


def _lru_prep(zin_ref, w, h_ref, cp_ref):
    conv_w = w["lru_conv_w"][...]
    xc = w["lru_conv_b"][...]
    for back in range(CONV_WIDTH):
        tap = CONV_WIDTH - 1 - back
        xc = xc + _zrows(zin_ref, C_XLRU, C_XLRU + BRANCH_W, back) * conv_w[tap:tap + 1, :]
    xc = _dot(w["perm"][...], xc)
    gates = [_dot(xc[:, j * PAIR_W:(j + 1) * PAIR_W], w["lru_gates"][j]) for j in range(PAIRS)]
    r = _sigmoid(jnp.concatenate([g[:, :PAIR_W] for g in gates], axis=-1) + w["lru_ba"][...])
    i = _sigmoid(jnp.concatenate([g[:, PAIR_W:] for g in gates], axis=-1) + w["lru_bx"][...])
    log_a = (-LRU_C * _softplus(-w["lru_lambda"][...])) * r
    a = jnp.exp(log_a)
    one_minus_a2 = 1.0 - a * a
    mult = jnp.where(one_minus_a2 > 0.0, one_minus_a2 * lax.rsqrt(one_minus_a2), 0.0)
    h_ref[...] = mult * (i * xc)
    cp_ref[...] = a


def _lru_chain(h_ref, cp_ref, st_ref, start_ref):
    last = pl.ds(TL - NSEG, NSEG)
    end_h = h_ref[last, :]
    end_cp = cp_ref[last, :]
    first = _first_segment_rows(BRANCH_W)
    carried = st_ref[...]
    start = jnp.where(first, carried, pltpu.roll(end_h + end_cp * carried, 1, axis=0))
    start_ref[...] = start
    st_ref[...] = jnp.where(first, pltpu.roll(end_h + end_cp * start, NSEG - 1, axis=0), 0.0)


def _lru_output(zin_ref, w, h_ref, cp_ref, start_ref):
    start = start_ref[...]
    h3 = h_ref[...].reshape(SEG, NSEG, BRANCH_W) + cp_ref[...].reshape(SEG, NSEG, BRANCH_W) * start[None]
    h = _dot(w["perm_t"][...], h3.reshape(TL, BRANCH_W))
    gate = _zrows(zin_ref, C_GLRU, C_GLRU + BRANCH_W)
    return h * _silu(gate)


class _Rows:
    def __init__(self, ref, start, n=1):
        self.ref, self.start, self.n = ref, start, n

    def __getitem__(self, idx):
        assert idx is Ellipsis
        return self.ref[self.start:self.start + self.n, :]


def _layer_body(names, is_last, *refs):
    n_in = len(names)
    w = dict(zip(names, refs[:n_in]))
    for table, (members, _) in VEC_TABLES.items():
        start = 0
        for name, n_rows in members:
            w[name] = _Rows(w[table], start, n_rows)
            start += n_rows
    out_ref = refs[n_in]
    (zin_ref, s_ref, xre_ref, xim_ref, s5_re_ref, s5_im_ref, s5_start_re_ref, s5_start_im_ref,
     lru_h_ref, lru_cp_ref, lru_st_ref, lru_start_ref) = refs[n_in + 1:]

    @pl.when(pl.program_id(0) == 0)
    def _():
        for b in range(NB):
            zin_ref[b * ZB:b * ZB + CARRY, :] = jnp.zeros((CARRY, D_IN), F32)
        s_ref[...] = jnp.zeros_like(s_ref)
        s5_re_ref[...] = jnp.zeros_like(s5_re_ref)
        s5_im_ref[...] = jnp.zeros_like(s5_im_ref)
        lru_st_ref[...] = jnp.zeros_like(lru_st_ref)

    h = w["h"][...].reshape(TL, D_MODEL)
    xn = _rmsnorm(h, w["norm_g"][...]).astype(BF16)
    for c0, c1 in IN_COL_BLKS:
        z = jnp.dot(xn, w["w_in"][:, c0:c1], preferred_element_type=F32)
        for b in range(NB):
            zin_ref[pl.ds(b * ZB + CARRY, CHUNK), c0:c1] = z[b * CHUNK:(b + 1) * CHUNK]

    _s5_expand(zin_ref, w, xre_ref, xim_ref)
    _lru_prep(zin_ref, w, lru_h_ref, lru_cp_ref)
    scan_steps = _segment_scan_steps(w, xre_ref, xim_ref, lru_h_ref, lru_cp_ref)
    y_rw = _rwkv_group(zin_ref, w, s_ref, scan_steps)
    _s5_chain(w, xre_ref, xim_ref, s5_re_ref, s5_im_ref, s5_start_re_ref, s5_start_im_ref)
    _lru_chain(lru_h_ref, lru_cp_ref, lru_st_ref, lru_start_ref)
    y_s5 = _s5_output(zin_ref, w, xre_ref, xim_ref, s5_start_re_ref, s5_start_im_ref)
    y_lru = _lru_output(zin_ref, w, lru_h_ref, lru_cp_ref, lru_start_ref)
    for b in range(NB):
        zin_ref[b * ZB:b * ZB + CARRY, :] = zin_ref[b * ZB + CHUNK:(b + 1) * ZB, :]

    y_mix = jnp.concatenate([y_rw.astype(BF16), y_s5.astype(BF16), y_lru.astype(BF16)], axis=-1)
    h = h + jnp.dot(y_mix, w["w_out"][...], preferred_element_type=F32)
    e = _rmsnorm(_dot(w["p"][...].reshape(TL, D_PLE), w["ple_w"][...]), w["ple_norm_g"][...])
    h = h + e * _sigmoid(_dot(h, w["ple_gate_w"][...]))
    if is_last:
        h = _rmsnorm(h, w["final_norm_g"][...])
    out_ref[...] = h.reshape(NB, CHUNK, D_MODEL)


def _block_diag(blocks):
    n, a, b = blocks.shape[-3:]
    eye = jnp.eye(n, dtype=blocks.dtype)
    return jnp.einsum("...nab,nm->...namb", blocks, eye).reshape(blocks.shape[:-3] + (n * a, n * b))


def _constants():
    nat = np.arange(TL)
    interleaved_row = (nat % SEG) * NSEG + nat // SEG
    perm = np.zeros((TL, TL), np.float32)
    perm[interleaved_row, nat] = 1.0
    t = nat[:, None]
    s = nat[None, :]
    tri = ((t // CHUNK) == (s // CHUNK)) & (s <= t)
    lane = np.arange(PAIR_W)
    ones_pair = (lane[:, None] // HEAD_DIM) == (lane[None, :] // HEAD_DIM)
    as_bf16 = lambda m: jnp.asarray(m.astype(np.float32), dtype=BF16)
    return as_bf16(perm), as_bf16(perm.T), as_bf16(tri), as_bf16(ones_pair)


def _s5_params(a_re, a_im, log_dt, b_re, b_im, c_re, c_im):
    depth = a_re.shape[0]
    a_re, a_im = a_re.astype(F32), a_im.astype(F32)
    dt = jnp.exp(log_dt.astype(F32))[..., None]
    mag = jnp.exp(a_re * dt)
    lr, li = mag * jnp.cos(a_im * dt), mag * jnp.sin(a_im * dt)
    den = a_re * a_re + a_im * a_im
    f_re = ((lr - 1.0) * a_re + li * a_im) / den
    f_im = (li * a_re - (lr - 1.0) * a_im) / den
    b_re, b_im = b_re.astype(F32), b_im.astype(F32)
    bb_re = f_re[..., None] * b_re - f_im[..., None] * b_im
    bb_im = f_re[..., None] * b_im + f_im[..., None] * b_re
    steps = jnp.arange(1, SEG + 1, dtype=F32)[:, None, None]
    pmag = jnp.exp((a_re * dt)[:, None] * steps)
    ang = (a_im * dt)[:, None] * steps
    pow_re = (pmag * jnp.cos(ang)).reshape(depth, SEG, S5_N)
    pow_im = (pmag * jnp.sin(ang)).reshape(depth, SEG, S5_N)
    gpb = S5_GROUPS // S5_BLOCKS

    def blocks(x, rows, cols):
        return _block_diag(x.reshape(depth, S5_BLOCKS, gpb, rows, cols))

    s5_b = jnp.concatenate([blocks(jnp.swapaxes(bb_re, 2, 3), S5_GROUP, S5_STATE),
                            blocks(jnp.swapaxes(bb_im, 2, 3), S5_GROUP, S5_STATE)], axis=-1).astype(BF16)
    s5_c = jnp.concatenate([blocks(jnp.swapaxes(c_re.astype(F32), 2, 3), S5_STATE, S5_GROUP),
                            -blocks(jnp.swapaxes(c_im.astype(F32), 2, 3), S5_STATE, S5_GROUP)], axis=-2).astype(BF16)
    return s5_b, s5_c, pow_re, pow_im


def _layer_call(h, p, layer, stacked, shared, is_last):
    bsz, seq, _ = h.shape
    assert bsz == NB and seq % CHUNK == 0
    names = ["h", "p"] + list(stacked.keys()) + list(shared.keys())
    arrays = [h, p] + list(stacked.values()) + list(shared.values())
    in_specs = [
        pl.BlockSpec((NB, CHUNK, D_MODEL), lambda j: (0, j, 0)),
        pl.BlockSpec((None, NB, CHUNK, D_PLE), lambda j: (layer, 0, j, 0)),
    ]
    for arr in stacked.values():
        index = (layer,) + (0,) * (arr.ndim - 1)
        in_specs.append(pl.BlockSpec((None,) + arr.shape[1:], lambda j, index=index: index,
                                     pipeline_mode=pl.Buffered(1)))
    for arr in shared.values():
        index = (0,) * arr.ndim
        in_specs.append(pl.BlockSpec(arr.shape, lambda j, index=index: index,
                                     pipeline_mode=pl.Buffered(1)))
    scratch = [
        pltpu.VMEM((NB * ZB, D_IN), F32),
        pltpu.VMEM((GROUPS * NB, GRP_W, GRP_W), F32),
        pltpu.VMEM((TL, S5_N), F32),
        pltpu.VMEM((TL, S5_N), F32),
        pltpu.VMEM((NSEG, S5_N), F32),
        pltpu.VMEM((NSEG, S5_N), F32),
        pltpu.VMEM((NSEG, S5_N), F32),
        pltpu.VMEM((NSEG, S5_N), F32),
        pltpu.VMEM((TL, BRANCH_W), F32),
        pltpu.VMEM((TL, BRANCH_W), F32),
        pltpu.VMEM((NSEG, BRANCH_W), F32),
        pltpu.VMEM((NSEG, BRANCH_W), F32),
    ]
    return pl.pallas_call(
        functools.partial(_layer_body, tuple(names), is_last),
        grid=(seq // CHUNK,),
        in_specs=in_specs,
        out_specs=pl.BlockSpec((NB, CHUNK, D_MODEL), lambda j: (0, j, 0)),
        out_shape=jax.ShapeDtypeStruct(h.shape, F32),
        scratch_shapes=scratch,
        compiler_params=pltpu.CompilerParams(
            dimension_semantics=("arbitrary",),
            vmem_limit_bytes=VMEM_LIMIT),
        name="hybrid_layer",
    )(*arrays)


def kernel(x, p, norm_g, w_in, rwkv_mu, rwkv_w0, rwkv_w2, rwkv_a0, rwkv_a2, rwkv_k_k, rwkv_k_a, rwkv_r_k, rwkv_ln_w, rwkv_ln_b, s5_a_re, s5_a_im, s5_log_dt, s5_b_re, s5_b_im, s5_c_re, s5_c_im, s5_d, s5_glu_w, s5_glu_b, lru_conv_w, lru_conv_b, lru_wa, lru_ba, lru_wx, lru_bx, lru_lambda, w_out, ple_w, ple_norm_g, ple_gate_w, final_norm_g):
    depth = w_in.shape[0]
    perm, perm_t, tri, ones_pair = _constants()
    shared = {"ones_pair": ones_pair, "chunk_tri": tri, "perm": perm, "perm_t": perm_t}

    s5_b, s5_c, pow_re, pow_im = _s5_params(s5_a_re, s5_a_im, s5_log_dt, s5_b_re, s5_b_im, s5_c_re, s5_c_im)
    zero = jnp.zeros((depth, LORA, BRANCH_W), F32)
    lora = jnp.concatenate([jnp.concatenate([rwkv_w2, zero], axis=2),
                            jnp.concatenate([zero, rwkv_a2], axis=2)], axis=1)
    pair_blocks = lambda m: _block_diag(m.reshape(depth, PAIRS, 2, LRU_BLOCK_DIM, LRU_BLOCK_DIM))
    vectors = {
        "rwkv_w0": rwkv_w0, "rwkv_a0": rwkv_a0, "rwkv_k_k": rwkv_k_k, "rwkv_k_a": rwkv_k_a,
        "rwkv_r_k": rwkv_r_k.reshape(depth, BRANCH_W), "rwkv_ln_w": rwkv_ln_w, "rwkv_ln_b": rwkv_ln_b,
        "s5_d": s5_d, "s5_glu_b": s5_glu_b, "lru_conv_b": lru_conv_b, "lru_ba": lru_ba, "lru_bx": lru_bx,
        "lru_lambda": lru_lambda, "lru_conv_w": lru_conv_w,
        "norm_g": norm_g, "ple_norm_g": ple_norm_g,
        "final_norm_g": jnp.broadcast_to(final_norm_g, (depth, D_MODEL)),
    }
    stacked = {
        table: jnp.concatenate([vectors[name].astype(F32).reshape(depth, rows, width) for name, rows in members], axis=1)
        for table, (members, width) in VEC_TABLES.items()
    }
    stacked.update({
        "rwkv_mu": rwkv_mu.astype(F32)[:, None, :],
        "w_in": w_in.astype(BF16),
        "rwkv_lora": lora.astype(BF16),
        "s5_b": s5_b,
        "s5_c": s5_c,
        "s5_pow_re": pow_re,
        "s5_pow_im": pow_im,
        "s5_powb_re": jnp.repeat(pow_re, NSEG, axis=1).astype(BF16),
        "s5_powb_im": jnp.repeat(pow_im, NSEG, axis=1).astype(BF16),
        "s5_glu_w": s5_glu_w.astype(BF16),
        "lru_gates": jnp.concatenate([pair_blocks(lru_wa), pair_blocks(lru_wx)], axis=-1).astype(BF16),
        "w_out": w_out.astype(BF16),
        "ple_w": ple_w.astype(BF16),
        "ple_gate_w": ple_gate_w.astype(BF16),
    })
    h = x.astype(F32)
    p = p.astype(F32)
    for i in range(depth):
        h = _layer_call(h, p, i, stacked, shared, is_last=(i == depth - 1))
    return h.astype(x.dtype)
```

```python
import functools
import math

import jax
import jax.numpy as jnp
import numpy as np
from jax import lax
from jax.experimental import pallas as pl
from jax.experimental.pallas import tpu as pltpu

F32 = jnp.float32
BF16 = jnp.bfloat16

D_MODEL = 1024
D_PLE = 256
BRANCH_W = 512
RWKV_HEADS = 8
HEAD_DIM = 64
PAIR_W = 2 * HEAD_DIM
PAIRS = RWKV_HEADS // 2
GRP_HEADS = 2
GRP_W = GRP_HEADS * HEAD_DIM
GROUPS = RWKV_HEADS // GRP_HEADS
LORA = 64
RWKV_GN_EPS = 64e-5
S5_GROUPS = 32
S5_GROUP = 16
S5_STATE = 64
S5_N = S5_GROUPS * S5_STATE
S5_BLOCKS = 4
S5_BLK_IN = BRANCH_W // S5_BLOCKS
S5_BLK_ST = S5_N // S5_BLOCKS
LRU_BLOCKS = 8
LRU_BLOCK_DIM = 64
CONV_WIDTH = 4
LRU_C = 8.0
NORM_EPS = 1e-6
LOG2_E = math.log2(math.e)
RWKV_SHIFT_W = 3 * BRANCH_W + 2 * LORA
C_GRW = RWKV_SHIFT_W
C_US5 = C_GRW + BRANCH_W
C_GS5 = C_US5 + BRANCH_W
C_XLRU = C_GS5 + BRANCH_W
C_GLRU = C_XLRU + BRANCH_W
D_IN = C_GLRU + BRANCH_W

NB = 4
CHUNK = 64
TL = NB * CHUNK
NCH = NB
NSEG = 8
SEG = TL // NSEG
SEGS_PER_SEQ = NSEG // NB
CARRY = 8
ZB = CARRY + CHUNK
BF16_ROWS = 16
IN_COL_BLKS = ((0, 1280), (1280, 2560), (2560, D_IN))
SCAN_LANES = 1024
SCAN_UNROLL = SEG
VMEM_LIMIT = 56 * 1024 * 1024

VEC_TABLES = {
    "vec512": ((("rwkv_w0", 1), ("rwkv_a0", 1), ("rwkv_k_k", 1), ("rwkv_k_a", 1), ("rwkv_r_k", 1),
                ("rwkv_ln_w", 1), ("rwkv_ln_b", 1), ("s5_d", 1), ("s5_glu_b", 1), ("lru_conv_b", 1),
                ("lru_ba", 1), ("lru_bx", 1), ("lru_lambda", 1), ("lru_conv_w", CONV_WIDTH)), BRANCH_W),
    "vec1024": ((("norm_g", 1), ("ple_norm_g", 1), ("final_norm_g", 1)), D_MODEL),
}


def _dot(a, b):
    return jnp.dot(a.astype(BF16), b.astype(BF16), preferred_element_type=F32)


def _bdot(a, b, lhs_c, rhs_c):
    dims = (((lhs_c,), (rhs_c,)), ((0,), (0,)))
    return lax.dot_general(a.astype(BF16), b.astype(BF16), dims, preferred_element_type=F32)


def _hi_lo(x):
    hi = x.astype(BF16)
    lo = (x - hi.astype(F32)).astype(BF16)
    return hi, lo


def _dot_hilo(m, x):
    hi, lo = _hi_lo(x)
    return (jnp.dot(m, hi, preferred_element_type=F32)
            + jnp.dot(m, lo, preferred_element_type=F32))


def _sigmoid(x):
    return 0.5 * jnp.tanh(0.5 * x) + 0.5


def _silu(x):
    return x * _sigmoid(x)


def _gelu_tanh(x):
    c = math.sqrt(2.0 / math.pi)
    inner = x * (c + (c * 0.044715) * (x * x))
    return x * (0.5 * jnp.tanh(inner) + 0.5)


def _softplus(x):
    return jnp.maximum(x, 0.0) + jnp.log(1.0 + jnp.exp(-jnp.abs(x)))


def _rmsnorm(x, g):
    return x * lax.rsqrt(jnp.mean(x * x, axis=-1, keepdims=True) + NORM_EPS) * g


def _split_pairs(x):
    x3 = x.reshape(NCH, CHUNK, BRANCH_W)
    return jnp.concatenate([x3[:, :, j * GRP_W:(j + 1) * GRP_W] for j in range(GROUPS)], axis=0)


def _merge_pairs(y):
    return jnp.concatenate([y[j * NCH:(j + 1) * NCH].reshape(TL, GRP_W) for j in range(GROUPS)], axis=-1)


def _bd(x):
    x = x.astype(BF16)
    head = lax.broadcasted_iota(jnp.int32, x.shape, 2) // HEAD_DIM
    zero = jnp.zeros_like(x)
    return jnp.concatenate([jnp.where(head == g, x, zero) for g in range(GRP_HEADS)], axis=1)


def _pair_nn(x, y):
    return _bdot(x, _bd(y), 2, 1)


def _head_sum(x, ones_pair):
    return jnp.concatenate([_dot(x[:, j * PAIR_W:(j + 1) * PAIR_W], ones_pair) for j in range(PAIRS)], axis=-1)


def _unit_lower_inverse_minus_eye(low, t_idx, s_idx):
    e = None
    size = 1
    while size < CHUNK:
        sibling = ((t_idx // (2 * size)) == (s_idx // (2 * size))) & ((t_idx // size) != (s_idx // size))
        off = jnp.where(sibling[None], low, 0.0)
        if e is None:
            e = -off
        else:
            a = off + _pair_nn(e, off)
            e = e - a - _pair_nn(a, e)
        size *= 2
    return e


def _zrows(zin_ref, lo, hi, back=0):
    return jnp.concatenate([zin_ref[pl.ds(b * ZB + CARRY - back, CHUNK), lo:hi] for b in range(NB)], axis=0)


def _rwkv_group(zin_ref, w, s_ref):
    mu = w["rwkv_mu"][...]

    def shifted(lo, hi):
        z = _zrows(zin_ref, lo, hi)
        zp = _zrows(zin_ref, lo, hi, back=1)
        return z + mu[:, lo:hi] * (zp - z)

    r = shifted(0, BRANCH_W)
    k = shifted(BRANCH_W, 2 * BRANCH_W)
    v = shifted(2 * BRANCH_W, 3 * BRANCH_W)
    lora_in = shifted(3 * BRANCH_W, RWKV_SHIFT_W)
    lane = lax.broadcasted_iota(jnp.int32, lora_in.shape, 1)
    lora_in = jnp.where(lane < LORA, jnp.tanh(lora_in), lora_in)
    lora = _dot(lora_in, w["rwkv_lora"][...])
    logd = -(math.exp(-0.5) * LOG2_E) * _sigmoid(w["rwkv_w0"][...] + lora[:, :BRANCH_W])
    a = _sigmoid(w["rwkv_a0"][...] + lora[:, BRANCH_W:])

    ones_pair = w["ones_pair"][...]
    kk = k * w["rwkv_k_k"][...]
    kk = kk * lax.rsqrt(jnp.maximum(_head_sum(kk * kk, ones_pair), 1e-24))
    k_a = w["rwkv_k_a"][...]
    k = k * ((1.0 - k_a) + a * k_a)
    b = kk * a

    g_inc = _dot_hilo(w["chunk_tri"][...], logd)
    e_inc = jnp.exp2(g_inc)
    e_neg = jnp.exp2(-g_inc)
    kkd = _split_pairs(kk * jnp.exp2(g_inc - logd))
    rd = _split_pairs(r * e_inc)
    bi = _split_pairs(b * e_neg)
    ki = _split_pairs(k * e_neg)
    vp = _split_pairs(v)
    gamma_c = _split_pairs(e_inc)[:, CHUNK - 1:CHUNK, :]
    bd = bi * gamma_c
    kd = ki * gamma_c

    t_idx = lax.broadcasted_iota(jnp.int32, (CHUNK, GRP_W), 0)
    s_idx = lax.broadcasted_iota(jnp.int32, (CHUNK, GRP_W), 1) % HEAD_DIM
    strict = (s_idx < t_idx)[None]
    incl = (s_idx <= t_idx)[None]
    row = lax.broadcasted_iota(jnp.int32, (GRP_W, GRP_W), 0)
    col = lax.broadcasted_iota(jnp.int32, (GRP_W, GRP_W), 1)
    same_head = ((row // HEAD_DIM) == (col // HEAD_DIM))[None]
    eye = (row == col)[None]

    lhs = jnp.concatenate([kkd, rd], axis=1)
    a_b = _bdot(lhs, _bd(bi), 2, 2)
    a_k = _bdot(lhs, _bd(ki), 2, 2)
    lb = jnp.where(strict, a_b[:, :CHUNK], 0.0)
    mb = jnp.where(incl, a_b[:, CHUNK:], 0.0)
    lk = jnp.where(strict, a_k[:, :CHUNK], 0.0)
    mk = jnp.where(incl, a_k[:, CHUNK:], 0.0)

    e = _unit_lower_inverse_minus_eye(lb, t_idx, s_idx)
    lkv = _pair_nn(lk, vp)
    ex = _bdot(e, jnp.concatenate([_bd(kkd), _bd(lkv)], axis=2), 2, 1)
    wmat = kkd + ex[:, :, :GRP_W]
    uloc = -(lkv + ex[:, :, GRP_W:])
    pmat = jnp.where(eye, gamma_c, 0.0) - jnp.where(same_head, _bdot(wmat, bd, 1, 1), 0.0)
    qmat = jnp.where(same_head, _bdot(jnp.concatenate([uloc, vp], axis=1),
                                      jnp.concatenate([bd, kd], axis=1), 1, 1), 0.0)
    yloc = _bdot(jnp.concatenate([mb, mk], axis=2), jnp.concatenate([_bd(uloc), _bd(vp)], axis=1), 2, 1)
    reff = rd - _pair_nn(mb, wmat)

    state = s_ref[...]
    y = _merge_pairs(yloc + _bdot(reff, state, 2, 2))
    s_ref[...] = _bdot(state, pmat, 2, 1) + qmat

    inv_n = 1.0 / HEAD_DIM
    mean = _head_sum(y, ones_pair) * inv_n
    yc = y - mean
    var = _head_sum(yc * yc, ones_pair) * inv_n
    yn = yc * lax.rsqrt(var + RWKV_GN_EPS) * w["rwkv_ln_w"][...] + w["rwkv_ln_b"][...]
    bonus = _head_sum(r * k * w["rwkv_r_k"][...], ones_pair) * v
    gate = _zrows(zin_ref, C_GRW, C_GRW + BRANCH_W)
    return (yn + bonus) * _silu(gate)


def _first_segment_rows(width):
    assert SEGS_PER_SEQ == 2
    return lax.broadcasted_iota(jnp.int32, (NSEG, width), 0) % SEGS_PER_SEQ == 0


def _s5_expand(zin_ref, w, xre_ref, xim_ref):
    u = _zrows(zin_ref, C_US5, C_US5 + BRANCH_W)
    u_il = jnp.dot(w["perm"][...], u.astype(BF16), preferred_element_type=F32).astype(BF16)
    for blk in range(S5_BLOCKS):
        bu = jnp.dot(u_il[:, blk * S5_BLK_IN:(blk + 1) * S5_BLK_IN], w["s5_b"][blk],
                     preferred_element_type=F32)
        xre_ref[:, blk * S5_BLK_ST:(blk + 1) * S5_BLK_ST] = bu[:, :S5_BLK_ST]
        xim_ref[:, blk * S5_BLK_ST:(blk + 1) * S5_BLK_ST] = bu[:, S5_BLK_ST:]


def _segment_scans(w, xre_ref, xim_ref, h_ref, cp_ref):
    n_loops = S5_N // SCAN_LANES
    for lb in range(n_loops):
        cols = slice(lb * SCAN_LANES, (lb + 1) * SCAN_LANES)
        lam_re = jnp.broadcast_to(w["s5_pow_re"][0:1, cols], (NSEG, SCAN_LANES))
        lam_im = jnp.broadcast_to(w["s5_pow_im"][0:1, cols], (NSEG, SCAN_LANES))
        with_lru = lb == n_loops - 1

        def step(t, carry, cols=cols, lam_re=lam_re, lam_im=lam_im, with_lru=with_lru):
            rws = pl.ds(pl.multiple_of(t * NSEG, NSEG), NSEG)
            sr, si = carry[:2]
            nr = lam_re * sr - lam_im * si + xre_ref[rws, cols]
            ni = lam_re * si + lam_im * sr + xim_ref[rws, cols]
            xre_ref[rws, cols] = nr
            xim_ref[rws, cols] = ni
            if not with_lru:
                return nr, ni
            hs, cs = carry[2:]
            a_t = cp_ref[rws, :]
            hs = a_t * hs + h_ref[rws, :]
            cs = a_t * cs
            h_ref[rws, :] = hs
            cp_ref[rws, :] = cs
            return nr, ni, hs, cs

        zero = jnp.zeros((NSEG, SCAN_LANES), F32)
        init = (zero, zero)
        if with_lru:
            init += (jnp.zeros((NSEG, BRANCH_W), F32), jnp.ones((NSEG, BRANCH_W), F32))
        lax.fori_loop(0, SEG, step, init, unroll=SCAN_UNROLL)


def _s5_chain(w, xre_ref, xim_ref, st_re_ref, st_im_ref, start_re_ref, start_im_ref):
    last = pl.ds(TL - NSEG, NSEG)
    end_re = xre_ref[last, :]
    end_im = xim_ref[last, :]
    seg_re = w["s5_pow_re"][SEG - 1:SEG, :]
    seg_im = w["s5_pow_im"][SEG - 1:SEG, :]
    first = _first_segment_rows(S5_N)
    c_re = st_re_ref[...]
    c_im = st_im_ref[...]
    e0_re = end_re + seg_re * c_re - seg_im * c_im
    e0_im = end_im + seg_re * c_im + seg_im * c_re
    start_re = jnp.where(first, c_re, pltpu.roll(e0_re, 1, axis=0))
    start_im = jnp.where(first, c_im, pltpu.roll(e0_im, 1, axis=0))
    start_re_ref[...] = start_re
    start_im_ref[...] = start_im
    e1_re = end_re + seg_re * start_re - seg_im * start_im
    e1_im = end_im + seg_re * start_im + seg_im * start_re
    st_re_ref[...] = jnp.where(first, pltpu.roll(e1_re, NSEG - 1, axis=0), 0.0)
    st_im_ref[...] = jnp.where(first, pltpu.roll(e1_im, NSEG - 1, axis=0), 0.0)


def _s5_output(zin_ref, w, xre_ref, xim_ref, start_re_ref, start_im_ref):
    u = _zrows(zin_ref, C_US5, C_US5 + BRANCH_W)
    ys = []
    grp = (TL // BF16_ROWS, BF16_ROWS, S5_BLK_ST)
    twice = BF16_ROWS // NSEG
    for blk in range(S5_BLOCKS):
        cols = slice(blk * S5_BLK_ST, (blk + 1) * S5_BLK_ST)
        s_re = jnp.concatenate([start_re_ref[:, cols]] * twice, axis=0).astype(BF16)[None]
        s_im = jnp.concatenate([start_im_ref[:, cols]] * twice, axis=0).astype(BF16)[None]
        p_re = w["s5_powb_re"][:, cols].reshape(grp)
        p_im = w["s5_powb_im"][:, cols].reshape(grp)
        x_re = xre_ref[:, cols].astype(BF16).reshape(grp) + (p_re * s_re - p_im * s_im)
        x_im = xim_ref[:, cols].astype(BF16).reshape(grp) + (p_re * s_im + p_im * s_re)
        x_ri = jnp.concatenate([x_re.reshape(TL, S5_BLK_ST), x_im.reshape(TL, S5_BLK_ST)], axis=-1)
        ys.append(jnp.dot(x_ri, w["s5_c"][blk], preferred_element_type=F32))
    y_il = jnp.concatenate(ys, axis=-1)
    y = _dot(w["perm_t"][...], y_il) + w["s5_d"][...] * u
    zg = _gelu_tanh(y)
    zg = zg * _sigmoid(_dot(zg, w["s5_glu_w"][...]) + w["s5_glu_b"][...])
    gate = _zrows(zin_ref, C_GS5, C_GS5 + BRANCH_W)
    return zg * _silu(gate)


def _lru_prep(zin_ref, w, h_ref, cp_ref):
    conv_w = w["lru_conv_w"][...]
    xc = w["lru_conv_b"][...]
    for back in range(CONV_WIDTH):
        tap = CONV_WIDTH - 1 - back
        xc = xc + _zrows(zin_ref, C_XLRU, C_XLRU + BRANCH_W, back) * conv_w[tap:tap + 1, :]
    xc = _dot(w["perm"][...], xc)
    gates = [_dot(xc[:, j * PAIR_W:(j + 1) * PAIR_W], w["lru_gates"][j]) for j in range(PAIRS)]
    r = _sigmoid(jnp.concatenate([g[:, :PAIR_W] for g in gates], axis=-1) + w["lru_ba"][...])
    i = _sigmoid(jnp.concatenate([g[:, PAIR_W:] for g in gates], axis=-1) + w["lru_bx"][...])
    log2_a = (-LRU_C * LOG2_E * _softplus(-w["lru_lambda"][...])) * r
    a = jnp.exp2(log2_a)
    one_minus_a2 = 1.0 - a * a
    mult = jnp.where(one_minus_a2 > 0.0, one_minus_a2 * lax.rsqrt(one_minus_a2), 0.0)
    h_ref[...] = mult * (i * xc)
    cp_ref[...] = a


def _lru_chain(h_ref, cp_ref, st_ref, start_ref):
    last = pl.ds(TL - NSEG, NSEG)
    end_h = h_ref[last, :]
    end_cp = cp_ref[last, :]
    first = _first_segment_rows(BRANCH_W)
    carried = st_ref[...]
    start = jnp.where(first, carried, pltpu.roll(end_h + end_cp * carried, 1, axis=0))
    start_ref[...] = start
    st_ref[...] = jnp.where(first, pltpu.roll(end_h + end_cp * start, NSEG - 1, axis=0), 0.0)


def _lru_output(zin_ref, w, h_ref, cp_ref, start_ref):
    start = start_ref[...]
    h3 = h_ref[...].reshape(SEG, NSEG, BRANCH_W) + cp_ref[...].reshape(SEG, NSEG, BRANCH_W) * start[None]
    h = _dot(w["perm_t"][...], h3.reshape(TL, BRANCH_W))
    gate = _zrows(zin_ref, C_GLRU, C_GLRU + BRANCH_W)
    return h * _silu(gate)


class _Rows:
    def __init__(self, ref, start, n=1):
        self.ref, self.start, self.n = ref, start, n

    def __getitem__(self, idx):
        assert idx is Ellipsis
        return self.ref[self.start:self.start + self.n, :]


def _layer_body(names, is_last, *refs):
    n_in = len(names)
    w = dict(zip(names, refs[:n_in]))
    for table, (members, _) in VEC_TABLES.items():
        start = 0
        for name, n_rows in members:
            w[name] = _Rows(w[table], start, n_rows)
            start += n_rows
    out_ref = refs[n_in]
    (zin_ref, s_ref, xre_ref, xim_ref, s5_re_ref, s5_im_ref, s5_start_re_ref, s5_start_im_ref,
     lru_h_ref, lru_cp_ref, lru_st_ref, lru_start_ref) = refs[n_in + 1:]

    @pl.when(pl.program_id(0) == 0)
    def _():
        for b in range(NB):
            zin_ref[b * ZB:b * ZB + CARRY, :] = jnp.zeros((CARRY, D_IN), F32)
        s_ref[...] = jnp.zeros_like(s_ref)
        s5_re_ref[...] = jnp.zeros_like(s5_re_ref)
        s5_im_ref[...] = jnp.zeros_like(s5_im_ref)
        lru_st_ref[...] = jnp.zeros_like(lru_st_ref)

    h = w["h"][...].reshape(TL, D_MODEL)
    xn = _rmsnorm(h, w["norm_g"][...]).astype(BF16)
    for c0, c1 in IN_COL_BLKS:
        z = jnp.dot(xn, w["w_in"][:, c0:c1], preferred_element_type=F32)
        for b in range(NB):
            zin_ref[pl.ds(b * ZB + CARRY, CHUNK), c0:c1] = z[b * CHUNK:(b + 1) * CHUNK]

    y_rw = _rwkv_group(zin_ref, w, s_ref)
    _s5_expand(zin_ref, w, xre_ref, xim_ref)
    _lru_prep(zin_ref, w, lru_h_ref, lru_cp_ref)
    _segment_scans(w, xre_ref, xim_ref, lru_h_ref, lru_cp_ref)
    _s5_chain(w, xre_ref, xim_ref, s5_re_ref, s5_im_ref, s5_start_re_ref, s5_start_im_ref)
    _lru_chain(lru_h_ref, lru_cp_ref, lru_st_ref, lru_start_ref)
    y_s5 = _s5_output(zin_ref, w, xre_ref, xim_ref, s5_start_re_ref, s5_start_im_ref)
    y_lru = _lru_output(zin_ref, w, lru_h_ref, lru_cp_ref, lru_start_ref)
    for b in range(NB):
        zin_ref[b * ZB:b * ZB + CARRY, :] = zin_ref[b * ZB + CHUNK:(b + 1) * ZB, :]

    y_mix = jnp.concatenate([y_rw.astype(BF16), y_s5.astype(BF16), y_lru.astype(BF16)], axis=-1)
    h = h + jnp.dot(y_mix, w["w_out"][...], preferred_element_type=F32)
    e = _rmsnorm(_dot(w["p"][...].reshape(TL, D_PLE), w["ple_w"][...]), w["ple_norm_g"][...])
    h = h + e * _sigmoid(_dot(h, w["ple_gate_w"][...]))
    if is_last:
        h = _rmsnorm(h, w["final_norm_g"][...])
    out_ref[...] = h.reshape(NB, CHUNK, D_MODEL)


def _block_diag(blocks):
    n, a, b = blocks.shape[-3:]
    eye = jnp.eye(n, dtype=blocks.dtype)
    return jnp.einsum("...nab,nm->...namb", blocks, eye).reshape(blocks.shape[:-3] + (n * a, n * b))


def _constants():
    nat = np.arange(TL)
    interleaved_row = (nat % SEG) * NSEG + nat // SEG
    perm = np.zeros((TL, TL), np.float32)
    perm[interleaved_row, nat] = 1.0
    t = nat[:, None]
    s = nat[None, :]
    tri = ((t // CHUNK) == (s // CHUNK)) & (s <= t)
    lane = np.arange(PAIR_W)
    ones_pair = (lane[:, None] // HEAD_DIM) == (lane[None, :] // HEAD_DIM)
    as_bf16 = lambda m: jnp.asarray(m.astype(np.float32), dtype=BF16)
    return as_bf16(perm), as_bf16(perm.T), as_bf16(tri), as_bf16(ones_pair)


def _s5_params(a_re, a_im, log_dt, b_re, b_im, c_re, c_im):
    depth = a_re.shape[0]
    a_re, a_im = a_re.astype(F32), a_im.astype(F32)
    dt = jnp.exp(log_dt.astype(F32))[..., None]
    mag = jnp.exp(a_re * dt)
    lr, li = mag * jnp.cos(a_im * dt), mag * jnp.sin(a_im * dt)
    den = a_re * a_re + a_im * a_im
    f_re = ((lr - 1.0) * a_re + li * a_im) / den
    f_im = (li * a_re - (lr - 1.0) * a_im) / den
    b_re, b_im = b_re.astype(F32), b_im.astype(F32)
    bb_re = f_re[..., None] * b_re - f_im[..., None] * b_im
    bb_im = f_re[..., None] * b_im + f_im[..., None] * b_re
    steps = jnp.arange(1, SEG + 1, dtype=F32)[:, None, None]
    pmag = jnp.exp((a_re * dt)[:, None] * steps)
    ang = (a_im * dt)[:, None] * steps
    pow_re = (pmag * jnp.cos(ang)).reshape(depth, SEG, S5_N)
    pow_im = (pmag * jnp.sin(ang)).reshape(depth, SEG, S5_N)
    gpb = S5_GROUPS // S5_BLOCKS

    def blocks(x, rows, cols):
        return _block_diag(x.reshape(depth, S5_BLOCKS, gpb, rows, cols))

    s5_b = jnp.concatenate([blocks(jnp.swapaxes(bb_re, 2, 3), S5_GROUP, S5_STATE),
                            blocks(jnp.swapaxes(bb_im, 2, 3), S5_GROUP, S5_STATE)], axis=-1).astype(BF16)
    s5_c = jnp.concatenate([blocks(jnp.swapaxes(c_re.astype(F32), 2, 3), S5_STATE, S5_GROUP),
                            -blocks(jnp.swapaxes(c_im.astype(F32), 2, 3), S5_STATE, S5_GROUP)], axis=-2).astype(BF16)
    return s5_b, s5_c, pow_re, pow_im


def _layer_call(h, p, layer, stacked, shared, is_last):
    bsz, seq, _ = h.shape
    assert bsz == NB and seq % CHUNK == 0
    names = ["h", "p"] + list(stacked.keys()) + list(shared.keys())
    arrays = [h, p] + list(stacked.values()) + list(shared.values())
    in_specs = [
        pl.BlockSpec((NB, CHUNK, D_MODEL), lambda j: (0, j, 0)),
        pl.BlockSpec((None, NB, CHUNK, D_PLE), lambda j: (layer, 0, j, 0)),
    ]
    for arr in stacked.values():
        index = (layer,) + (0,) * (arr.ndim - 1)
        in_specs.append(pl.BlockSpec((None,) + arr.shape[1:], lambda j, index=index: index,
                                     pipeline_mode=pl.Buffered(1)))
    for arr in shared.values():
        index = (0,) * arr.ndim
        in_specs.append(pl.BlockSpec(arr.shape, lambda j, index=index: index,
                                     pipeline_mode=pl.Buffered(1)))
    scratch = [
        pltpu.VMEM((NB * ZB, D_IN), F32),
        pltpu.VMEM((GROUPS * NB, GRP_W, GRP_W), F32),
        pltpu.VMEM((TL, S5_N), F32),
        pltpu.VMEM((TL, S5_N), F32),
        pltpu.VMEM((NSEG, S5_N), F32),
        pltpu.VMEM((NSEG, S5_N), F32),
        pltpu.VMEM((NSEG, S5_N), F32),
        pltpu.VMEM((NSEG, S5_N), F32),
        pltpu.VMEM((TL, BRANCH_W), F32),
        pltpu.VMEM((TL, BRANCH_W), F32),
        pltpu.VMEM((NSEG, BRANCH_W), F32),
        pltpu.VMEM((NSEG, BRANCH_W), F32),
    ]
    return pl.pallas_call(
        functools.partial(_layer_body, tuple(names), is_last),
        grid=(seq // CHUNK,),
        in_specs=in_specs,
        out_specs=pl.BlockSpec((NB, CHUNK, D_MODEL), lambda j: (0, j, 0)),
        out_shape=jax.ShapeDtypeStruct(h.shape, F32),
        scratch_shapes=scratch,
        compiler_params=pltpu.CompilerParams(
            dimension_semantics=("arbitrary",),
            vmem_limit_bytes=VMEM_LIMIT),
        name="hybrid_layer",
    )(*arrays)


def kernel(x, p, norm_g, w_in, rwkv_mu, rwkv_w0, rwkv_w2, rwkv_a0, rwkv_a2, rwkv_k_k, rwkv_k_a, rwkv_r_k, rwkv_ln_w, rwkv_ln_b, s5_a_re, s5_a_im, s5_log_dt, s5_b_re, s5_b_im, s5_c_re, s5_c_im, s5_d, s5_glu_w, s5_glu_b, lru_conv_w, lru_conv_b, lru_wa, lru_ba, lru_wx, lru_bx, lru_lambda, w_out, ple_w, ple_norm_g, ple_gate_w, final_norm_g):
    depth = w_in.shape[0]
    perm, perm_t, tri, ones_pair = _constants()
    shared = {"ones_pair": ones_pair, "chunk_tri": tri, "perm": perm, "perm_t": perm_t}

    s5_b, s5_c, pow_re, pow_im = _s5_params(s5_a_re, s5_a_im, s5_log_dt, s5_b_re, s5_b_im, s5_c_re, s5_c_im)
    zero = jnp.zeros((depth, LORA, BRANCH_W), F32)
    lora = jnp.concatenate([jnp.concatenate([rwkv_w2, zero], axis=2),
                            jnp.concatenate([zero, rwkv_a2], axis=2)], axis=1)
    pair_blocks = lambda m: _block_diag(m.reshape(depth, PAIRS, 2, LRU_BLOCK_DIM, LRU_BLOCK_DIM))
    vectors = {
        "rwkv_w0": rwkv_w0, "rwkv_a0": rwkv_a0, "rwkv_k_k": rwkv_k_k, "rwkv_k_a": rwkv_k_a,
        "rwkv_r_k": rwkv_r_k.reshape(depth, BRANCH_W), "rwkv_ln_w": rwkv_ln_w, "rwkv_ln_b": rwkv_ln_b,
        "s5_d": s5_d, "s5_glu_b": s5_glu_b, "lru_conv_b": lru_conv_b, "lru_ba": lru_ba, "lru_bx": lru_bx,
        "lru_lambda": lru_lambda, "lru_conv_w": lru_conv_w,
        "norm_g": norm_g, "ple_norm_g": ple_norm_g,
        "final_norm_g": jnp.broadcast_to(final_norm_g, (depth, D_MODEL)),
    }
    stacked = {
        table: jnp.concatenate([vectors[name].astype(F32).reshape(depth, rows, width) for name, rows in members], axis=1)
        for table, (members, width) in VEC_TABLES.items()
    }
    stacked.update({
        "rwkv_mu": rwkv_mu.astype(F32)[:, None, :],
        "w_in": w_in.astype(BF16),
        "rwkv_lora": lora.astype(BF16),
        "s5_b": s5_b,
        "s5_c": s5_c,
        "s5_pow_re": pow_re,
        "s5_pow_im": pow_im,
        "s5_powb_re": jnp.repeat(pow_re, NSEG, axis=1).astype(BF16),
        "s5_powb_im": jnp.repeat(pow_im, NSEG, axis=1).astype(BF16),
        "s5_glu_w": s5_glu_w.astype(BF16),
        "lru_gates": jnp.concatenate([pair_blocks(lru_wa), pair_blocks(lru_wx)], axis=-1).astype(BF16),
        "w_out": w_out.astype(BF16),
        "ple_w": ple_w.astype(BF16),
        "ple_gate_w": ple_gate_w.astype(BF16),
    })
    h = x.astype(F32)
    p = p.astype(F32)
    for i in range(depth):
        h = _layer_call(h, p, i, stacked, shared, is_last=(i == depth - 1))
    return h.astype(x.dtype)
```

```python
import functools
import math

import jax
import jax.numpy as jnp
import numpy as np
from jax import lax
from jax.experimental import pallas as pl
from jax.experimental.pallas import tpu as pltpu

F32 = jnp.float32
BF16 = jnp.bfloat16

D_MODEL = 1024
D_PLE = 256
BRANCH_W = 512
RWKV_HEADS = 8
HEAD_DIM = 64
PAIR_W = 2 * HEAD_DIM
PAIRS = RWKV_HEADS // 2
GRP_HEADS = 2
GRP_W = GRP_HEADS * HEAD_DIM
GROUPS = RWKV_HEADS // GRP_HEADS
LORA = 64
RWKV_GN_EPS = 64e-5
S5_GROUPS = 32
S5_GROUP = 16
S5_STATE = 64
S5_N = S5_GROUPS * S5_STATE
S5_BLOCKS = 4
S5_BLK_IN = BRANCH_W // S5_BLOCKS
S5_BLK_ST = S5_N // S5_BLOCKS
LRU_BLOCKS = 8
LRU_BLOCK_DIM = 64
CONV_WIDTH = 4
LRU_C = 8.0
NORM_EPS = 1e-6
LOG2_E = math.log2(math.e)
RWKV_SHIFT_W = 3 * BRANCH_W + 2 * LORA
C_GRW = RWKV_SHIFT_W
C_US5 = C_GRW + BRANCH_W
C_GS5 = C_US5 + BRANCH_W
C_XLRU = C_GS5 + BRANCH_W
C_GLRU = C_XLRU + BRANCH_W
D_IN = C_GLRU + BRANCH_W

NB = 4
CHUNK = 64
TL = NB * CHUNK
NCH = NB
NSEG = 8
SEG = TL // NSEG
SEGS_PER_SEQ = NSEG // NB
CARRY = 8
ZB = CARRY + CHUNK
BF16_ROWS = 16
IN_COL_BLKS = ((0, 1280), (1280, 2560), (2560, D_IN))
SCAN_LANES = 1024
SCAN_UNROLL = SEG
VMEM_LIMIT = 56 * 1024 * 1024

VEC_TABLES = {
    "vec512": ((("rwkv_w0", 1), ("rwkv_a0", 1), ("rwkv_k_k", 1), ("rwkv_k_a", 1), ("rwkv_r_k", 1),
                ("rwkv_ln_w", 1), ("rwkv_ln_b", 1), ("s5_d", 1), ("s5_glu_b", 1), ("lru_conv_b", 1),
                ("lru_ba", 1), ("lru_bx", 1), ("lru_lambda", 1), ("lru_conv_w", CONV_WIDTH)), BRANCH_W),
    "vec1024": ((("norm_g", 1), ("ple_norm_g", 1), ("final_norm_g", 1)), D_MODEL),
}


def _dot(a, b):
    return jnp.dot(a.astype(BF16), b.astype(BF16), preferred_element_type=F32)


def _bdot(a, b, lhs_c, rhs_c):
    dims = (((lhs_c,), (rhs_c,)), ((0,), (0,)))
    return lax.dot_general(a.astype(BF16), b.astype(BF16), dims, preferred_element_type=F32)


def _hi_lo(x):
    hi = x.astype(BF16)
    lo = (x - hi.astype(F32)).astype(BF16)
    return hi, lo


def _dot_hilo(m, x):
    hi, lo = _hi_lo(x)
    return (jnp.dot(m, hi, preferred_element_type=F32)
            + jnp.dot(m, lo, preferred_element_type=F32))


def _sigmoid(x):
    return 0.5 * jnp.tanh(0.5 * x) + 0.5


def _silu(x):
    return x * _sigmoid(x)


def _gelu_tanh(x):
    c = math.sqrt(2.0 / math.pi)
    inner = x * (c + (c * 0.044715) * (x * x))
    return x * (0.5 * jnp.tanh(inner) + 0.5)


def _softplus(x):
    return jnp.maximum(x, 0.0) + jnp.log(1.0 + jnp.exp(-jnp.abs(x)))


def _rmsnorm(x, g):
    return x * lax.rsqrt(jnp.mean(x * x, axis=-1, keepdims=True) + NORM_EPS) * g


def _split_pairs(x):
    x3 = x.reshape(NCH, CHUNK, BRANCH_W)
    return jnp.concatenate([x3[:, :, j * GRP_W:(j + 1) * GRP_W] for j in range(GROUPS)], axis=0)


def _merge_pairs(y):
    return jnp.concatenate([y[j * NCH:(j + 1) * NCH].reshape(TL, GRP_W) for j in range(GROUPS)], axis=-1)


def _bd(x):
    x = x.astype(BF16)
    head = lax.broadcasted_iota(jnp.int32, x.shape, 2) // HEAD_DIM
    zero = jnp.zeros_like(x)
    return jnp.concatenate([jnp.where(head == g, x, zero) for g in range(GRP_HEADS)], axis=1)


def _pair_nn(x, y):
    return _bdot(x, _bd(y), 2, 1)


def _head_sum(x, ones_pair):
    return jnp.concatenate([_dot(x[:, j * PAIR_W:(j + 1) * PAIR_W], ones_pair) for j in range(PAIRS)], axis=-1)


def _unit_lower_inverse_minus_eye(low, t_idx, s_idx):
    e = None
    size = 1
    while size < CHUNK:
        sibling = ((t_idx // (2 * size)) == (s_idx // (2 * size))) & ((t_idx // size) != (s_idx // size))
        off = jnp.where(sibling[None], low, 0.0)
        if e is None:
            e = -off
        else:
            a = off + _pair_nn(e, off)
            e = e - a - _pair_nn(a, e)
        size *= 2
    return e


def _zrows(zin_ref, lo, hi, back=0):
    return jnp.concatenate([zin_ref[pl.ds(b * ZB + CARRY - back, CHUNK), lo:hi] for b in range(NB)], axis=0)


def _rwkv_group(zin_ref, w, s_ref):
    mu = w["rwkv_mu"][...]

    def shifted(lo, hi):
        z = _zrows(zin_ref, lo, hi)
        zp = _zrows(zin_ref, lo, hi, back=1)
        return z + mu[:, lo:hi] * (zp - z)

    r = shifted(0, BRANCH_W)
    k = shifted(BRANCH_W, 2 * BRANCH_W)
    v = shifted(2 * BRANCH_W, 3 * BRANCH_W)
    lora_in = shifted(3 * BRANCH_W, RWKV_SHIFT_W)
    lane = lax.broadcasted_iota(jnp.int32, lora_in.shape, 1)
    lora_in = jnp.where(lane < LORA, jnp.tanh(lora_in), lora_in)
    lora = _dot(lora_in, w["rwkv_lora"][...])
    logd = -(math.exp(-0.5) * LOG2_E) * _sigmoid(w["rwkv_w0"][...] + lora[:, :BRANCH_W])
    a = _sigmoid(w["rwkv_a0"][...] + lora[:, BRANCH_W:])

    ones_pair = w["ones_pair"][...]
    kk = k * w["rwkv_k_k"][...]
    kk = kk * lax.rsqrt(jnp.maximum(_head_sum(kk * kk, ones_pair), 1e-24))
    k_a = w["rwkv_k_a"][...]
    k = k * ((1.0 - k_a) + a * k_a)
    b = kk * a

    g_inc = _dot_hilo(w["chunk_tri"][...], logd)
    e_inc = jnp.exp2(g_inc)
    e_neg = jnp.exp2(-g_inc)
    kkd = _split_pairs(kk * jnp.exp2(g_inc - logd))
    rd = _split_pairs(r * e_inc)
    bi = _split_pairs(b * e_neg)
    ki = _split_pairs(k * e_neg)
    vp = _split_pairs(v)
    gamma_c = _split_pairs(e_inc)[:, CHUNK - 1:CHUNK, :]
    bd = bi * gamma_c
    kd = ki * gamma_c

    t_idx = lax.broadcasted_iota(jnp.int32, (CHUNK, GRP_W), 0)
    s_idx = lax.broadcasted_iota(jnp.int32, (CHUNK, GRP_W), 1) % HEAD_DIM
    strict = (s_idx < t_idx)[None]
    incl = (s_idx <= t_idx)[None]
    row = lax.broadcasted_iota(jnp.int32, (GRP_W, GRP_W), 0)
    col = lax.broadcasted_iota(jnp.int32, (GRP_W, GRP_W), 1)
    same_head = ((row // HEAD_DIM) == (col // HEAD_DIM))[None]
    eye = (row == col)[None]

    lhs = jnp.concatenate([kkd, rd], axis=1)
    a_bk = _bdot(lhs, jnp.concatenate([_bd(bi), _bd(ki)], axis=1), 2, 2)
    lb = jnp.where(strict, a_bk[:, :CHUNK, :GRP_W], 0.0)
    mb = jnp.where(incl, a_bk[:, CHUNK:, :GRP_W], 0.0)
    lk = jnp.where(strict, a_bk[:, :CHUNK, GRP_W:], 0.0)
    mk = jnp.where(incl, a_bk[:, CHUNK:, GRP_W:], 0.0)

    e = _unit_lower_inverse_minus_eye(lb, t_idx, s_idx)
    lmkv = _pair_nn(jnp.concatenate([lk, mk], axis=1), vp)
    lkv, mkv = lmkv[:, :CHUNK], lmkv[:, CHUNK:]
    ex = _bdot(e, jnp.concatenate([_bd(kkd), _bd(lkv)], axis=2), 2, 1)
    wmat = kkd + ex[:, :, :GRP_W]
    uloc = -(lkv + ex[:, :, GRP_W:])
    pmat = jnp.where(eye, gamma_c, 0.0) - jnp.where(same_head, _bdot(wmat, bd, 1, 1), 0.0)
    qmat = jnp.where(same_head, _bdot(jnp.concatenate([uloc, vp], axis=1),
                                      jnp.concatenate([bd, kd], axis=1), 1, 1), 0.0)
    mbx = _bdot(mb, jnp.concatenate([_bd(wmat), _bd(uloc)], axis=2), 2, 1)
    reff = rd - mbx[:, :, :GRP_W]
    yloc = mbx[:, :, GRP_W:] + mkv

    state = s_ref[...]
    y = _merge_pairs(yloc + _bdot(reff, state, 2, 2))
    s_ref[...] = _bdot(state, pmat, 2, 1) + qmat

    inv_n = 1.0 / HEAD_DIM
    mean = _head_sum(y, ones_pair) * inv_n
    yc = y - mean
    var = _head_sum(yc * yc, ones_pair) * inv_n
    yn = yc * lax.rsqrt(var + RWKV_GN_EPS) * w["rwkv_ln_w"][...] + w["rwkv_ln_b"][...]
    bonus = _head_sum(r * k * w["rwkv_r_k"][...], ones_pair) * v
    gate = _zrows(zin_ref, C_GRW, C_GRW + BRANCH_W)
    return (yn + bonus) * _silu(gate)


def _first_segment_rows(width):
    assert SEGS_PER_SEQ == 2
    return lax.broadcasted_iota(jnp.int32, (NSEG, width), 0) % SEGS_PER_SEQ == 0


def _s5_expand(zin_ref, w, xre_ref, xim_ref):
    u = _zrows(zin_ref, C_US5, C_US5 + BRANCH_W)
    u_il = jnp.dot(w["perm"][...], u.astype(BF16), preferred_element_type=F32).astype(BF16)
    for blk in range(S5_BLOCKS):
        bu = jnp.dot(u_il[:, blk * S5_BLK_IN:(blk + 1) * S5_BLK_IN], w["s5_b"][blk],
                     preferred_element_type=F32)
        xre_ref[:, blk * S5_BLK_ST:(blk + 1) * S5_BLK_ST] = bu[:, :S5_BLK_ST]
        xim_ref[:, blk * S5_BLK_ST:(blk + 1) * S5_BLK_ST] = bu[:, S5_BLK_ST:]


def _segment_scans(w, xre_ref, xim_ref, h_ref, cp_ref):
    n_loops = S5_N // SCAN_LANES
    for lb in range(n_loops):
        cols = slice(lb * SCAN_LANES, (lb + 1) * SCAN_LANES)
        lam_re = jnp.broadcast_to(w["s5_pow_re"][0:1, cols], (NSEG, SCAN_LANES))
        lam_im = jnp.broadcast_to(w["s5_pow_im"][0:1, cols], (NSEG, SCAN_LANES))
        with_lru = lb == n_loops - 1

        def step(t, carry, cols=cols, lam_re=lam_re, lam_im=lam_im, with_lru=with_lru):
            rws = pl.ds(pl.multiple_of(t * NSEG, NSEG), NSEG)
            sr, si = carry[:2]
            nr = lam_re * sr - lam_im * si + xre_ref[rws, cols]
            ni = lam_re * si + lam_im * sr + xim_ref[rws, cols]
            xre_ref[rws, cols] = nr
            xim_ref[rws, cols] = ni
            if not with_lru:
                return nr, ni
            hs, cs = carry[2:]
            a_t = cp_ref[rws, :]
            hs = a_t * hs + h_ref[rws, :]
            cs = a_t * cs
            h_ref[rws, :] = hs
            cp_ref[rws, :] = cs
            return nr, ni, hs, cs

        zero = jnp.zeros((NSEG, SCAN_LANES), F32)
        init = (zero, zero)
        if with_lru:
            init += (jnp.zeros((NSEG, BRANCH_W), F32), jnp.ones((NSEG, BRANCH_W), F32))
        lax.fori_loop(0, SEG, step, init, unroll=SCAN_UNROLL)


def _s5_chain(w, xre_ref, xim_ref, st_re_ref, st_im_ref, start_re_ref, start_im_ref):
    last = pl.ds(TL - NSEG, NSEG)
    end_re = xre_ref[last, :]
    end_im = xim_ref[last, :]
    seg_re = w["s5_pow_re"][SEG - 1:SEG, :]
    seg_im = w["s5_pow_im"][SEG - 1:SEG, :]
    first = _first_segment_rows(S5_N)
    c_re = st_re_ref[...]
    c_im = st_im_ref[...]
    e0_re = end_re + seg_re * c_re - seg_im * c_im
    e0_im = end_im + seg_re * c_im + seg_im * c_re
    start_re = jnp.where(first, c_re, pltpu.roll(e0_re, 1, axis=0))
    start_im = jnp.where(first, c_im, pltpu.roll(e0_im, 1, axis=0))
    start_re_ref[...] = start_re
    start_im_ref[...] = start_im
    e1_re = end_re + seg_re * start_re - seg_im * start_im
    e1_im = end_im + seg_re * start_im + seg_im * start_re
    st_re_ref[...] = jnp.where(first, pltpu.roll(e1_re, NSEG - 1, axis=0), 0.0)
    st_im_ref[...] = jnp.where(first, pltpu.roll(e1_im, NSEG - 1, axis=0), 0.0)


def _s5_output(zin_ref, w, xre_ref, xim_ref, start_re_ref, start_im_ref):
    u = _zrows(zin_ref, C_US5, C_US5 + BRANCH_W)
    ys = []
    grp = (TL // BF16_ROWS, BF16_ROWS, S5_BLK_ST)
    twice = BF16_ROWS // NSEG
    for blk in range(S5_BLOCKS):
        cols = slice(blk * S5_BLK_ST, (blk + 1) * S5_BLK_ST)
        s_re = jnp.concatenate([start_re_ref[:, cols]] * twice, axis=0).astype(BF16)[None]
        s_im = jnp.concatenate([start_im_ref[:, cols]] * twice, axis=0).astype(BF16)[None]
        p_re = w["s5_powb_re"][:, cols].reshape(grp)
        p_im = w["s5_powb_im"][:, cols].reshape(grp)
        x_re = xre_ref[:, cols].astype(BF16).reshape(grp) + (p_re * s_re - p_im * s_im)
        x_im = xim_ref[:, cols].astype(BF16).reshape(grp) + (p_re * s_im + p_im * s_re)
        x_ri = jnp.concatenate([x_re.reshape(TL, S5_BLK_ST), x_im.reshape(TL, S5_BLK_ST)], axis=-1)
        ys.append(jnp.dot(x_ri, w["s5_c"][blk], preferred_element_type=F32))
    y_il = jnp.concatenate(ys, axis=-1)
    y = _dot(w["perm_t"][...], y_il) + w["s5_d"][...] * u
    zg = _gelu_tanh(y)
    zg = zg * _sigmoid(_dot(zg, w["s5_glu_w"][...]) + w["s5_glu_b"][...])
    gate = _zrows(zin_ref, C_GS5, C_GS5 + BRANCH_W)
    return zg * _silu(gate)


def _lru_prep(zin_ref, w, h_ref, cp_ref):
    conv_w = w["lru_conv_w"][...]
    xc = w["lru_conv_b"][...]
    for back in range(CONV_WIDTH):
        tap = CONV_WIDTH - 1 - back
        xc = xc + _zrows(zin_ref, C_XLRU, C_XLRU + BRANCH_W, back) * conv_w[tap:tap + 1, :]
    xc = _dot(w["perm"][...], xc)
    gates = [_dot(xc[:, j * PAIR_W:(j + 1) * PAIR_W], w["lru_gates"][j]) for j in range(PAIRS)]
    r = _sigmoid(jnp.concatenate([g[:, :PAIR_W] for g in gates], axis=-1) + w["lru_ba"][...])
    i = _sigmoid(jnp.concatenate([g[:, PAIR_W:] for g in gates], axis=-1) + w["lru_bx"][...])
    log2_a = (-LRU_C * LOG2_E * _softplus(-w["lru_lambda"][...])) * r
    a = jnp.exp2(log2_a)
    one_minus_a2 = 1.0 - a * a
    mult = jnp.where(one_minus_a2 > 0.0, one_minus_a2 * lax.rsqrt(one_minus_a2), 0.0)
    h_ref[...] = mult * (i * xc)
    cp_ref[...] = a


def _lru_chain(h_ref, cp_ref, st_ref, start_ref):
    last = pl.ds(TL - NSEG, NSEG)
    end_h = h_ref[last, :]
    end_cp = cp_ref[last, :]
    first = _first_segment_rows(BRANCH_W)
    carried = st_ref[...]
    start = jnp.where(first, carried, pltpu.roll(end_h + end_cp * carried, 1, axis=0))
    start_ref[...] = start
    st_ref[...] = jnp.where(first, pltpu.roll(end_h + end_cp * start, NSEG - 1, axis=0), 0.0)


def _lru_output(zin_ref, w, h_ref, cp_ref, start_ref):
    start = start_ref[...]
    h3 = h_ref[...].reshape(SEG, NSEG, BRANCH_W) + cp_ref[...].reshape(SEG, NSEG, BRANCH_W) * start[None]
    h = _dot(w["perm_t"][...], h3.reshape(TL, BRANCH_W))
    gate = _zrows(zin_ref, C_GLRU, C_GLRU + BRANCH_W)
    return h * _silu(gate)


class _Rows:
    def __init__(self, ref, start, n=1):
        self.ref, self.start, self.n = ref, start, n

    def __getitem__(self, idx):
        assert idx is Ellipsis
        return self.ref[self.start:self.start + self.n, :]


def _layer_body(names, is_last, *refs):
    n_in = len(names)
    w = dict(zip(names, refs[:n_in]))
    for table, (members, _) in VEC_TABLES.items():
        start = 0
        for name, n_rows in members:
            w[name] = _Rows(w[table], start, n_rows)
            start += n_rows
    out_ref = refs[n_in]
    (zin_ref, s_ref, xre_ref, xim_ref, s5_re_ref, s5_im_ref, s5_start_re_ref, s5_start_im_ref,
     lru_h_ref, lru_cp_ref, lru_st_ref, lru_start_ref) = refs[n_in + 1:]

    @pl.when(pl.program_id(0) == 0)
    def _():
        for b in range(NB):
            zin_ref[b * ZB:b * ZB + CARRY, :] = jnp.zeros((CARRY, D_IN), F32)
        s_ref[...] = jnp.zeros_like(s_ref)
        s5_re_ref[...] = jnp.zeros_like(s5_re_ref)
        s5_im_ref[...] = jnp.zeros_like(s5_im_ref)
        lru_st_ref[...] = jnp.zeros_like(lru_st_ref)

    h = w["h"][...].reshape(TL, D_MODEL)
    xn = _rmsnorm(h, w["norm_g"][...]).astype(BF16)
    for c0, c1 in IN_COL_BLKS:
        z = jnp.dot(xn, w["w_in"][:, c0:c1], preferred_element_type=F32)
        for b in range(NB):
            zin_ref[pl.ds(b * ZB + CARRY, CHUNK), c0:c1] = z[b * CHUNK:(b + 1) * CHUNK]

    y_rw = _rwkv_group(zin_ref, w, s_ref)
    _s5_expand(zin_ref, w, xre_ref, xim_ref)
    _lru_prep(zin_ref, w, lru_h_ref, lru_cp_ref)
    _segment_scans(w, xre_ref, xim_ref, lru_h_ref, lru_cp_ref)
    _s5_chain(w, xre_ref, xim_ref, s5_re_ref, s5_im_ref, s5_start_re_ref, s5_start_im_ref)
    _lru_chain(lru_h_ref, lru_cp_ref, lru_st_ref, lru_start_ref)
    y_s5 = _s5_output(zin_ref, w, xre_ref, xim_ref, s5_start_re_ref, s5_start_im_ref)
    y_lru = _lru_output(zin_ref, w, lru_h_ref, lru_cp_ref, lru_start_ref)
    for b in range(NB):
        zin_ref[b * ZB:b * ZB + CARRY, :] = zin_ref[b * ZB + CHUNK:(b + 1) * ZB, :]

    y_mix = jnp.concatenate([y_rw.astype(BF16), y_s5.astype(BF16), y_lru.astype(BF16)], axis=-1)
    h = h + jnp.dot(y_mix, w["w_out"][...], preferred_element_type=F32)
    e = _rmsnorm(_dot(w["p"][...].reshape(TL, D_PLE), w["ple_w"][...]), w["ple_norm_g"][...])
    h = h + e * _sigmoid(_dot(h, w["ple_gate_w"][...]))
    if is_last:
        h = _rmsnorm(h, w["final_norm_g"][...])
    out_ref[...] = h.reshape(NB, CHUNK, D_MODEL)


def _block_diag(blocks):
    n, a, b = blocks.shape[-3:]
    eye = jnp.eye(n, dtype=blocks.dtype)
    return jnp.einsum("...nab,nm->...namb", blocks, eye).reshape(blocks.shape[:-3] + (n * a, n * b))


def _constants():
    nat = np.arange(TL)
    interleaved_row = (nat % SEG) * NSEG + nat // SEG
    perm = np.zeros((TL, TL), np.float32)
    perm[interleaved_row, nat] = 1.0
    t = nat[:, None]
    s = nat[None, :]
    tri = ((t // CHUNK) == (s // CHUNK)) & (s <= t)
    lane = np.arange(PAIR_W)
    ones_pair = (lane[:, None] // HEAD_DIM) == (lane[None, :] // HEAD_DIM)
    as_bf16 = lambda m: jnp.asarray(m.astype(np.float32), dtype=BF16)
    return as_bf16(perm), as_bf16(perm.T), as_bf16(tri), as_bf16(ones_pair)


def _s5_params(a_re, a_im, log_dt, b_re, b_im, c_re, c_im):
    depth = a_re.shape[0]
    a_re, a_im = a_re.astype(F32), a_im.astype(F32)
    dt = jnp.exp(log_dt.astype(F32))[..., None]
    mag = jnp.exp(a_re * dt)
    lr, li = mag * jnp.cos(a_im * dt), mag * jnp.sin(a_im * dt)
    den = a_re * a_re + a_im * a_im
    f_re = ((lr - 1.0) * a_re + li * a_im) / den
    f_im = (li * a_re - (lr - 1.0) * a_im) / den
    b_re, b_im = b_re.astype(F32), b_im.astype(F32)
    bb_re = f_re[..., None] * b_re - f_im[..., None] * b_im
    bb_im = f_re[..., None] * b_im + f_im[..., None] * b_re
    steps = jnp.arange(1, SEG + 1, dtype=F32)[:, None, None]
    pmag = jnp.exp((a_re * dt)[:, None] * steps)
    ang = (a_im * dt)[:, None] * steps
    pow_re = (pmag * jnp.cos(ang)).reshape(depth, SEG, S5_N)
    pow_im = (pmag * jnp.sin(ang)).reshape(depth, SEG, S5_N)
    gpb = S5_GROUPS // S5_BLOCKS

    def blocks(x, rows, cols):
        return _block_diag(x.reshape(depth, S5_BLOCKS, gpb, rows, cols))

    s5_b = jnp.concatenate([blocks(jnp.swapaxes(bb_re, 2, 3), S5_GROUP, S5_STATE),
                            blocks(jnp.swapaxes(bb_im, 2, 3), S5_GROUP, S5_STATE)], axis=-1).astype(BF16)
    s5_c = jnp.concatenate([blocks(jnp.swapaxes(c_re.astype(F32), 2, 3), S5_STATE, S5_GROUP),
                            -blocks(jnp.swapaxes(c_im.astype(F32), 2, 3), S5_STATE, S5_GROUP)], axis=-2).astype(BF16)
    return s5_b, s5_c, pow_re, pow_im


def _layer_call(h, p, layer, stacked, shared, is_last):
    bsz, seq, _ = h.shape
    assert bsz == NB and seq % CHUNK == 0
    names = ["h", "p"] + list(stacked.keys()) + list(shared.keys())
    arrays = [h, p] + list(stacked.values()) + list(shared.values())
    in_specs = [
        pl.BlockSpec((NB, CHUNK, D_MODEL), lambda j: (0, j, 0)),
        pl.BlockSpec((None, NB, CHUNK, D_PLE), lambda j: (layer, 0, j, 0)),
    ]
    for arr in stacked.values():
        index = (layer,) + (0,) * (arr.ndim - 1)
        in_specs.append(pl.BlockSpec((None,) + arr.shape[1:], lambda j, index=index: index,
                                     pipeline_mode=pl.Buffered(1)))
    for arr in shared.values():
        index = (0,) * arr.ndim
        in_specs.append(pl.BlockSpec(arr.shape, lambda j, index=index: index,
                                     pipeline_mode=pl.Buffered(1)))
    scratch = [
        pltpu.VMEM((NB * ZB, D_IN), F32),
        pltpu.VMEM((GROUPS * NB, GRP_W, GRP_W), F32),
        pltpu.VMEM((TL, S5_N), F32),
        pltpu.VMEM((TL, S5_N), F32),
        pltpu.VMEM((NSEG, S5_N), F32),
        pltpu.VMEM((NSEG, S5_N), F32),
        pltpu.VMEM((NSEG, S5_N), F32),
        pltpu.VMEM((NSEG, S5_N), F32),
        pltpu.VMEM((TL, BRANCH_W), F32),
        pltpu.VMEM((TL, BRANCH_W), F32),
        pltpu.VMEM((NSEG, BRANCH_W), F32),
        pltpu.VMEM((NSEG, BRANCH_W), F32),
    ]
    return pl.pallas_call(
        functools.partial(_layer_body, tuple(names), is_last),
        grid=(seq // CHUNK,),
        in_specs=in_specs,
        out_specs=pl.BlockSpec((NB, CHUNK, D_MODEL), lambda j: (0, j, 0)),
        out_shape=jax.ShapeDtypeStruct(h.shape, F32),
        scratch_shapes=scratch,
        compiler_params=pltpu.CompilerParams(
            dimension_semantics=("arbitrary",),
            vmem_limit_bytes=VMEM_LIMIT),
        name="hybrid_layer",
    )(*arrays)


def kernel(x, p, norm_g, w_in, rwkv_mu, rwkv_w0, rwkv_w2, rwkv_a0, rwkv_a2, rwkv_k_k, rwkv_k_a, rwkv_r_k, rwkv_ln_w, rwkv_ln_b, s5_a_re, s5_a_im, s5_log_dt, s5_b_re, s5_b_im, s5_c_re, s5_c_im, s5_d, s5_glu_w, s5_glu_b, lru_conv_w, lru_conv_b, lru_wa, lru_ba, lru_wx, lru_bx, lru_lambda, w_out, ple_w, ple_norm_g, ple_gate_w, final_norm_g):
    depth = w_in.shape[0]
    perm, perm_t, tri, ones_pair = _constants()
    shared = {"ones_pair": ones_pair, "chunk_tri": tri, "perm": perm, "perm_t": perm_t}

    s5_b, s5_c, pow_re, pow_im = _s5_params(s5_a_re, s5_a_im, s5_log_dt, s5_b_re, s5_b_im, s5_c_re, s5_c_im)
    zero = jnp.zeros((depth, LORA, BRANCH_W), F32)
    lora = jnp.concatenate([jnp.concatenate([rwkv_w2, zero], axis=2),
                            jnp.concatenate([zero, rwkv_a2], axis=2)], axis=1)
    pair_blocks = lambda m: _block_diag(m.reshape(depth, PAIRS, 2, LRU_BLOCK_DIM, LRU_BLOCK_DIM))
    vectors = {
        "rwkv_w0": rwkv_w0, "rwkv_a0": rwkv_a0, "rwkv_k_k": rwkv_k_k, "rwkv_k_a": rwkv_k_a,
        "rwkv_r_k": rwkv_r_k.reshape(depth, BRANCH_W), "rwkv_ln_w": rwkv_ln_w, "rwkv_ln_b": rwkv_ln_b,
        "s5_d": s5_d, "s5_glu_b": s5_glu_b, "lru_conv_b": lru_conv_b, "lru_ba": lru_ba, "lru_bx": lru_bx,
        "lru_lambda": lru_lambda, "lru_conv_w": lru_conv_w,
        "norm_g": norm_g, "ple_norm_g": ple_norm_g,
        "final_norm_g": jnp.broadcast_to(final_norm_g, (depth, D_MODEL)),
    }
    stacked = {
        table: jnp.concatenate([vectors[name].astype(F32).reshape(depth, rows, width) for name, rows in members], axis=1)
        for table, (members, width) in VEC_TABLES.items()
    }
    stacked.update({
        "rwkv_mu": rwkv_mu.astype(F32)[:, None, :],
        "w_in": w_in.astype(BF16),
        "rwkv_lora": lora.astype(BF16),
        "s5_b": s5_b,
        "s5_c": s5_c,
        "s5_pow_re": pow_re,
        "s5_pow_im": pow_im,
        "s5_powb_re": jnp.repeat(pow_re, NSEG, axis=1).astype(BF16),
        "s5_powb_im": jnp.repeat(pow_im, NSEG, axis=1).astype(BF16),
        "s5_glu_w": s5_glu_w.astype(BF16),
        "lru_gates": jnp.concatenate([pair_blocks(lru_wa), pair_blocks(lru_wx)], axis=-1).astype(BF16),
        "w_out": w_out.astype(BF16),
        "ple_w": ple_w.astype(BF16),
        "ple_gate_w": ple_gate_w.astype(BF16),
    })
    h = x.astype(F32)
    p = p.astype(F32)
    for i in range(depth):
        h = _layer_call(h, p, i, stacked, shared, is_last=(i == depth - 1))
    return h.astype(x.dtype)
```
